```python
import jax, jax.numpy as jnp
from jax import lax
import numpy as np

D_MODEL = 1024
BATCH = 2
SEQ = 8192
DEPTH = 2

CTX_LEN = 256
GRID_W = 64
HEAD_DIM = 64
N_HEADS = D_MODEL // HEAD_DIM
CONV_CH = D_MODEL // 2
FOURIER_CH = D_MODEL // 2
FOURIER_GROUP = 64
CONV_WIDTH = 3
WIN_ROWS = 8
WIN_COLS = 16
D_FF = 4 * D_MODEL
N_MOD = 6
EPS = 1e-6

kernel_name = "hybrid_conv_fourier_natten_dit_trunk"


def _rms(x, g):
    xf = x.astype(jnp.float32)
    y = xf * lax.rsqrt(jnp.mean(xf * xf, axis=-1, keepdims=True) + EPS)
    return y.astype(x.dtype) * g


def _modulate(h, shift, scale):
    return h * (1 + scale) + shift


def _conv3(z, w):
    zp = jnp.pad(z, ((0, 0), (1, 1), (0, 0)))
    return zp[:, :-2] * w[0] + zp[:, 1:-1] * w[1] + zp[:, 2:] * w[2]


def _conv_fourier_mixer(h, w_in, conv_w, w_out):
    b, L, _ = h.shape
    u = h @ w_in
    a_x, a_c, a_b, f = jnp.split(u, [CONV_CH, 2 * CONV_CH, 3 * CONV_CH], axis=-1)
    a_y = a_b * _conv3(a_c * a_x, conv_w)
    fg = f.reshape(b, L, FOURIER_CH // FOURIER_GROUP, FOURIER_GROUP).astype(jnp.float32)
    f_y = jnp.fft.fftn(fg, axes=(1, 3), norm="ortho").real
    f_y = f_y.reshape(b, L, FOURIER_CH).astype(h.dtype)
    return jnp.concatenate([a_y, f_y], axis=-1) @ w_out


def _heads(h, w, g=None):
    b, n, _ = h.shape
    t = (h @ w).reshape(b, n, -1, HEAD_DIM)
    return t if g is None else _rms(t, g)


def _neighbourhood_attention(q, k, v, k_ctx, v_ctx, rpb):
    b, n, h, d = q.shape
    rows = n // GRID_W
    kr = min(WIN_ROWS, rows)
    qg = q.reshape(b, rows, GRID_W, h, d)
    kg = k.reshape(b, rows, GRID_W, h, d)
    vg = v.reshape(b, rows, GRID_W, h, d)
    cols = jnp.arange(GRID_W)
    col_idx = jnp.clip(cols - WIN_COLS // 2, 0, GRID_W - WIN_COLS)[:, None] + jnp.arange(WIN_COLS)
    col_off = col_idx - cols[:, None] + (WIN_COLS - 1)
    bias_cols = rpb[:, :, col_off]
    scale = d ** -0.5
    n_loc = kr * WIN_COLS

    def row_block(r):
        rs = jnp.clip(r - kr // 2, 0, rows - kr)
        q_r = lax.dynamic_index_in_dim(qg, r, axis=1, keepdims=False)
        k_win = lax.dynamic_slice_in_dim(kg, rs, kr, axis=1)[:, :, col_idx]
        v_win = lax.dynamic_slice_in_dim(vg, rs, kr, axis=1)[:, :, col_idx]
        row_off = rs + jnp.arange(kr) - r + (WIN_ROWS - 1)
        bias = jnp.take(bias_cols, row_off, axis=1)
        s_loc = jnp.einsum('bwhd,brwchd->bhwrc', q_r, k_win).astype(jnp.float32) * scale
        s_loc = s_loc + jnp.transpose(bias, (0, 2, 1, 3)).astype(jnp.float32)[None]
        s_ctx = jnp.einsum('bwhd,bnhd->bhwn', q_r, k_ctx).astype(jnp.float32) * scale
        s = jnp.concatenate([s_loc.reshape(b, h, GRID_W, n_loc), s_ctx], axis=-1)
        p = jax.nn.softmax(s, axis=-1).astype(v.dtype)
        p_loc = p[..., :n_loc].reshape(b, h, GRID_W, kr, WIN_COLS)
        p_ctx = p[..., n_loc:]
        return (jnp.einsum('bhwrc,brwchd->bwhd', p_loc, v_win)
                + jnp.einsum('bhwn,bnhd->bwhd', p_ctx, v_ctx))

    o = lax.map(row_block, jnp.arange(rows))
    return jnp.moveaxis(o, 0, 1).reshape(b, n, h * d)


def _context_attention(q, k, v):
    b, n, h, d = q.shape
    s = jnp.einsum('bqhd,bkhd->bhqk', q, k).astype(jnp.float32) * (d ** -0.5)
    p = jax.nn.softmax(s, axis=-1).astype(v.dtype)
    return jnp.einsum('bhqk,bkhd->bqhd', p, v).reshape(b, n, h * d)


def _mlp(h, w1, w2):
    return jnp.square(jax.nn.relu(h @ w1)) @ w2


def setup_inputs(seed: int = 0) -> dict:
    key = jax.random.key(seed)
    ks = jax.random.split(key, 20)
    n_even = (DEPTH + 1) // 2
    n_odd = DEPTH // 2
    f32 = jnp.float32

    def nrm(k, shape, s):
        return jax.random.normal(k, shape, f32) * s

    return {
        "x": nrm(ks[0], (BATCH, SEQ, D_MODEL), 1.0),
        "c": nrm(ks[1], (BATCH, D_MODEL), 1.0),
        "ctx": nrm(ks[2], (BATCH, CTX_LEN, D_MODEL), 1.0),
        "c_ctx": nrm(ks[3], (D_MODEL,), 1.0),
        "ada_w": nrm(ks[4], (DEPTH, D_MODEL, N_MOD * D_MODEL), 0.5 * D_MODEL ** -0.5),
        "ada_b": nrm(ks[5], (DEPTH, N_MOD * D_MODEL), 0.02),
        "norm_mix_g": 1.0 + nrm(ks[6], (DEPTH, D_MODEL), 0.02),
        "norm_mlp_g": 1.0 + nrm(ks[7], (DEPTH, D_MODEL), 0.02),
        "mlp_w1": nrm(ks[8], (DEPTH, D_MODEL, D_FF), D_MODEL ** -0.5),
        "mlp_w2": nrm(ks[9], (DEPTH, D_FF, D_MODEL), D_FF ** -0.5),
        "ab_w_in": nrm(ks[10], (n_even, D_MODEL, 3 * CONV_CH + FOURIER_CH), D_MODEL ** -0.5),
        "ab_conv_w": nrm(ks[11], (n_even, CONV_WIDTH, CONV_CH), CONV_WIDTH ** -0.5),
        "ab_w_out": nrm(ks[12], (n_even, CONV_CH + FOURIER_CH, D_MODEL), (CONV_CH + FOURIER_CH) ** -0.5),
        "na_w_qkv": nrm(ks[13], (n_odd, D_MODEL, 3 * N_HEADS * HEAD_DIM), D_MODEL ** -0.5),
        "na_q_g": 1.0 + nrm(ks[14], (n_odd, HEAD_DIM), 0.02),
        "na_k_g": 1.0 + nrm(ks[15], (n_odd, HEAD_DIM), 0.02),
        "na_rpb": nrm(ks[16], (n_odd, N_HEADS, 2 * WIN_ROWS - 1, 2 * WIN_COLS - 1), 0.1),
        "na_w_out": nrm(ks[17], (n_odd, N_HEADS * HEAD_DIM, D_MODEL), (N_HEADS * HEAD_DIM) ** -0.5),
    }


def reference(x, c, ctx, c_ctx, ada_w, ada_b, norm_mix_g, norm_mlp_g, mlp_w1, mlp_w2,
              ab_w_in, ab_conv_w, ab_w_out, na_w_qkv, na_q_g, na_k_g, na_rpb, na_w_out):
    hd = N_HEADS * HEAD_DIM
    cond_x = jax.nn.silu(c)
    cond_c = jax.nn.silu(c_ctx)[None]
    for i in range(DEPTH):
        last = i == DEPTH - 1
        j = i // 2
        sx1, cx1, gx1, sx2, cx2, gx2 = jnp.split((cond_x @ ada_w[i] + ada_b[i])[:, None, :], N_MOD, axis=-1)
        sc1, cc1, gc1, sc2, cc2, gc2 = jnp.split((cond_c @ ada_w[i] + ada_b[i])[:, None, :], N_MOD, axis=-1)
        hx = _modulate(_rms(x, norm_mix_g[i]), sx1, cx1)
        if i % 2 == 0:
            x = x + gx1 * _conv_fourier_mixer(hx, ab_w_in[j], ab_conv_w[j], ab_w_out[j])
            if not last:
                hc = _modulate(_rms(ctx, norm_mix_g[i]), sc1, cc1)
                ctx = ctx + gc1 * _conv_fourier_mixer(hc, ab_w_in[j], ab_conv_w[j], ab_w_out[j])
        else:
            w = na_w_qkv[j]
            hc = _modulate(_rms(ctx, norm_mix_g[i]), sc1, cc1)
            q = _heads(hx, w[:, :hd], na_q_g[j])
            k = _heads(hx, w[:, hd:2 * hd], na_k_g[j])
            v = _heads(hx, w[:, 2 * hd:])
            k_c = _heads(hc, w[:, hd:2 * hd], na_k_g[j])
            v_c = _heads(hc, w[:, 2 * hd:])
            x = x + gx1 * (_neighbourhood_attention(q, k, v, k_c, v_c, na_rpb[j]) @ na_w_out[j])
            if not last:
                q_c = _heads(hc, w[:, :hd], na_q_g[j])
                ctx = ctx + gc1 * (_context_attention(q_c, k_c, v_c) @ na_w_out[j])
        x = x + gx2 * _mlp(_modulate(_rms(x, norm_mlp_g[i]), sx2, cx2), mlp_w1[i], mlp_w2[i])
        if not last:
            ctx = ctx + gc2 * _mlp(_modulate(_rms(ctx, norm_mlp_g[i]), sc2, cc2), mlp_w1[i], mlp_w2[i])
    return x
```

```python
import functools

import numpy as np
import jax
import jax.numpy as jnp
from jax import lax
from jax.experimental import pallas as pl
from jax.experimental.pallas import tpu as pltpu

F32 = jnp.float32
BF16 = jnp.bfloat16

D = 1024
DEPTH = 2
SEQ = 8192
CTX = 256
GRID_W = 64
GRID_ROWS = SEQ // GRID_W
HEAD_DIM = 64
N_HEADS = 16
CONV_CH = 512
FOUR_CH = 512
FOUR_GROUP = 64
WIN_ROWS = 8
WIN_COLS = 16
D_FF = 4 * D
N_MOD = 6
EPS = 1e-6

TM = 512
HALO = 8
FF_CHUNK = 1024
VMEM_LIMIT = 56 * 1024 * 1024

DFT_A = 32
DFT_B = SEQ // DFT_A
SUB = 8

QCOLS = 16
NGRP = GRID_W // QCOLS
KCOLS = 32
KC0 = tuple(int(np.clip(QCOLS * g - 8, 0, GRID_W - KCOLS)) for g in range(NGRP))
RB = 8
NRB = GRID_ROWS // RB
WROWS = 16
KROWS_PER_TILE = 128 // KCOLS
NEG = -1e30


def _cparams(n_axes):
    return pltpu.CompilerParams(dimension_semantics=("arbitrary",) * n_axes,
                                vmem_limit_bytes=VMEM_LIMIT)


def _const_spec(shape):
    nd = len(shape)
    return pl.BlockSpec(shape, lambda *_: (0,) * nd)


def _bf16_const(a):
    return jnp.asarray(a, F32).astype(BF16)


def _rms_mod(x, g, shift, scale):
    ms = jnp.mean(x * x, axis=-1, keepdims=True)
    y = x * lax.rsqrt(ms + EPS)
    return (y * g) * (1.0 + scale) + shift


def _mod_rows(mod_ref, row, ks):
    return [mod_ref[pl.ds(row, 1), k * D:(k + 1) * D] for k in ks]


def _mlp(x1, g, shift, scale, w1_ref, w2_ref):
    h = _rms_mod(x1, g, shift, scale).astype(BF16)
    acc = jnp.zeros(x1.shape, F32)
    for c in range(D_FF // FF_CHUNK):
        a = jnp.dot(h, w1_ref[:, c * FF_CHUNK:(c + 1) * FF_CHUNK], preferred_element_type=F32)
        a = jnp.maximum(a, 0.0)
        a = (a * a).astype(BF16)
        acc = acc + jnp.dot(a, w2_ref[c * FF_CHUNK:(c + 1) * FF_CHUNK, :], preferred_element_type=F32)
    return acc


def _mod_body(cond_ref, w_ref, b_ref, o_ref):
    c = cond_ref[...]
    s = c * jax.nn.sigmoid(c)
    o_ref[...] = jnp.dot(s.astype(BF16), w_ref[...].astype(BF16), preferred_element_type=F32) + b_ref[...]


def _modulation(cond8, ada_w, ada_b):
    nt = N_MOD * D // D
    return pl.pallas_call(
        _mod_body,
        grid=(DEPTH, nt),
        in_specs=[_const_spec((8, D)),
                  pl.BlockSpec((None, D, D), lambda l, n: (l, 0, n)),
                  pl.BlockSpec((None, 1, D), lambda l, n: (l, 0, n))],
        out_specs=pl.BlockSpec((None, 8, D), lambda l, n: (l, 0, n)),
        out_shape=jax.ShapeDtypeStruct((DEPTH, 8, N_MOD * D), F32),
        compiler_params=_cparams(2),
        name="adaln_mod",
    )(cond8, ada_w, ada_b.reshape(DEPTH, 1, N_MOD * D))


def _inproj_body(mod_row, seq_len, tm, xp_ref, x_ref, xn_ref, mod_ref, g_ref, w_ref, cw_ref,
                 ay_ref, f_ref):
    b = pl.program_id(0)
    t = pl.program_id(1)
    row = b if mod_row is None else mod_row
    xx = jnp.concatenate([xp_ref[...], x_ref[...], xn_ref[...]], axis=0)
    shift, scale = _mod_rows(mod_ref, row, (0, 1))
    h = _rms_mod(xx, g_ref[...], shift, scale).astype(BF16)
    u = jnp.dot(h, w_ref[...], preferred_element_type=F32)
    z = u[:, CONV_CH:2 * CONV_CH] * u[:, 0:CONV_CH]
    n = t * tm - HALO + lax.broadcasted_iota(jnp.int32, (tm + 2 * HALO, 1), 0)
    z = jnp.where((n >= 0) & (n < seq_len), z, 0.0)
    cw = cw_ref[...]
    zc = (z[HALO - 1:HALO - 1 + tm] * cw[0:1] + z[HALO:HALO + tm] * cw[1:2]
          + z[HALO + 1:HALO + 1 + tm] * cw[2:3])
    ay_ref[...] = (u[HALO:HALO + tm, 2 * CONV_CH:3 * CONV_CH] * zc).astype(BF16)
    f_ref[...] = u[HALO:HALO + tm, 3 * CONV_CH:]


def _inproj(x, mod, g, w_in_bf, conv_w8, mod_row, tm):
    bsz, seq_len, _ = x.shape
    nt = seq_len // tm
    nb8 = seq_len // HALO
    r8 = tm // HALO
    body = functools.partial(_inproj_body, mod_row, seq_len, tm)
    return pl.pallas_call(
        body,
        grid=(bsz, nt),
        in_specs=[pl.BlockSpec((None, HALO, D), lambda b, t: (b, jnp.maximum(t * r8 - 1, 0), 0)),
                  pl.BlockSpec((None, tm, D), lambda b, t: (b, t, 0)),
                  pl.BlockSpec((None, HALO, D), lambda b, t: (b, jnp.minimum((t + 1) * r8, nb8 - 1), 0)),
                  _const_spec((8, N_MOD * D)),
                  _const_spec((1, D)),
                  _const_spec((D, 3 * CONV_CH + FOUR_CH)),
                  _const_spec((8, CONV_CH))],
        out_specs=[pl.BlockSpec((None, tm, CONV_CH), lambda b, t: (b, t, 0)),
                   pl.BlockSpec((None, tm, FOUR_CH), lambda b, t: (b, t, 0))],
        out_shape=[jax.ShapeDtypeStruct((bsz, seq_len, CONV_CH), BF16),
                   jax.ShapeDtypeStruct((bsz, seq_len, FOUR_CH), F32)],
        compiler_params=_cparams(2),
        name="mixer_in",
    )(x, x, x, mod, g, w_in_bf, conv_w8)


def _dft_consts():
    a = np.arange(DFT_A)
    s = np.arange(SUB)
    ang = 2.0 * np.pi * np.outer(a, a) / DFT_A
    eye = np.eye(SUB)
    re = np.einsum("va,ts->vtas", np.cos(ang), eye).reshape(DFT_A * SUB, DFT_A * SUB)
    im = np.einsum("va,ts->vtas", -np.sin(ang), eye).reshape(DFT_A * SUB, DFT_A * SUB)
    l1 = np.concatenate([re, im], axis=0)
    m = np.arange(DFT_B // SUB)
    bb = (SUB * m[:, None, None] + s[None, None, :])
    tang = 2.0 * np.pi * a[None, :, None] * bb / SEQ
    twr = np.cos(tang).reshape(DFT_B // SUB, DFT_A * SUB, 1)
    twi = (-np.sin(tang)).reshape(DFT_B // SUB, DFT_A * SUB, 1)
    u = np.arange(DFT_B)
    ang2 = 2.0 * np.pi * np.outer(u, u) / DFT_B
    c2, s2 = np.cos(ang2), np.sin(ang2)
    l2 = np.block([[c2, s2], [s2, -c2]])
    return l1, twr, twi, l2


def _dft_a_body(x_ref, l1_ref, twr_ref, twi_ref, br_ref, bi_ref):
    rows = DFT_A * SUB
    x = x_ref[...].reshape(rows, FOUR_CH).astype(BF16)
    a = jnp.dot(l1_ref[...], x, preferred_element_type=F32)
    ar, ai = a[:rows], a[rows:]
    reps = FOUR_CH // 128
    twr = jnp.concatenate([twr_ref[...]] * reps, axis=1)
    twi = jnp.concatenate([twi_ref[...]] * reps, axis=1)
    br_ref[...] = (ar * twr - ai * twi).reshape(DFT_A, SUB, FOUR_CH)
    bi_ref[...] = (ar * twi + ai * twr).reshape(DFT_A, SUB, FOUR_CH)


def _dft_b_body(br_ref, bi_ref, l2_ref, pq_ref):
    bm = jnp.concatenate([br_ref[...], bi_ref[...]], axis=0).astype(BF16)
    pq = jnp.dot(l2_ref[...], bm, preferred_element_type=F32)
    pq_ref[...] = jnp.concatenate([pq[:DFT_B], pq[DFT_B:]], axis=1).astype(BF16)


def _dft_positions(f):
    bsz = f.shape[0]
    l1, twr, twi, l2 = _dft_consts()
    nm = DFT_B // SUB
    rows = DFT_A * SUB
    twr_b = jnp.broadcast_to(jnp.asarray(twr, F32), (nm, rows, 128))
    twi_b = jnp.broadcast_to(jnp.asarray(twi, F32), (nm, rows, 128))
    f5 = f.reshape(bsz, DFT_A, nm, SUB, FOUR_CH)
    blk5 = pl.BlockSpec((None, DFT_A, None, SUB, FOUR_CH), lambda b, m: (b, 0, m, 0, 0))
    br, bi = pl.pallas_call(
        _dft_a_body,
        grid=(bsz, nm),
        in_specs=[blk5, _const_spec((2 * rows, rows)),
                  pl.BlockSpec((None, rows, 128), lambda b, m: (m, 0, 0)),
                  pl.BlockSpec((None, rows, 128), lambda b, m: (m, 0, 0))],
        out_specs=[blk5, blk5],
        out_shape=[jax.ShapeDtypeStruct((bsz, DFT_A, nm, SUB, FOUR_CH), F32)] * 2,
        compiler_params=_cparams(2),
        name="dft_stage_a",
    )(f5, _bf16_const(l1), twr_b, twi_b)
    br = br.reshape(bsz, DFT_A, DFT_B, FOUR_CH)
    bi = bi.reshape(bsz, DFT_A, DFT_B, FOUR_CH)
    blk4 = pl.BlockSpec((None, None, DFT_B, FOUR_CH), lambda b, v: (b, v, 0, 0))
    pq = pl.pallas_call(
        _dft_b_body,
        grid=(bsz, DFT_A),
        in_specs=[blk4, blk4, _const_spec((2 * DFT_B, 2 * DFT_B))],
        out_specs=pl.BlockSpec((None, None, DFT_B, 2 * FOUR_CH), lambda b, v: (b, v, 0, 0)),
        out_shape=jax.ShapeDtypeStruct((bsz, DFT_A, DFT_B, 2 * FOUR_CH), BF16),
        compiler_params=_cparams(2),
        name="dft_stage_b",
    )(br, bi, _bf16_const(l2))
    return jnp.transpose(pq, (0, 2, 1, 3)).reshape(bsz, SEQ, 2 * FOUR_CH)


def _dft_small_body(f_ref, lc_ref, pq_ref):
    n = f_ref.shape[0]
    pq = jnp.dot(lc_ref[...], f_ref[...].astype(BF16), preferred_element_type=F32)
    pq_ref[...] = jnp.concatenate([pq[:n], pq[n:]], axis=1).astype(BF16)


def _dft_positions_small(f):
    bsz, n, _ = f.shape
    k = np.arange(n)
    ang = 2.0 * np.pi * np.outer(k, k) / n
    lc = np.concatenate([np.cos(ang), np.sin(ang)], axis=0)
    return pl.pallas_call(
        _dft_small_body,
        grid=(bsz,),
        in_specs=[pl.BlockSpec((None, n, FOUR_CH), lambda b: (b, 0, 0)), _const_spec((2 * n, n))],
        out_specs=pl.BlockSpec((None, n, 2 * FOUR_CH), lambda b: (b, 0, 0)),
        out_shape=jax.ShapeDtypeStruct((bsz, n, 2 * FOUR_CH), BF16),
        compiler_params=_cparams(1),
        name="dft_small",
    )(f, _bf16_const(lc))


def _channel_mix_const(seq_len):
    k = np.arange(FOUR_GROUP)
    ang = 2.0 * np.pi * np.outer(k, k) / FOUR_GROUP
    ng = FOUR_CH // FOUR_GROUP
    bdc = np.kron(np.eye(ng), np.cos(ang))
    bds = np.kron(np.eye(ng), np.sin(ang))
    return np.concatenate([bdc, -bds], axis=0) / np.sqrt(seq_len * FOUR_GROUP)


def _mixout_mlp_body(mod_row, x_ref, ay_ref, pq_ref, mod_ref, g2_ref, mix_ref, wout_ref,
                     w1_ref, w2_ref, o_ref):
    b = pl.program_id(0)
    row = b if mod_row is None else mod_row
    gate1, shift2, scale2, gate2 = _mod_rows(mod_ref, row, (2, 3, 4, 5))
    fy = jnp.dot(pq_ref[...], mix_ref[...], preferred_element_type=F32)
    cat = jnp.concatenate([ay_ref[...], fy.astype(BF16)], axis=1)
    mo = jnp.dot(cat, wout_ref[...], preferred_element_type=F32)
    x1 = x_ref[...] + gate1 * mo
    o_ref[...] = x1 + gate2 * _mlp(x1, g2_ref[...], shift2, scale2, w1_ref, w2_ref)


def _mixout_mlp(x, ay, pq, mod, g2, mix_bf, wout_bf, w1_bf, w2_bf, mod_row, tm):
    bsz, seq_len, _ = x.shape
    row_spec = lambda w: pl.BlockSpec((None, tm, w), lambda b, t: (b, t, 0))
    body = functools.partial(_mixout_mlp_body, mod_row)
    return pl.pallas_call(
        body,
        grid=(bsz, seq_len // tm),
        in_specs=[row_spec(D), row_spec(CONV_CH), row_spec(2 * FOUR_CH),
                  _const_spec((8, N_MOD * D)), _const_spec((1, D)),
                  _const_spec((2 * FOUR_CH, FOUR_CH)), _const_spec((D, D)),
                  _const_spec((D, D_FF)), _const_spec((D_FF, D))],
        out_specs=row_spec(D),
        out_shape=jax.ShapeDtypeStruct((bsz, seq_len, D), F32),
        compiler_params=_cparams(2),
        name="mixer_out_mlp",
    )(x, ay, pq, mod, g2, mix_bf, wout_bf, w1_bf, w2_bf)


def _head_rms(t, bd_ref, gt):
    tt = (t * t).astype(BF16)
    w = bd_ref.shape[0]
    ms = jnp.concatenate(
        [jnp.dot(tt[:, j * w:(j + 1) * w], bd_ref[...], preferred_element_type=F32)
         for j in range(D // w)], axis=1)
    return t * lax.rsqrt(ms + EPS) * gt


def _qkv_common(mod_row, x_ref, mod_ref, g_ref, w_ref, bd_ref, qg_ref, kg_ref):
    b = pl.program_id(0)
    row = b if mod_row is None else mod_row
    shift, scale = _mod_rows(mod_ref, row, (0, 1))
    h = _rms_mod(x_ref[...], g_ref[...], shift, scale).astype(BF16)
    qkv = jnp.dot(h, w_ref[...], preferred_element_type=F32)
    q = _head_rms(qkv[:, 0:D], bd_ref, qg_ref[...]) * (HEAD_DIM ** -0.5)
    k = _head_rms(qkv[:, D:2 * D], bd_ref, kg_ref[...])
    v = qkv[:, 2 * D:]
    return q, k, v


def _qkv_grid_body(x_ref, mod_ref, g_ref, w_ref, bd_ref, qg_ref, kg_ref, q_ref, k_ref, v_ref):
    q, k, v = _qkv_common(None, x_ref, mod_ref, g_ref, w_ref, bd_ref, qg_ref, kg_ref)
    for rho in range(TM // GRID_W):
        for g in range(NGRP):
            q0 = rho * GRID_W + QCOLS * g
            q_ref[g, rho * QCOLS:(rho + 1) * QCOLS, :] = q[q0:q0 + QCOLS].astype(BF16)
            k0 = rho * GRID_W + KC0[g]
            k_ref[g, rho * KCOLS:(rho + 1) * KCOLS, :] = k[k0:k0 + KCOLS].astype(BF16)
            v_ref[g, rho * KCOLS:(rho + 1) * KCOLS, :] = v[k0:k0 + KCOLS].astype(BF16)


def _qkv_ctx_body(mod_row, x_ref, mod_ref, g_ref, w_ref, bd_ref, qg_ref, kg_ref, k_ref, v_ref):
    _, k, v = _qkv_common(mod_row, x_ref, mod_ref, g_ref, w_ref, bd_ref, qg_ref, kg_ref)
    k_ref[...] = k.astype(BF16)
    v_ref[...] = v.astype(BF16)


def _qkv_consts(q_g, k_g):
    w = 256
    bd = np.kron(np.eye(w // HEAD_DIM), np.ones((HEAD_DIM, HEAD_DIM))) / HEAD_DIM
    qg = jnp.tile(q_g, N_HEADS).reshape(1, D)
    kg = jnp.tile(k_g, N_HEADS).reshape(1, D)
    return _bf16_const(bd), qg, kg


def _qkv_in_specs(tm):
    return [pl.BlockSpec((None, tm, D), lambda b, t: (b, t, 0)),
            _const_spec((8, N_MOD * D)), _const_spec((1, D)), _const_spec((D, 3 * D)),
            _const_spec((256, 256)), _const_spec((1, D)), _const_spec((1, D))]


def _qkv_grid(x, mod, g, wqkv_bf, q_g, k_g):
    bsz = x.shape[0]
    bd, qg, kg = _qkv_consts(q_g, k_g)
    nq = TM // GRID_W * QCOLS
    nk = TM // GRID_W * KCOLS
    return pl.pallas_call(
        _qkv_grid_body,
        grid=(bsz, SEQ // TM),
        in_specs=_qkv_in_specs(TM),
        out_specs=[pl.BlockSpec((None, NGRP, nq, D), lambda b, t: (b, 0, t, 0)),
                   pl.BlockSpec((None, NGRP, nk, D), lambda b, t: (b, 0, t, 0)),
                   pl.BlockSpec((None, NGRP, nk, D), lambda b, t: (b, 0, t, 0))],
        out_shape=[jax.ShapeDtypeStruct((bsz, NGRP, GRID_ROWS * QCOLS, D), BF16),
                   jax.ShapeDtypeStruct((bsz, NGRP, GRID_ROWS * KCOLS, D), BF16),
                   jax.ShapeDtypeStruct((bsz, NGRP, GRID_ROWS * KCOLS, D), BF16)],
        compiler_params=_cparams(2),
        name="qkv_grid",
    )(x, mod, g, wqkv_bf, bd, qg, kg)


def _qkv_ctx(ctx, mod, g, wqkv_bf, q_g, k_g, mod_row):
    bsz, n, _ = ctx.shape
    bd, qg, kg = _qkv_consts(q_g, k_g)
    spec = pl.BlockSpec((None, n, D), lambda b, t: (b, t, 0))
    return pl.pallas_call(
        functools.partial(_qkv_ctx_body, mod_row),
        grid=(bsz, 1),
        in_specs=_qkv_in_specs(n),
        out_specs=[spec, spec],
        out_shape=[jax.ShapeDtypeStruct((bsz, n, D), BF16)] * 2,
        compiler_params=_cparams(2),
        name="qkv_ctx",
    )(ctx, mod, g, wqkv_bf, bd, qg, kg)


def _bias_tables(rpb):
    n_t = 18
    n_ro = 2 * WIN_ROWS - 1
    n_co = 2 * WIN_COLS - 1
    g = np.arange(NGRP)[:, None, None]
    cq = np.arange(QCOLS)[None, :, None]
    kcw = np.arange(KCOLS)[None, None, :]
    c = QCOLS * g + cq
    kc = np.asarray(KC0)[:, None, None] + kcw
    cs = np.clip(c - WIN_COLS // 2, 0, GRID_W - WIN_COLS)
    col_ok = (kc >= cs) & (kc < cs + WIN_COLS)
    co = np.where(col_ok, kc - c + (WIN_COLS - 1), -1)
    onehot = (np.arange(n_co)[:, None, None, None] == co[None]).astype(np.float32)
    sel = jnp.einsum("hrc,cn->hrn", rpb, jnp.asarray(onehot.reshape(n_co, -1)),
                     precision=lax.Precision.HIGHEST)
    sel = sel.reshape(N_HEADS, n_ro, NGRP, QCOLS, KCOLS)
    pad = jnp.pad(sel, ((0, 0), (3, KROWS_PER_TILE - 1), (0, 0), (0, 0), (0, 0)))
    vals = jnp.stack([pad[:, jj:jj + n_t] for jj in range(KROWS_PER_TILE)], axis=4)
    vals = vals.transpose(0, 2, 1, 3, 4, 5)
    vals = jnp.where(col_ok[None, :, None, :, None, :], vals, NEG)
    bt = vals.reshape(N_HEADS // 2, 2, NGRP, n_t, QCOLS, 128).transpose(0, 2, 1, 3, 4, 5)
    delta = np.arange(13)[:, None, None] - 8
    jj2 = (np.arange(128) // KCOLS)[None, None, :]
    rm = np.where((jj2 >= delta) & (jj2 < delta + WIN_ROWS), 0.0, NEG)
    rm = np.broadcast_to(rm, (13, QCOLS, 128)).astype(np.float32)
    return bt.astype(F32), jnp.asarray(rm)


def _attn_body(q_ref, k_ref, v_ref, kc_ref, vc_ref, bt_ref, rm_ref, o_ref):
    kc = kc_ref[...]
    vc = vc_ref[...]
    nq = RB * QCOLS
    nk = WROWS * KCOLS
    lane = lax.broadcasted_iota(jnp.int32, (nq, 128), 1)
    nt = nk // 128
    dn = (((1,), (1,)), ((), ()))

    def step(rb, carry):
        wr = jnp.clip(RB * rb - WIN_ROWS // 2, 0, GRID_ROWS - WROWS)
        koff = pl.multiple_of(wr * KCOLS, 128)
        kw = k_ref[pl.ds(koff, nk), :]
        vw = v_ref[pl.ds(koff, nk), :]
        qoff = pl.multiple_of(rb * nq, nq)
        q = q_ref[pl.ds(qoff, nq), :]
        outs = []
        for e in range(2):
            in_head = (lane >= HEAD_DIM * e) & (lane < HEAD_DIM * (e + 1))
            qm = jnp.where(in_head, q, jnp.zeros_like(q))
            s_loc = lax.dot_general(qm, kw, dn, preferred_element_type=F32)
            s_ctx = lax.dot_general(qm, kc, dn, preferred_element_type=F32)
            p_rows = []
            l_rows = []
            for i in range(RB):
                r = RB * rb + i
                rs = jnp.clip(r - WIN_ROWS // 2, 0, GRID_ROWS - WIN_ROWS)
                a = rs - wr
                rows = slice(i * QCOLS, (i + 1) * QCOLS)
                blks = []
                for jq in range(nt):
                    rmi = jnp.clip(a - KROWS_PER_TILE * jq, -8, 4) + 8
                    bti = jnp.clip(wr + KROWS_PER_TILE * jq - r + (WIN_ROWS - 1), -3, 14) + 3
                    blks.append(s_loc[rows, jq * 128:(jq + 1) * 128] + bt_ref[e, bti] + rm_ref[rmi])
                for jc in range(CTX // 128):
                    blks.append(s_ctx[rows, jc * 128:(jc + 1) * 128])
                m = jnp.max(functools.reduce(jnp.maximum, blks), axis=-1, keepdims=True)
                ps = [jnp.exp(sb - m) for sb in blks]
                l_rows.append(jnp.sum(functools.reduce(jnp.add, ps), axis=-1, keepdims=True))
                p_rows.append(jnp.concatenate(ps, axis=1))
            p = jnp.concatenate(p_rows, axis=0).astype(BF16)
            lsum = jnp.concatenate(l_rows, axis=0)
            o = (jnp.dot(p[:, :nk], vw, preferred_element_type=F32)
                 + jnp.dot(p[:, nk:], vc, preferred_element_type=F32))
            outs.append(o / lsum)
        o = jnp.where(lane < HEAD_DIM, outs[0], outs[1])
        o_ref[pl.ds(qoff, nq), :] = o.astype(BF16)
        return carry

    lax.fori_loop(0, NRB, step, 0, unroll=2)


def _attention(qcb, kcb, vcb, kc, vc, bt, rm):
    bsz = qcb.shape[0]
    nhp = N_HEADS // 2
    n_t = bt.shape[3]
    return pl.pallas_call(
        _attn_body,
        grid=(bsz, nhp, NGRP),
        in_specs=[pl.BlockSpec((None, None, GRID_ROWS * QCOLS, 128), lambda b, h, g: (b, g, 0, h)),
                  pl.BlockSpec((None, None, GRID_ROWS * KCOLS, 128), lambda b, h, g: (b, g, 0, h)),
                  pl.BlockSpec((None, None, GRID_ROWS * KCOLS, 128), lambda b, h, g: (b, g, 0, h)),
                  pl.BlockSpec((None, CTX, 128), lambda b, h, g: (b, 0, h)),
                  pl.BlockSpec((None, CTX, 128), lambda b, h, g: (b, 0, h)),
                  pl.BlockSpec((None, None, 2, n_t, QCOLS, 128), lambda b, h, g: (h, g, 0, 0, 0, 0)),
                  _const_spec((13, QCOLS, 128))],
        out_specs=pl.BlockSpec((None, None, GRID_ROWS * QCOLS, 128), lambda b, h, g: (b, g, 0, h)),
        out_shape=jax.ShapeDtypeStruct((bsz, NGRP, GRID_ROWS * QCOLS, D), BF16),
        compiler_params=_cparams(3),
        name="nbr_attention",
    )(qcb, kcb, vcb, kc, vc, bt, rm)


def _attnout_mlp_body(x_ref, o_ref_in, mod_ref, g2_ref, wo_ref, w1_ref, w2_ref, out_ref):
    b = pl.program_id(0)
    gate1, shift2, scale2, gate2 = _mod_rows(mod_ref, b, (2, 3, 4, 5))
    chunks = [o_ref_in[g, rho * QCOLS:(rho + 1) * QCOLS, :]
              for rho in range(TM // GRID_W) for g in range(NGRP)]
    o_nat = jnp.concatenate(chunks, axis=0)
    mo = jnp.dot(o_nat, wo_ref[...], preferred_element_type=F32)
    x1 = x_ref[...] + gate1 * mo
    out_ref[...] = x1 + gate2 * _mlp(x1, g2_ref[...], shift2, scale2, w1_ref, w2_ref)


def _attnout_mlp(x, ocb, mod, g2, wo_bf, w1_bf, w2_bf):
    bsz = x.shape[0]
    nq = TM // GRID_W * QCOLS
    return pl.pallas_call(
        _attnout_mlp_body,
        grid=(bsz, SEQ // TM),
        in_specs=[pl.BlockSpec((None, TM, D), lambda b, t: (b, t, 0)),
                  pl.BlockSpec((None, NGRP, nq, D), lambda b, t: (b, 0, t, 0)),
                  _const_spec((8, N_MOD * D)), _const_spec((1, D)),
                  _const_spec((D, D)), _const_spec((D, D_FF)), _const_spec((D_FF, D))],
        out_specs=pl.BlockSpec((None, TM, D), lambda b, t: (b, t, 0)),
        out_shape=jax.ShapeDtypeStruct((bsz, SEQ, D), F32),
        compiler_params=_cparams(2),
        name="attn_out_mlp",
    )(x, ocb, mod, g2, wo_bf, w1_bf, w2_bf)


def kernel(x, c, ctx, c_ctx, ada_w, ada_b, norm_mix_g, norm_mlp_g, mlp_w1, mlp_w2, ab_w_in, ab_conv_w,
           ab_w_out, na_w_qkv, na_q_g, na_k_g, na_rpb, na_w_out):
    bsz = x.shape[0]
    ctx_row = bsz
    cond8 = jnp.zeros((8, D), F32).at[:bsz].set(c).at[ctx_row].set(c_ctx)
    mods = _modulation(cond8, ada_w, ada_b)

    bf = lambda w: w.astype(BF16)
    g_mix = norm_mix_g.reshape(DEPTH, 1, D)
    g_mlp = norm_mlp_g.reshape(DEPTH, 1, D)
    conv_w8 = jnp.zeros((8, CONV_CH), F32).at[:3].set(ab_conv_w[0])

    w_in, w_out0 = bf(ab_w_in[0]), bf(ab_w_out[0])
    w1_0, w2_0 = bf(mlp_w1[0]), bf(mlp_w2[0])
    ay, f = _inproj(x, mods[0], g_mix[0], w_in, conv_w8, None, TM)
    pq = _dft_positions(f)
    mix_x = _bf16_const(_channel_mix_const(SEQ))
    x = _mixout_mlp(x, ay, pq, mods[0], g_mlp[0], mix_x, w_out0, w1_0, w2_0, None, TM)

    ay_c, f_c = _inproj(ctx, mods[0], g_mix[0], w_in, conv_w8, ctx_row, CTX)
    pq_c = _dft_positions_small(f_c)
    mix_c = _bf16_const(_channel_mix_const(CTX))
    ctx = _mixout_mlp(ctx, ay_c, pq_c, mods[0], g_mlp[0], mix_c, w_out0, w1_0, w2_0, ctx_row, CTX)

    wqkv, wo = bf(na_w_qkv[0]), bf(na_w_out[0])
    qcb, kcb, vcb = _qkv_grid(x, mods[1], g_mix[1], wqkv, na_q_g[0], na_k_g[0])
    kc, vc = _qkv_ctx(ctx, mods[1], g_mix[1], wqkv, na_q_g[0], na_k_g[0], ctx_row)
    bt, rm = _bias_tables(na_rpb[0])
    ocb = _attention(qcb, kcb, vcb, kc, vc, bt, rm)
    x = _attnout_mlp(x, ocb, mods[1], g_mlp[1], wo, bf(mlp_w1[1]), bf(mlp_w2[1]))
    return x
```

```python
import functools

import numpy as np
import jax
import jax.numpy as jnp
from jax import lax
from jax.experimental import pallas as pl
from jax.experimental.pallas import tpu as pltpu

F32 = jnp.float32
BF16 = jnp.bfloat16

D = 1024
DEPTH = 2
SEQ = 8192
CTX = 256
GRID_W = 64
GRID_ROWS = SEQ // GRID_W
HEAD_DIM = 64
N_HEADS = 16
CONV_CH = 512
FOUR_CH = 512
FOUR_GROUP = 64
WIN_ROWS = 8
WIN_COLS = 16
D_FF = 4 * D
N_MOD = 6
EPS = 1e-6

TM = 512
HALO = 8
FF_CHUNK = 1024
VMEM_LIMIT = 56 * 1024 * 1024

DFT_A = 32
DFT_B = SEQ // DFT_A
SUB = 8

QCOLS = 16
NGRP = GRID_W // QCOLS
KCOLS = 32
KC0 = tuple(int(np.clip(QCOLS * g - 8, 0, GRID_W - KCOLS)) for g in range(NGRP))
RB = 8
NRB = GRID_ROWS // RB
WROWS = 16
KROWS_PER_TILE = 128 // KCOLS
NEG = -1e30


def _cparams(n_axes):
    return pltpu.CompilerParams(dimension_semantics=("arbitrary",) * n_axes,
                                vmem_limit_bytes=VMEM_LIMIT)


def _const_spec(shape):
    nd = len(shape)
    return pl.BlockSpec(shape, lambda *_: (0,) * nd)


def _bf16_const(a):
    return jnp.asarray(a, F32).astype(BF16)


def _rms_mod(x, g, shift, scale):
    ms = jnp.mean(x * x, axis=-1, keepdims=True)
    y = x * lax.rsqrt(ms + EPS)
    return (y * g) * (1.0 + scale) + shift


def _mod_rows(mod_ref, row, ks):
    return [mod_ref[pl.ds(row, 1), k * D:(k + 1) * D] for k in ks]


def _mlp(x1, g, shift, scale, w1_ref, w2_ref):
    h = _rms_mod(x1, g, shift, scale).astype(BF16)
    acc = jnp.zeros(x1.shape, F32)
    for c in range(D_FF // FF_CHUNK):
        a = jnp.dot(h, w1_ref[:, c * FF_CHUNK:(c + 1) * FF_CHUNK], preferred_element_type=F32)
        a = jnp.maximum(a, 0.0)
        a = (a * a).astype(BF16)
        acc = acc + jnp.dot(a, w2_ref[c * FF_CHUNK:(c + 1) * FF_CHUNK, :], preferred_element_type=F32)
    return acc


def _mod_body(cond_ref, w_ref, b_ref, o_ref):
    c = cond_ref[...]
    s = c * jax.nn.sigmoid(c)
    o_ref[...] = jnp.dot(s.astype(BF16), w_ref[...].astype(BF16), preferred_element_type=F32) + b_ref[...]


def _modulation(cond8, ada_w, ada_b):
    nt = N_MOD * D // D
    return pl.pallas_call(
        _mod_body,
        grid=(DEPTH, nt),
        in_specs=[_const_spec((8, D)),
                  pl.BlockSpec((None, D, D), lambda l, n: (l, 0, n)),
                  pl.BlockSpec((None, 1, D), lambda l, n: (l, 0, n))],
        out_specs=pl.BlockSpec((None, 8, D), lambda l, n: (l, 0, n)),
        out_shape=jax.ShapeDtypeStruct((DEPTH, 8, N_MOD * D), F32),
        compiler_params=_cparams(2),
        name="adaln_mod",
    )(cond8, ada_w, ada_b.reshape(DEPTH, 1, N_MOD * D))


def _inproj_body(mod_row, seq_len, tm, xp_ref, x_ref, xn_ref, mod_ref, g_ref, w_ref, cw_ref,
                 ay_ref, f_ref):
    b = pl.program_id(0)
    t = pl.program_id(1)
    row = b if mod_row is None else mod_row
    xx = jnp.concatenate([xp_ref[...], x_ref[...], xn_ref[...]], axis=0)
    shift, scale = _mod_rows(mod_ref, row, (0, 1))
    h = _rms_mod(xx, g_ref[...], shift, scale).astype(BF16)
    u = jnp.dot(h, w_ref[...], preferred_element_type=F32)
    z = u[:, CONV_CH:2 * CONV_CH] * u[:, 0:CONV_CH]
    n = t * tm - HALO + lax.broadcasted_iota(jnp.int32, (tm + 2 * HALO, 1), 0)
    z = jnp.where((n >= 0) & (n < seq_len), z, 0.0)
    cw = cw_ref[...]
    zc = (z[HALO - 1:HALO - 1 + tm] * cw[0:1] + z[HALO:HALO + tm] * cw[1:2]
          + z[HALO + 1:HALO + 1 + tm] * cw[2:3])
    ay_ref[...] = (u[HALO:HALO + tm, 2 * CONV_CH:3 * CONV_CH] * zc).astype(BF16)
    f_ref[...] = u[HALO:HALO + tm, 3 * CONV_CH:]


def _inproj(x, mod, g, w_in_bf, conv_w8, mod_row, tm):
    bsz, seq_len, _ = x.shape
    nt = seq_len // tm
    nb8 = seq_len // HALO
    r8 = tm // HALO
    body = functools.partial(_inproj_body, mod_row, seq_len, tm)
    return pl.pallas_call(
        body,
        grid=(bsz, nt),
        in_specs=[pl.BlockSpec((None, HALO, D), lambda b, t: (b, jnp.maximum(t * r8 - 1, 0), 0)),
                  pl.BlockSpec((None, tm, D), lambda b, t: (b, t, 0)),
                  pl.BlockSpec((None, HALO, D), lambda b, t: (b, jnp.minimum((t + 1) * r8, nb8 - 1), 0)),
                  _const_spec((8, N_MOD * D)),
                  _const_spec((1, D)),
                  _const_spec((D, 3 * CONV_CH + FOUR_CH)),
                  _const_spec((8, CONV_CH))],
        out_specs=[pl.BlockSpec((None, tm, CONV_CH), lambda b, t: (b, t, 0)),
                   pl.BlockSpec((None, tm, FOUR_CH), lambda b, t: (b, t, 0))],
        out_shape=[jax.ShapeDtypeStruct((bsz, seq_len, CONV_CH), BF16),
                   jax.ShapeDtypeStruct((bsz, seq_len, FOUR_CH), F32)],
        compiler_params=_cparams(2),
        name="mixer_in",
    )(x, x, x, mod, g, w_in_bf, conv_w8)


def _dft_consts():
    a = np.arange(DFT_A)
    s = np.arange(SUB)
    ang = 2.0 * np.pi * np.outer(a, a) / DFT_A
    eye = np.eye(SUB)
    re = np.einsum("va,ts->vtas", np.cos(ang), eye).reshape(DFT_A * SUB, DFT_A * SUB)
    im = np.einsum("va,ts->vtas", -np.sin(ang), eye).reshape(DFT_A * SUB, DFT_A * SUB)
    l1 = np.concatenate([re, im], axis=0)
    m = np.arange(DFT_B // SUB)
    bb = (SUB * m[:, None, None] + s[None, None, :])
    tang = 2.0 * np.pi * a[None, :, None] * bb / SEQ
    twr = np.cos(tang).reshape(DFT_B // SUB, DFT_A * SUB, 1)
    twi = (-np.sin(tang)).reshape(DFT_B // SUB, DFT_A * SUB, 1)
    u = np.arange(DFT_B)
    ang2 = 2.0 * np.pi * np.outer(u, u) / DFT_B
    c2, s2 = np.cos(ang2), np.sin(ang2)
    l2 = np.block([[c2, s2], [s2, -c2]])
    return l1, twr, twi, l2


DFT_MPAIR = 2
DFT_VB = 4


def _dft_a_body(x_ref, l1_ref, twr_ref, twi_ref, br_ref, bi_ref):
    rows = DFT_A * SUB
    reps = FOUR_CH // 128
    x = x_ref[...]
    brs, bis = [], []
    for j in range(DFT_MPAIR):
        xj = x[:, j * SUB:(j + 1) * SUB, :].reshape(rows, FOUR_CH).astype(BF16)
        a = jnp.dot(l1_ref[...], xj, preferred_element_type=F32)
        ar, ai = a[:rows], a[rows:]
        twr = jnp.concatenate([twr_ref[j]] * reps, axis=1)
        twi = jnp.concatenate([twi_ref[j]] * reps, axis=1)
        brs.append((ar * twr - ai * twi).reshape(DFT_A, SUB, FOUR_CH))
        bis.append((ar * twi + ai * twr).reshape(DFT_A, SUB, FOUR_CH))
    br_ref[...] = jnp.concatenate(brs, axis=1).astype(BF16)
    bi_ref[...] = jnp.concatenate(bis, axis=1).astype(BF16)


def _dft_b_body(br_ref, bi_ref, l2_ref, pq_ref):
    for j in range(DFT_VB):
        bm = jnp.concatenate([br_ref[j], bi_ref[j]], axis=0)
        pq = jnp.dot(l2_ref[...], bm, preferred_element_type=F32)
        pq_ref[j] = jnp.concatenate([pq[:DFT_B], pq[DFT_B:]], axis=1).astype(BF16)


def _dft_positions(f):
    bsz = f.shape[0]
    l1, twr, twi, l2 = _dft_consts()
    nm = DFT_B // SUB
    nm2 = nm // DFT_MPAIR
    rows = DFT_A * SUB
    twr_b = jnp.broadcast_to(jnp.asarray(twr, F32), (nm, rows, 128))
    twi_b = jnp.broadcast_to(jnp.asarray(twi, F32), (nm, rows, 128))
    f5 = f.reshape(bsz, DFT_A, nm2, DFT_MPAIR * SUB, FOUR_CH)
    blk5 = pl.BlockSpec((None, DFT_A, None, DFT_MPAIR * SUB, FOUR_CH), lambda m, b: (b, 0, m, 0, 0))
    tw_spec = pl.BlockSpec((DFT_MPAIR, rows, 128), lambda m, b: (m, 0, 0))
    br, bi = pl.pallas_call(
        _dft_a_body,
        grid=(nm2, bsz),
        in_specs=[blk5, _const_spec((2 * rows, rows)), tw_spec, tw_spec],
        out_specs=[blk5, blk5],
        out_shape=[jax.ShapeDtypeStruct((bsz, DFT_A, nm2, DFT_MPAIR * SUB, FOUR_CH), BF16)] * 2,
        compiler_params=_cparams(2),
        name="dft_stage_a",
    )(f5, _bf16_const(l1), twr_b, twi_b)
    br = br.reshape(bsz, DFT_A, DFT_B, FOUR_CH)
    bi = bi.reshape(bsz, DFT_A, DFT_B, FOUR_CH)
    blk4 = pl.BlockSpec((None, DFT_VB, DFT_B, FOUR_CH), lambda b, v: (b, v, 0, 0))
    pq = pl.pallas_call(
        _dft_b_body,
        grid=(bsz, DFT_A // DFT_VB),
        in_specs=[blk4, blk4, _const_spec((2 * DFT_B, 2 * DFT_B))],
        out_specs=pl.BlockSpec((None, DFT_VB, DFT_B, 2 * FOUR_CH), lambda b, v: (b, v, 0, 0)),
        out_shape=jax.ShapeDtypeStruct((bsz, DFT_A, DFT_B, 2 * FOUR_CH), BF16),
        compiler_params=_cparams(2),
        name="dft_stage_b",
    )(br, bi, _bf16_const(l2))
    return jnp.transpose(pq, (0, 2, 1, 3)).reshape(bsz, SEQ, 2 * FOUR_CH)


def _dft_small_body(f_ref, lc_ref, pq_ref):
    n = f_ref.shape[0]
    pq = jnp.dot(lc_ref[...], f_ref[...].astype(BF16), preferred_element_type=F32)
    pq_ref[...] = jnp.concatenate([pq[:n], pq[n:]], axis=1).astype(BF16)


def _dft_positions_small(f):
    bsz, n, _ = f.shape
    k = np.arange(n)
    ang = 2.0 * np.pi * np.outer(k, k) / n
    lc = np.concatenate([np.cos(ang), np.sin(ang)], axis=0)
    return pl.pallas_call(
        _dft_small_body,
        grid=(bsz,),
        in_specs=[pl.BlockSpec((None, n, FOUR_CH), lambda b: (b, 0, 0)), _const_spec((2 * n, n))],
        out_specs=pl.BlockSpec((None, n, 2 * FOUR_CH), lambda b: (b, 0, 0)),
        out_shape=jax.ShapeDtypeStruct((bsz, n, 2 * FOUR_CH), BF16),
        compiler_params=_cparams(1),
        name="dft_small",
    )(f, _bf16_const(lc))


def _channel_mix_const(seq_len):
    k = np.arange(FOUR_GROUP)
    ang = 2.0 * np.pi * np.outer(k, k) / FOUR_GROUP
    ng = FOUR_CH // FOUR_GROUP
    bdc = np.kron(np.eye(ng), np.cos(ang))
    bds = np.kron(np.eye(ng), np.sin(ang))
    return np.concatenate([bdc, -bds], axis=0) / np.sqrt(seq_len * FOUR_GROUP)


def _mixout_mlp_body(mod_row, x_ref, ay_ref, pq_ref, mod_ref, g2_ref, mix_ref, wout_ref,
                     w1_ref, w2_ref, o_ref):
    b = pl.program_id(0)
    row = b if mod_row is None else mod_row
    gate1, shift2, scale2, gate2 = _mod_rows(mod_ref, row, (2, 3, 4, 5))
    fy = jnp.dot(pq_ref[...], mix_ref[...], preferred_element_type=F32)
    cat = jnp.concatenate([ay_ref[...], fy.astype(BF16)], axis=1)
    mo = jnp.dot(cat, wout_ref[...], preferred_element_type=F32)
    x1 = x_ref[...] + gate1 * mo
    o_ref[...] = x1 + gate2 * _mlp(x1, g2_ref[...], shift2, scale2, w1_ref, w2_ref)


def _mixout_mlp(x, ay, pq, mod, g2, mix_bf, wout_bf, w1_bf, w2_bf, mod_row, tm):
    bsz, seq_len, _ = x.shape
    row_spec = lambda w: pl.BlockSpec((None, tm, w), lambda b, t: (b, t, 0))
    body = functools.partial(_mixout_mlp_body, mod_row)
    return pl.pallas_call(
        body,
        grid=(bsz, seq_len // tm),
        in_specs=[row_spec(D), row_spec(CONV_CH), row_spec(2 * FOUR_CH),
                  _const_spec((8, N_MOD * D)), _const_spec((1, D)),
                  _const_spec((2 * FOUR_CH, FOUR_CH)), _const_spec((D, D)),
                  _const_spec((D, D_FF)), _const_spec((D_FF, D))],
        out_specs=row_spec(D),
        out_shape=jax.ShapeDtypeStruct((bsz, seq_len, D), F32),
        compiler_params=_cparams(2),
        name="mixer_out_mlp",
    )(x, ay, pq, mod, g2, mix_bf, wout_bf, w1_bf, w2_bf)


def _head_rms(t, bd_ref, gt):
    tt = (t * t).astype(BF16)
    w = bd_ref.shape[0]
    ms = jnp.concatenate(
        [jnp.dot(tt[:, j * w:(j + 1) * w], bd_ref[...], preferred_element_type=F32)
         for j in range(D // w)], axis=1)
    return t * lax.rsqrt(ms + EPS) * gt


def _qkv_common(mod_row, x_ref, mod_ref, g_ref, w_ref, bd_ref, qg_ref, kg_ref):
    b = pl.program_id(0)
    row = b if mod_row is None else mod_row
    shift, scale = _mod_rows(mod_ref, row, (0, 1))
    h = _rms_mod(x_ref[...], g_ref[...], shift, scale).astype(BF16)
    qkv = jnp.dot(h, w_ref[...], preferred_element_type=F32)
    q = _head_rms(qkv[:, 0:D], bd_ref, qg_ref[...]) * (HEAD_DIM ** -0.5)
    k = _head_rms(qkv[:, D:2 * D], bd_ref, kg_ref[...])
    v = qkv[:, 2 * D:]
    return q, k, v


def _qkv_grid_body(x_ref, mod_ref, g_ref, w_ref, bd_ref, qg_ref, kg_ref, q_ref, k_ref, v_ref):
    q, k, v = _qkv_common(None, x_ref, mod_ref, g_ref, w_ref, bd_ref, qg_ref, kg_ref)
    for rho in range(TM // GRID_W):
        for g in range(NGRP):
            q0 = rho * GRID_W + QCOLS * g
            q_ref[g, rho * QCOLS:(rho + 1) * QCOLS, :] = q[q0:q0 + QCOLS].astype(BF16)
            k0 = rho * GRID_W + KC0[g]
            k_ref[g, rho * KCOLS:(rho + 1) * KCOLS, :] = k[k0:k0 + KCOLS].astype(BF16)
            v_ref[g, rho * KCOLS:(rho + 1) * KCOLS, :] = v[k0:k0 + KCOLS].astype(BF16)


def _qkv_ctx_body(mod_row, x_ref, mod_ref, g_ref, w_ref, bd_ref, qg_ref, kg_ref, k_ref, v_ref):
    _, k, v = _qkv_common(mod_row, x_ref, mod_ref, g_ref, w_ref, bd_ref, qg_ref, kg_ref)
    k_ref[...] = k.astype(BF16)
    v_ref[...] = v.astype(BF16)


def _qkv_consts(q_g, k_g):
    w = 256
    bd = np.kron(np.eye(w // HEAD_DIM), np.ones((HEAD_DIM, HEAD_DIM))) / HEAD_DIM
    qg = jnp.tile(q_g, N_HEADS).reshape(1, D)
    kg = jnp.tile(k_g, N_HEADS).reshape(1, D)
    return _bf16_const(bd), qg, kg


def _qkv_in_specs(tm):
    return [pl.BlockSpec((None, tm, D), lambda b, t: (b, t, 0)),
            _const_spec((8, N_MOD * D)), _const_spec((1, D)), _const_spec((D, 3 * D)),
            _const_spec((256, 256)), _const_spec((1, D)), _const_spec((1, D))]


def _qkv_grid(x, mod, g, wqkv_bf, q_g, k_g):
    bsz = x.shape[0]
    bd, qg, kg = _qkv_consts(q_g, k_g)
    nq = TM // GRID_W * QCOLS
    nk = TM // GRID_W * KCOLS
    return pl.pallas_call(
        _qkv_grid_body,
        grid=(bsz, SEQ // TM),
        in_specs=_qkv_in_specs(TM),
        out_specs=[pl.BlockSpec((None, NGRP, nq, D), lambda b, t: (b, 0, t, 0)),
                   pl.BlockSpec((None, NGRP, nk, D), lambda b, t: (b, 0, t, 0)),
                   pl.BlockSpec((None, NGRP, nk, D), lambda b, t: (b, 0, t, 0))],
        out_shape=[jax.ShapeDtypeStruct((bsz, NGRP, GRID_ROWS * QCOLS, D), BF16),
                   jax.ShapeDtypeStruct((bsz, NGRP, GRID_ROWS * KCOLS, D), BF16),
                   jax.ShapeDtypeStruct((bsz, NGRP, GRID_ROWS * KCOLS, D), BF16)],
        compiler_params=_cparams(2),
        name="qkv_grid",
    )(x, mod, g, wqkv_bf, bd, qg, kg)


def _qkv_ctx(ctx, mod, g, wqkv_bf, q_g, k_g, mod_row):
    bsz, n, _ = ctx.shape
    bd, qg, kg = _qkv_consts(q_g, k_g)
    spec = pl.BlockSpec((None, n, D), lambda b, t: (b, t, 0))
    return pl.pallas_call(
        functools.partial(_qkv_ctx_body, mod_row),
        grid=(bsz, 1),
        in_specs=_qkv_in_specs(n),
        out_specs=[spec, spec],
        out_shape=[jax.ShapeDtypeStruct((bsz, n, D), BF16)] * 2,
        compiler_params=_cparams(2),
        name="qkv_ctx",
    )(ctx, mod, g, wqkv_bf, bd, qg, kg)


def _bias_tables(rpb):
    n_t = 18
    n_ro = 2 * WIN_ROWS - 1
    n_co = 2 * WIN_COLS - 1
    g = np.arange(NGRP)[:, None, None]
    cq = np.arange(QCOLS)[None, :, None]
    kcw = np.arange(KCOLS)[None, None, :]
    c = QCOLS * g + cq
    kc = np.asarray(KC0)[:, None, None] + kcw
    cs = np.clip(c - WIN_COLS // 2, 0, GRID_W - WIN_COLS)
    col_ok = (kc >= cs) & (kc < cs + WIN_COLS)
    co = np.where(col_ok, kc - c + (WIN_COLS - 1), -1)
    onehot = (np.arange(n_co)[:, None, None, None] == co[None]).astype(np.float32)
    sel = jnp.einsum("hrc,cn->hrn", rpb, jnp.asarray(onehot.reshape(n_co, -1)),
                     precision=lax.Precision.HIGHEST)
    sel = sel.reshape(N_HEADS, n_ro, NGRP, QCOLS, KCOLS).transpose(0, 2, 1, 3, 4)
    pad = jnp.pad(sel, ((0, 0), (0, 0), (3, KROWS_PER_TILE - 1), (0, 0), (0, 0)))
    vals = jnp.stack([pad[:, :, jj:jj + n_t] for jj in range(KROWS_PER_TILE)], axis=4)
    vals = jnp.where(col_ok[None, :, None, :, None, :], vals, NEG)
    bt = vals.reshape(N_HEADS // 2, 2, NGRP, n_t, QCOLS, 128)
    delta = np.arange(13)[:, None, None] - 8
    jj2 = (np.arange(128) // KCOLS)[None, None, :]
    rm = np.where((jj2 >= delta) & (jj2 < delta + WIN_ROWS), 0.0, NEG)
    rm = np.broadcast_to(rm, (13, QCOLS, 128)).astype(np.float32)
    return bt.astype(F32), jnp.asarray(rm)


def _attn_body(q_ref, k_ref, v_ref, kc_ref, vc_ref, bt_ref, rm_ref, o_ref, s_scr):
    kc = kc_ref[...]
    vc = vc_ref[...]
    nq = RB * QCOLS
    nk = WROWS * KCOLS
    lane = lax.broadcasted_iota(jnp.int32, (nq, 128), 1)
    nt = nk // 128
    dn = (((1,), (1,)), ((), ()))

    def window_row(rb):
        return jnp.clip(RB * rb - WIN_ROWS // 2, 0, GRID_ROWS - WROWS)

    def scores(rb, slot):
        koff = pl.multiple_of(window_row(rb) * KCOLS, 128)
        kw = k_ref[pl.ds(koff, nk), :]
        q = q_ref[pl.ds(pl.multiple_of(rb * nq, nq), nq), :]
        for e in range(2):
            in_head = (lane >= HEAD_DIM * e) & (lane < HEAD_DIM * (e + 1))
            qm = jnp.where(in_head, q, jnp.zeros_like(q))
            s_scr[slot, e, :, 0:nk] = lax.dot_general(qm, kw, dn, preferred_element_type=F32)
            s_scr[slot, e, :, nk:] = lax.dot_general(qm, kc, dn, preferred_element_type=F32)

    def finish(rb, slot):
        wr = window_row(rb)
        vw = v_ref[pl.ds(pl.multiple_of(wr * KCOLS, 128), nk), :]
        outs = []
        for e in range(2):
            p_rows = []
            l_rows = []
            for i in range(RB):
                r = RB * rb + i
                a = jnp.clip(r - WIN_ROWS // 2, 0, GRID_ROWS - WIN_ROWS) - wr
                rows = slice(i * QCOLS, (i + 1) * QCOLS)
                blks = []
                for jq in range(nt):
                    rmi = jnp.clip(a - KROWS_PER_TILE * jq, -8, 4) + 8
                    bti = jnp.clip(wr + KROWS_PER_TILE * jq - r + (WIN_ROWS - 1), -3, 14) + 3
                    blks.append(s_scr[slot, e, rows, jq * 128:(jq + 1) * 128] + bt_ref[e, bti] + rm_ref[rmi])
                for jc in range(CTX // 128):
                    blks.append(s_scr[slot, e, rows, nk + jc * 128:nk + (jc + 1) * 128])
                m = jnp.max(functools.reduce(jnp.maximum, blks), axis=-1, keepdims=True)
                ps = [jnp.exp(sb - m) for sb in blks]
                l_rows.append(jnp.sum(functools.reduce(jnp.add, ps), axis=-1, keepdims=True))
                p_rows.append(jnp.concatenate(ps, axis=1))
            p = jnp.concatenate(p_rows, axis=0).astype(BF16)
            lsum = jnp.concatenate(l_rows, axis=0)
            o = (jnp.dot(p[:, :nk], vw, preferred_element_type=F32)
                 + jnp.dot(p[:, nk:], vc, preferred_element_type=F32))
            outs.append(o / lsum)
        o = jnp.where(lane < HEAD_DIM, outs[0], outs[1])
        o_ref[pl.ds(pl.multiple_of(rb * nq, nq), nq), :] = o.astype(BF16)

    scores(0, 0)

    def pair(tt, carry):
        t0 = 2 * tt
        scores(t0 + 1, 1)
        finish(t0, 0)
        scores(jnp.minimum(t0 + 2, NRB - 1), 0)
        finish(t0 + 1, 1)
        return carry

    lax.fori_loop(0, NRB // 2, pair, 0)


def _attention(qcb, kcb, vcb, kc, vc, bt, rm):
    bsz = qcb.shape[0]
    nhp = N_HEADS // 2
    n_t = bt.shape[3]
    return pl.pallas_call(
        _attn_body,
        grid=(bsz, nhp, NGRP),
        in_specs=[pl.BlockSpec((None, None, GRID_ROWS * QCOLS, 128), lambda b, h, g: (b, g, 0, h)),
                  pl.BlockSpec((None, None, GRID_ROWS * KCOLS, 128), lambda b, h, g: (b, g, 0, h)),
                  pl.BlockSpec((None, None, GRID_ROWS * KCOLS, 128), lambda b, h, g: (b, g, 0, h)),
                  pl.BlockSpec((None, CTX, 128), lambda b, h, g: (b, 0, h)),
                  pl.BlockSpec((None, CTX, 128), lambda b, h, g: (b, 0, h)),
                  pl.BlockSpec((None, 2, None, n_t, QCOLS, 128), lambda b, h, g: (h, 0, g, 0, 0, 0)),
                  _const_spec((13, QCOLS, 128))],
        out_specs=pl.BlockSpec((None, None, GRID_ROWS * QCOLS, 128), lambda b, h, g: (b, g, 0, h)),
        out_shape=jax.ShapeDtypeStruct((bsz, NGRP, GRID_ROWS * QCOLS, D), BF16),
        scratch_shapes=[pltpu.VMEM((2, 2, RB * QCOLS, WROWS * KCOLS + CTX), F32)],
        compiler_params=_cparams(3),
        name="nbr_attention",
    )(qcb, kcb, vcb, kc, vc, bt, rm)


def _attnout_mlp_body(x_ref, o_ref_in, mod_ref, g2_ref, wo_ref, w1_ref, w2_ref, out_ref):
    b = pl.program_id(0)
    gate1, shift2, scale2, gate2 = _mod_rows(mod_ref, b, (2, 3, 4, 5))
    chunks = [o_ref_in[g, rho * QCOLS:(rho + 1) * QCOLS, :]
              for rho in range(TM // GRID_W) for g in range(NGRP)]
    o_nat = jnp.concatenate(chunks, axis=0)
    mo = jnp.dot(o_nat, wo_ref[...], preferred_element_type=F32)
    x1 = x_ref[...] + gate1 * mo
    out_ref[...] = x1 + gate2 * _mlp(x1, g2_ref[...], shift2, scale2, w1_ref, w2_ref)


def _attnout_mlp(x, ocb, mod, g2, wo_bf, w1_bf, w2_bf):
    bsz = x.shape[0]
    nq = TM // GRID_W * QCOLS
    return pl.pallas_call(
        _attnout_mlp_body,
        grid=(bsz, SEQ // TM),
        in_specs=[pl.BlockSpec((None, TM, D), lambda b, t: (b, t, 0)),
                  pl.BlockSpec((None, NGRP, nq, D), lambda b, t: (b, 0, t, 0)),
                  _const_spec((8, N_MOD * D)), _const_spec((1, D)),
                  _const_spec((D, D)), _const_spec((D, D_FF)), _const_spec((D_FF, D))],
        out_specs=pl.BlockSpec((None, TM, D), lambda b, t: (b, t, 0)),
        out_shape=jax.ShapeDtypeStruct((bsz, SEQ, D), F32),
        compiler_params=_cparams(2),
        name="attn_out_mlp",
    )(x, ocb, mod, g2, wo_bf, w1_bf, w2_bf)


def kernel(x, c, ctx, c_ctx, ada_w, ada_b, norm_mix_g, norm_mlp_g, mlp_w1, mlp_w2, ab_w_in, ab_conv_w,
           ab_w_out, na_w_qkv, na_q_g, na_k_g, na_rpb, na_w_out):
    bsz = x.shape[0]
    ctx_row = bsz
    cond8 = jnp.zeros((8, D), F32).at[:bsz].set(c).at[ctx_row].set(c_ctx)
    mods = _modulation(cond8, ada_w, ada_b)

    bf = lambda w: w.astype(BF16)
    g_mix = norm_mix_g.reshape(DEPTH, 1, D)
    g_mlp = norm_mlp_g.reshape(DEPTH, 1, D)
    conv_w8 = jnp.zeros((8, CONV_CH), F32).at[:3].set(ab_conv_w[0])

    w_in, w_out0 = bf(ab_w_in[0]), bf(ab_w_out[0])
    w1_0, w2_0 = bf(mlp_w1[0]), bf(mlp_w2[0])
    ay, f = _inproj(x, mods[0], g_mix[0], w_in, conv_w8, None, TM)
    pq = _dft_positions(f)
    mix_x = _bf16_const(_channel_mix_const(SEQ))
    x = _mixout_mlp(x, ay, pq, mods[0], g_mlp[0], mix_x, w_out0, w1_0, w2_0, None, TM)

    ay_c, f_c = _inproj(ctx, mods[0], g_mix[0], w_in, conv_w8, ctx_row, CTX)
    pq_c = _dft_positions_small(f_c)
    mix_c = _bf16_const(_channel_mix_const(CTX))
    ctx = _mixout_mlp(ctx, ay_c, pq_c, mods[0], g_mlp[0], mix_c, w_out0, w1_0, w2_0, ctx_row, CTX)

    wqkv, wo = bf(na_w_qkv[0]), bf(na_w_out[0])
    qcb, kcb, vcb = _qkv_grid(x, mods[1], g_mix[1], wqkv, na_q_g[0], na_k_g[0])
    kc, vc = _qkv_ctx(ctx, mods[1], g_mix[1], wqkv, na_q_g[0], na_k_g[0], ctx_row)
    bt, rm = _bias_tables(na_rpb[0])
    ocb = _attention(qcb, kcb, vcb, kc, vc, bt, rm)
    x = _attnout_mlp(x, ocb, mods[1], g_mlp[1], wo, bf(mlp_w1[1]), bf(mlp_w2[1]))
    return x
```

```python
import functools

import numpy as np
import jax
import jax.numpy as jnp
from jax import lax
from jax.experimental import pallas as pl
from jax.experimental.pallas import tpu as pltpu

F32 = jnp.float32
BF16 = jnp.bfloat16

D = 1024
DEPTH = 2
SEQ = 8192
CTX = 256
GRID_W = 64
GRID_ROWS = SEQ // GRID_W
HEAD_DIM = 64
N_HEADS = 16
CONV_CH = 512
FOUR_CH = 512
FOUR_GROUP = 64
WIN_ROWS = 8
WIN_COLS = 16
D_FF = 4 * D
N_MOD = 6
EPS = 1e-6

TM = 512
HALO = 8
FF_CHUNK = 1024
VMEM_LIMIT = 56 * 1024 * 1024

DFT_A = 32
DFT_B = SEQ // DFT_A
SUB = 8

QCOLS = 16
NGRP = GRID_W // QCOLS
KCOLS = 32
KC0 = tuple(int(np.clip(QCOLS * g - 8, 0, GRID_W - KCOLS)) for g in range(NGRP))
RB = 8
NRB = GRID_ROWS // RB
WROWS = 16
KROWS_PER_TILE = 128 // KCOLS
NEG = -1e30
LOG2E = float(np.log2(np.e))
N_BT = 2 * WIN_ROWS - 1 + KROWS_PER_TILE - 1


def _cparams(n_axes):
    return pltpu.CompilerParams(dimension_semantics=("arbitrary",) * n_axes,
                                vmem_limit_bytes=VMEM_LIMIT)


def _const_spec(shape):
    nd = len(shape)
    return pl.BlockSpec(shape, lambda *_: (0,) * nd)


def _bf16_const(a):
    return jnp.asarray(a, F32).astype(BF16)


def _rms_mod(x, g, shift, scale):
    ms = jnp.mean(x * x, axis=-1, keepdims=True)
    y = x * lax.rsqrt(ms + EPS)
    return (y * g) * (1.0 + scale) + shift


def _mod_rows(mod_ref, row, ks):
    return [mod_ref[pl.ds(row, 1), k * D:(k + 1) * D] for k in ks]


def _mlp(x1, g, shift, scale, w1_ref, w2_ref):
    h = _rms_mod(x1, g, shift, scale).astype(BF16)
    acc = jnp.zeros(x1.shape, F32)
    for c in range(D_FF // FF_CHUNK):
        a = jnp.dot(h, w1_ref[:, c * FF_CHUNK:(c + 1) * FF_CHUNK], preferred_element_type=F32)
        a = jnp.maximum(a, 0.0)
        a = (a * a).astype(BF16)
        acc = acc + jnp.dot(a, w2_ref[c * FF_CHUNK:(c + 1) * FF_CHUNK, :], preferred_element_type=F32)
    return acc


def _mod_body(cond_ref, w_ref, b_ref, o_ref):
    c = cond_ref[...]
    s = c * jax.nn.sigmoid(c)
    o_ref[...] = jnp.dot(s.astype(BF16), w_ref[...].astype(BF16), preferred_element_type=F32) + b_ref[...]


def _modulation(cond8, ada_w, ada_b):
    nt = N_MOD * D // D
    return pl.pallas_call(
        _mod_body,
        grid=(DEPTH, nt),
        in_specs=[_const_spec((8, D)),
                  pl.BlockSpec((None, D, D), lambda l, n: (l, 0, n)),
                  pl.BlockSpec((None, 1, D), lambda l, n: (l, 0, n))],
        out_specs=pl.BlockSpec((None, 8, D), lambda l, n: (l, 0, n)),
        out_shape=jax.ShapeDtypeStruct((DEPTH, 8, N_MOD * D), F32),
        compiler_params=_cparams(2),
        name="adaln_mod",
    )(cond8, ada_w, ada_b.reshape(DEPTH, 1, N_MOD * D))


def _inproj_body(mod_row, seq_len, tm, xp_ref, x_ref, xn_ref, mod_ref, g_ref, w_ref, cw_ref,
                 ay_ref, f_ref):
    b = pl.program_id(0)
    t = pl.program_id(1)
    row = b if mod_row is None else mod_row
    xx = jnp.concatenate([xp_ref[...], x_ref[...], xn_ref[...]], axis=0)
    shift, scale = _mod_rows(mod_ref, row, (0, 1))
    h = _rms_mod(xx, g_ref[...], shift, scale).astype(BF16)
    u = jnp.dot(h, w_ref[...], preferred_element_type=F32)
    z = u[:, CONV_CH:2 * CONV_CH] * u[:, 0:CONV_CH]
    n = t * tm - HALO + lax.broadcasted_iota(jnp.int32, (tm + 2 * HALO, 1), 0)
    z = jnp.where((n >= 0) & (n < seq_len), z, 0.0)
    cw = cw_ref[...]
    zc = (z[HALO - 1:HALO - 1 + tm] * cw[0:1] + z[HALO:HALO + tm] * cw[1:2]
          + z[HALO + 1:HALO + 1 + tm] * cw[2:3])
    ay_ref[...] = (u[HALO:HALO + tm, 2 * CONV_CH:3 * CONV_CH] * zc).astype(BF16)
    f_ref[...] = u[HALO:HALO + tm, 3 * CONV_CH:]


def _inproj(x, mod, g, w_in_bf, conv_w8, mod_row, tm):
    bsz, seq_len, _ = x.shape
    nt = seq_len // tm
    nb8 = seq_len // HALO
    r8 = tm // HALO
    body = functools.partial(_inproj_body, mod_row, seq_len, tm)
    return pl.pallas_call(
        body,
        grid=(bsz, nt),
        in_specs=[pl.BlockSpec((None, HALO, D), lambda b, t: (b, jnp.maximum(t * r8 - 1, 0), 0)),
                  pl.BlockSpec((None, tm, D), lambda b, t: (b, t, 0)),
                  pl.BlockSpec((None, HALO, D), lambda b, t: (b, jnp.minimum((t + 1) * r8, nb8 - 1), 0)),
                  _const_spec((8, N_MOD * D)),
                  _const_spec((1, D)),
                  _const_spec((D, 3 * CONV_CH + FOUR_CH)),
                  _const_spec((8, CONV_CH))],
        out_specs=[pl.BlockSpec((None, tm, CONV_CH), lambda b, t: (b, t, 0)),
                   pl.BlockSpec((None, tm, FOUR_CH), lambda b, t: (b, t, 0))],
        out_shape=[jax.ShapeDtypeStruct((bsz, seq_len, CONV_CH), BF16),
                   jax.ShapeDtypeStruct((bsz, seq_len, FOUR_CH), F32)],
        compiler_params=_cparams(2),
        name="mixer_in",
    )(x, x, x, mod, g, w_in_bf, conv_w8)


def _dft_consts():
    a = np.arange(DFT_A)
    s = np.arange(SUB)
    ang = 2.0 * np.pi * np.outer(a, a) / DFT_A
    eye = np.eye(SUB)
    re = np.einsum("va,ts->vtas", np.cos(ang), eye).reshape(DFT_A * SUB, DFT_A * SUB)
    im = np.einsum("va,ts->vtas", -np.sin(ang), eye).reshape(DFT_A * SUB, DFT_A * SUB)
    l1 = np.concatenate([re, im], axis=0)
    m = np.arange(DFT_B // SUB)
    bb = (SUB * m[:, None, None] + s[None, None, :])
    tang = 2.0 * np.pi * a[None, :, None] * bb / SEQ
    twr = np.cos(tang).reshape(DFT_B // SUB, DFT_A * SUB, 1)
    twi = (-np.sin(tang)).reshape(DFT_B // SUB, DFT_A * SUB, 1)
    u = np.arange(DFT_B)
    ang2 = 2.0 * np.pi * np.outer(u, u) / DFT_B
    c2, s2 = np.cos(ang2), np.sin(ang2)
    l2 = np.block([[c2, s2], [s2, -c2]])
    return l1, twr, twi, l2


DFT_MPAIR = 2
DFT_VB = 4


def _dft_a_body(x_ref, l1_ref, twr_ref, twi_ref, br_ref, bi_ref):
    rows = DFT_A * SUB
    reps = FOUR_CH // 128
    x = x_ref[...]
    brs, bis = [], []
    for j in range(DFT_MPAIR):
        xj = x[:, j * SUB:(j + 1) * SUB, :].reshape(rows, FOUR_CH).astype(BF16)
        a = jnp.dot(l1_ref[...], xj, preferred_element_type=F32)
        ar, ai = a[:rows], a[rows:]
        twr = jnp.concatenate([twr_ref[j]] * reps, axis=1)
        twi = jnp.concatenate([twi_ref[j]] * reps, axis=1)
        brs.append((ar * twr - ai * twi).reshape(DFT_A, SUB, FOUR_CH))
        bis.append((ar * twi + ai * twr).reshape(DFT_A, SUB, FOUR_CH))
    br_ref[...] = jnp.concatenate(brs, axis=1).astype(BF16)
    bi_ref[...] = jnp.concatenate(bis, axis=1).astype(BF16)


def _dft_b_body(br_ref, bi_ref, l2_ref, pq_ref):
    for j in range(DFT_VB):
        bm = jnp.concatenate([br_ref[j], bi_ref[j]], axis=0)
        pq = jnp.dot(l2_ref[...], bm, preferred_element_type=F32)
        pq_ref[j] = jnp.concatenate([pq[:DFT_B], pq[DFT_B:]], axis=1).astype(BF16)


def _dft_positions(f):
    bsz = f.shape[0]
    l1, twr, twi, l2 = _dft_consts()
    nm = DFT_B // SUB
    nm2 = nm // DFT_MPAIR
    rows = DFT_A * SUB
    twr_b = jnp.broadcast_to(jnp.asarray(twr, F32), (nm, rows, 128))
    twi_b = jnp.broadcast_to(jnp.asarray(twi, F32), (nm, rows, 128))
    f5 = f.reshape(bsz, DFT_A, nm2, DFT_MPAIR * SUB, FOUR_CH)
    blk5 = pl.BlockSpec((None, DFT_A, None, DFT_MPAIR * SUB, FOUR_CH), lambda m, b: (b, 0, m, 0, 0))
    tw_spec = pl.BlockSpec((DFT_MPAIR, rows, 128), lambda m, b: (m, 0, 0))
    br, bi = pl.pallas_call(
        _dft_a_body,
        grid=(nm2, bsz),
        in_specs=[blk5, _const_spec((2 * rows, rows)), tw_spec, tw_spec],
        out_specs=[blk5, blk5],
        out_shape=[jax.ShapeDtypeStruct((bsz, DFT_A, nm2, DFT_MPAIR * SUB, FOUR_CH), BF16)] * 2,
        compiler_params=_cparams(2),
        name="dft_stage_a",
    )(f5, _bf16_const(l1), twr_b, twi_b)
    br = br.reshape(bsz, DFT_A, DFT_B, FOUR_CH)
    bi = bi.reshape(bsz, DFT_A, DFT_B, FOUR_CH)
    blk4 = pl.BlockSpec((None, DFT_VB, DFT_B, FOUR_CH), lambda b, v: (b, v, 0, 0))
    pq = pl.pallas_call(
        _dft_b_body,
        grid=(bsz, DFT_A // DFT_VB),
        in_specs=[blk4, blk4, _const_spec((2 * DFT_B, 2 * DFT_B))],
        out_specs=pl.BlockSpec((None, DFT_VB, DFT_B, 2 * FOUR_CH), lambda b, v: (b, v, 0, 0)),
        out_shape=jax.ShapeDtypeStruct((bsz, DFT_A, DFT_B, 2 * FOUR_CH), BF16),
        compiler_params=_cparams(2),
        name="dft_stage_b",
    )(br, bi, _bf16_const(l2))
    return jnp.transpose(pq, (0, 2, 1, 3)).reshape(bsz, SEQ, 2 * FOUR_CH)


def _dft_small_body(f_ref, lc_ref, pq_ref):
    n = f_ref.shape[0]
    pq = jnp.dot(lc_ref[...], f_ref[...].astype(BF16), preferred_element_type=F32)
    pq_ref[...] = jnp.concatenate([pq[:n], pq[n:]], axis=1).astype(BF16)


def _dft_positions_small(f):
    bsz, n, _ = f.shape
    k = np.arange(n)
    ang = 2.0 * np.pi * np.outer(k, k) / n
    lc = np.concatenate([np.cos(ang), np.sin(ang)], axis=0)
    return pl.pallas_call(
        _dft_small_body,
        grid=(bsz,),
        in_specs=[pl.BlockSpec((None, n, FOUR_CH), lambda b: (b, 0, 0)), _const_spec((2 * n, n))],
        out_specs=pl.BlockSpec((None, n, 2 * FOUR_CH), lambda b: (b, 0, 0)),
        out_shape=jax.ShapeDtypeStruct((bsz, n, 2 * FOUR_CH), BF16),
        compiler_params=_cparams(1),
        name="dft_small",
    )(f, _bf16_const(lc))


def _channel_mix_const(seq_len):
    k = np.arange(FOUR_GROUP)
    ang = 2.0 * np.pi * np.outer(k, k) / FOUR_GROUP
    ng = FOUR_CH // FOUR_GROUP
    bdc = np.kron(np.eye(ng), np.cos(ang))
    bds = np.kron(np.eye(ng), np.sin(ang))
    return np.concatenate([bdc, -bds], axis=0) / np.sqrt(seq_len * FOUR_GROUP)


def _mixout_mlp_body(mod_row, x_ref, ay_ref, pq_ref, mod_ref, g2_ref, mix_ref, wout_ref,
                     w1_ref, w2_ref, o_ref):
    b = pl.program_id(0)
    row = b if mod_row is None else mod_row
    gate1, shift2, scale2, gate2 = _mod_rows(mod_ref, row, (2, 3, 4, 5))
    fy = jnp.dot(pq_ref[...], mix_ref[...], preferred_element_type=F32)
    cat = jnp.concatenate([ay_ref[...], fy.astype(BF16)], axis=1)
    mo = jnp.dot(cat, wout_ref[...], preferred_element_type=F32)
    x1 = x_ref[...] + gate1 * mo
    o_ref[...] = x1 + gate2 * _mlp(x1, g2_ref[...], shift2, scale2, w1_ref, w2_ref)


def _mixout_mlp(x, ay, pq, mod, g2, mix_bf, wout_bf, w1_bf, w2_bf, mod_row, tm):
    bsz, seq_len, _ = x.shape
    row_spec = lambda w: pl.BlockSpec((None, tm, w), lambda b, t: (b, t, 0))
    body = functools.partial(_mixout_mlp_body, mod_row)
    return pl.pallas_call(
        body,
        grid=(bsz, seq_len // tm),
        in_specs=[row_spec(D), row_spec(CONV_CH), row_spec(2 * FOUR_CH),
                  _const_spec((8, N_MOD * D)), _const_spec((1, D)),
                  _const_spec((2 * FOUR_CH, FOUR_CH)), _const_spec((D, D)),
                  _const_spec((D, D_FF)), _const_spec((D_FF, D))],
        out_specs=row_spec(D),
        out_shape=jax.ShapeDtypeStruct((bsz, seq_len, D), F32),
        compiler_params=_cparams(2),
        name="mixer_out_mlp",
    )(x, ay, pq, mod, g2, mix_bf, wout_bf, w1_bf, w2_bf)


def _head_rms(t, bd_ref, gt):
    tt = (t * t).astype(BF16)
    w = bd_ref.shape[0]
    ms = jnp.concatenate(
        [jnp.dot(tt[:, j * w:(j + 1) * w], bd_ref[...], preferred_element_type=F32)
         for j in range(D // w)], axis=1)
    return t * lax.rsqrt(ms + EPS) * gt


def _qkv_common(mod_row, x_ref, mod_ref, g_ref, w_ref, bd_ref, qg_ref, kg_ref):
    b = pl.program_id(0)
    row = b if mod_row is None else mod_row
    shift, scale = _mod_rows(mod_ref, row, (0, 1))
    h = _rms_mod(x_ref[...], g_ref[...], shift, scale).astype(BF16)
    qkv = jnp.dot(h, w_ref[...], preferred_element_type=F32)
    q = _head_rms(qkv[:, 0:D], bd_ref, qg_ref[...]) * (HEAD_DIM ** -0.5 * LOG2E)
    k = _head_rms(qkv[:, D:2 * D], bd_ref, kg_ref[...])
    v = qkv[:, 2 * D:]
    return q, k, v


def _qkv_grid_body(x_ref, mod_ref, g_ref, w_ref, bd_ref, qg_ref, kg_ref, q_ref, k_ref, v_ref):
    q, k, v = _qkv_common(None, x_ref, mod_ref, g_ref, w_ref, bd_ref, qg_ref, kg_ref)
    for rho in range(TM // GRID_W):
        for g in range(NGRP):
            q0 = rho * GRID_W + QCOLS * g
            q_ref[g, rho * QCOLS:(rho + 1) * QCOLS, :] = q[q0:q0 + QCOLS].astype(BF16)
            k0 = rho * GRID_W + KC0[g]
            k_ref[g, rho * KCOLS:(rho + 1) * KCOLS, :] = k[k0:k0 + KCOLS].astype(BF16)
            v_ref[g, rho * KCOLS:(rho + 1) * KCOLS, :] = v[k0:k0 + KCOLS].astype(BF16)


def _qkv_ctx_body(mod_row, x_ref, mod_ref, g_ref, w_ref, bd_ref, qg_ref, kg_ref, k_ref, v_ref):
    _, k, v = _qkv_common(mod_row, x_ref, mod_ref, g_ref, w_ref, bd_ref, qg_ref, kg_ref)
    k_ref[...] = k.astype(BF16)
    v_ref[...] = v.astype(BF16)


def _qkv_consts(q_g, k_g):
    w = 256
    bd = np.kron(np.eye(w // HEAD_DIM), np.ones((HEAD_DIM, HEAD_DIM))) / HEAD_DIM
    qg = jnp.tile(q_g, N_HEADS).reshape(1, D)
    kg = jnp.tile(k_g, N_HEADS).reshape(1, D)
    return _bf16_const(bd), qg, kg


def _qkv_in_specs(tm):
    return [pl.BlockSpec((None, tm, D), lambda b, t: (b, t, 0)),
            _const_spec((8, N_MOD * D)), _const_spec((1, D)), _const_spec((D, 3 * D)),
            _const_spec((256, 256)), _const_spec((1, D)), _const_spec((1, D))]


def _qkv_grid(x, mod, g, wqkv_bf, q_g, k_g):
    bsz = x.shape[0]
    bd, qg, kg = _qkv_consts(q_g, k_g)
    nq = TM // GRID_W * QCOLS
    nk = TM // GRID_W * KCOLS
    return pl.pallas_call(
        _qkv_grid_body,
        grid=(bsz, SEQ // TM),
        in_specs=_qkv_in_specs(TM),
        out_specs=[pl.BlockSpec((None, NGRP, nq, D), lambda b, t: (b, 0, t, 0)),
                   pl.BlockSpec((None, NGRP, nk, D), lambda b, t: (b, 0, t, 0)),
                   pl.BlockSpec((None, NGRP, nk, D), lambda b, t: (b, 0, t, 0))],
        out_shape=[jax.ShapeDtypeStruct((bsz, NGRP, GRID_ROWS * QCOLS, D), BF16),
                   jax.ShapeDtypeStruct((bsz, NGRP, GRID_ROWS * KCOLS, D), BF16),
                   jax.ShapeDtypeStruct((bsz, NGRP, GRID_ROWS * KCOLS, D), BF16)],
        compiler_params=_cparams(2),
        name="qkv_grid",
    )(x, mod, g, wqkv_bf, bd, qg, kg)


def _qkv_ctx(ctx, mod, g, wqkv_bf, q_g, k_g, mod_row):
    bsz, n, _ = ctx.shape
    bd, qg, kg = _qkv_consts(q_g, k_g)
    spec = pl.BlockSpec((None, n, D), lambda b, t: (b, t, 0))
    return pl.pallas_call(
        functools.partial(_qkv_ctx_body, mod_row),
        grid=(bsz, 1),
        in_specs=_qkv_in_specs(n),
        out_specs=[spec, spec],
        out_shape=[jax.ShapeDtypeStruct((bsz, n, D), BF16)] * 2,
        compiler_params=_cparams(2),
        name="qkv_ctx",
    )(ctx, mod, g, wqkv_bf, bd, qg, kg)


def _bias_tables(rpb):
    n_ro = 2 * WIN_ROWS - 1
    n_co = 2 * WIN_COLS - 1
    g = np.arange(NGRP)[:, None, None]
    cq = np.arange(QCOLS)[None, :, None]
    kcw = (np.arange(128) % KCOLS)[None, None, :]
    c = QCOLS * g + cq
    kc = np.asarray(KC0)[:, None, None] + kcw
    cs = np.clip(c - WIN_COLS // 2, 0, GRID_W - WIN_COLS)
    col_ok = (kc >= cs) & (kc < cs + WIN_COLS)
    co = np.where(col_ok, kc - c + (WIN_COLS - 1), -1)
    onehot = (np.arange(n_co)[:, None, None, None] == co[None]).astype(np.float32)
    sel = jnp.einsum("hrc,cn->hrn", rpb, jnp.asarray(onehot.reshape(n_co, -1)),
                     precision=lax.Precision.HIGHEST)
    sel = sel.reshape(N_HEADS, n_ro, NGRP, QCOLS, 128).transpose(0, 2, 1, 3, 4) * LOG2E
    sel = jnp.where(col_ok[None, :, None], sel, NEG)
    rt = jnp.pad(sel, ((0, 0), (0, 0), (3, KROWS_PER_TILE - 1), (0, 0), (0, 0)))
    rt = rt.reshape(N_HEADS // 2, 2, NGRP, n_ro + 2 + KROWS_PER_TILE, QCOLS, 128)
    delta = np.arange(13)[:, None, None] - 8
    jj2 = (np.arange(128) // KCOLS)[None, None, :]
    rm = np.where((jj2 >= delta) & (jj2 < delta + WIN_ROWS), 0.0, NEG)
    rm = np.broadcast_to(rm, (13, QCOLS, 128)).astype(np.float32)
    return rt.astype(F32), jnp.asarray(rm)


def _tile_indices(rb, i, jq):
    wr = int(np.clip(RB * rb - WIN_ROWS // 2, 0, GRID_ROWS - WROWS))
    r = RB * rb + i
    delta = int(np.clip(r - WIN_ROWS // 2, 0, GRID_ROWS - WIN_ROWS)) - wr - KROWS_PER_TILE * jq
    if not -WIN_ROWS < delta < KROWS_PER_TILE:
        return None
    bti = wr + KROWS_PER_TILE * jq - r + (WIN_ROWS - 1) + 3
    assert 0 <= bti < N_BT and 0 <= delta + 8 < 13
    return bti, delta + 8


def _attn_body(q_ref, k_ref, v_ref, kc_ref, vc_ref, rt_ref, rm_ref, o_ref, s_scr, bt_scr, bm_scr):
    kc = kc_ref[...]
    vc = vc_ref[...]
    nq = RB * QCOLS
    nk = WROWS * KCOLS
    lane = lax.broadcasted_iota(jnp.int32, (nq, 128), 1)
    nt = nk // 128
    dn = (((1,), (1,)), ((), ()))

    lgrp = lax.broadcasted_iota(jnp.int32, (QCOLS, 128), 1) // KCOLS
    for e in range(2):
        for t in range(N_BT):
            blk = rt_ref[e, t + KROWS_PER_TILE - 1]
            for jj in range(KROWS_PER_TILE - 2, -1, -1):
                blk = jnp.where(lgrp == jj, rt_ref[e, t + jj], blk)
            bt_scr[e, t] = blk
    for e in range(2):
        for i in range(RB):
            for jq in range(nt):
                idx = _tile_indices(1, i, jq)
                if idx is not None:
                    bm_scr[e, i * nt + jq] = bt_scr[e, idx[0]] + rm_ref[idx[1]]

    def window_row(rb):
        return jnp.clip(RB * rb - WIN_ROWS // 2, 0, GRID_ROWS - WROWS)

    def scores(rb, slot):
        rb = jnp.asarray(rb, jnp.int32)
        koff = pl.multiple_of(window_row(rb) * KCOLS, 128)
        kw = k_ref[pl.ds(koff, nk), :]
        q = q_ref[pl.ds(pl.multiple_of(rb * nq, nq), nq), :]
        for e in range(2):
            in_head = (lane >= HEAD_DIM * e) & (lane < HEAD_DIM * (e + 1))
            qm = jnp.where(in_head, q, jnp.zeros_like(q))
            s_scr[slot, e, :, 0:nk] = lax.dot_general(qm, kw, dn, preferred_element_type=F32)
            s_scr[slot, e, :, nk:] = lax.dot_general(qm, kc, dn, preferred_element_type=F32)

    def finish(rb, slot, edge_rb):
        rb = jnp.asarray(rb, jnp.int32)
        wr = window_row(rb)
        vw = v_ref[pl.ds(pl.multiple_of(wr * KCOLS, 128), nk), :]
        zero_tile = jnp.zeros((QCOLS, 128), F32)
        outs = []
        for e in range(2):
            p_rows = []
            l_rows = []
            for i in range(RB):
                rows = slice(i * QCOLS, (i + 1) * QCOLS)
                blks = {}
                for jq in range(nt):
                    idx = _tile_indices(1 if edge_rb is None else edge_rb, i, jq)
                    if idx is None:
                        continue
                    add = bm_scr[e, i * nt + jq] if edge_rb is None else bt_scr[e, idx[0]] + rm_ref[idx[1]]
                    blks[jq] = s_scr[slot, e, rows, jq * 128:(jq + 1) * 128] + add
                for jc in range(CTX // 128):
                    blks[nt + jc] = s_scr[slot, e, rows, nk + jc * 128:nk + (jc + 1) * 128]
                vals = list(blks.values())
                m = jnp.max(functools.reduce(jnp.maximum, vals), axis=-1, keepdims=True)
                ps = {j: jnp.exp2(sb - m) for j, sb in blks.items()}
                l_rows.append(jnp.sum(functools.reduce(jnp.add, list(ps.values())), axis=-1, keepdims=True))
                p_rows.append(jnp.concatenate([ps.get(j, zero_tile) for j in range(nt + CTX // 128)], axis=1))
            p = jnp.concatenate(p_rows, axis=0).astype(BF16)
            lsum = jnp.concatenate(l_rows, axis=0)
            o = (jnp.dot(p[:, :nk], vw, preferred_element_type=F32)
                 + jnp.dot(p[:, nk:], vc, preferred_element_type=F32))
            outs.append(o / lsum)
        o = jnp.where(lane < HEAD_DIM, outs[0], outs[1])
        o_ref[pl.ds(pl.multiple_of(rb * nq, nq), nq), :] = o.astype(BF16)

    scores(0, 0)
    scores(1, 1)
    finish(0, 0, 0)
    scores(2, 0)
    finish(1, 1, None)

    def pair(tt, carry):
        t0 = 2 * tt
        scores(t0 + 1, 1)
        finish(t0, 0, None)
        scores(t0 + 2, 0)
        finish(t0 + 1, 1, None)
        return carry

    lax.fori_loop(1, NRB // 2 - 1, pair, 0)
    scores(NRB - 1, 1)
    finish(NRB - 2, 0, None)
    finish(NRB - 1, 1, NRB - 1)


def _attention(qcb, kcb, vcb, kc, vc, bt, rm):
    bsz = qcb.shape[0]
    nhp = N_HEADS // 2
    n_t = bt.shape[3]
    return pl.pallas_call(
        _attn_body,
        grid=(bsz, nhp, NGRP),
        in_specs=[pl.BlockSpec((None, None, GRID_ROWS * QCOLS, 128), lambda b, h, g: (b, g, 0, h)),
                  pl.BlockSpec((None, None, GRID_ROWS * KCOLS, 128), lambda b, h, g: (b, g, 0, h)),
                  pl.BlockSpec((None, None, GRID_ROWS * KCOLS, 128), lambda b, h, g: (b, g, 0, h)),
                  pl.BlockSpec((None, CTX, 128), lambda b, h, g: (b, 0, h)),
                  pl.BlockSpec((None, CTX, 128), lambda b, h, g: (b, 0, h)),
                  pl.BlockSpec((None, 2, None, n_t, QCOLS, 128), lambda b, h, g: (h, 0, g, 0, 0, 0)),
                  _const_spec((13, QCOLS, 128))],
        out_specs=pl.BlockSpec((None, None, GRID_ROWS * QCOLS, 128), lambda b, h, g: (b, g, 0, h)),
        out_shape=jax.ShapeDtypeStruct((bsz, NGRP, GRID_ROWS * QCOLS, D), BF16),
        scratch_shapes=[pltpu.VMEM((2, 2, RB * QCOLS, WROWS * KCOLS + CTX), F32),
                        pltpu.VMEM((2, N_BT, QCOLS, 128), F32),
                        pltpu.VMEM((2, RB * WROWS * KCOLS // 128, QCOLS, 128), F32)],
        compiler_params=_cparams(3),
        name="nbr_attention",
    )(qcb, kcb, vcb, kc, vc, bt, rm)


def _attnout_mlp_body(x_ref, o_ref_in, mod_ref, g2_ref, wo_ref, w1_ref, w2_ref, out_ref):
    b = pl.program_id(0)
    gate1, shift2, scale2, gate2 = _mod_rows(mod_ref, b, (2, 3, 4, 5))
    chunks = [o_ref_in[g, rho * QCOLS:(rho + 1) * QCOLS, :]
              for rho in range(TM // GRID_W) for g in range(NGRP)]
    o_nat = jnp.concatenate(chunks, axis=0)
    mo = jnp.dot(o_nat, wo_ref[...], preferred_element_type=F32)
    x1 = x_ref[...] + gate1 * mo
    out_ref[...] = x1 + gate2 * _mlp(x1, g2_ref[...], shift2, scale2, w1_ref, w2_ref)


def _attnout_mlp(x, ocb, mod, g2, wo_bf, w1_bf, w2_bf):
    bsz = x.shape[0]
    nq = TM // GRID_W * QCOLS
    return pl.pallas_call(
        _attnout_mlp_body,
        grid=(bsz, SEQ // TM),
        in_specs=[pl.BlockSpec((None, TM, D), lambda b, t: (b, t, 0)),
                  pl.BlockSpec((None, NGRP, nq, D), lambda b, t: (b, 0, t, 0)),
                  _const_spec((8, N_MOD * D)), _const_spec((1, D)),
                  _const_spec((D, D)), _const_spec((D, D_FF)), _const_spec((D_FF, D))],
        out_specs=pl.BlockSpec((None, TM, D), lambda b, t: (b, t, 0)),
        out_shape=jax.ShapeDtypeStruct((bsz, SEQ, D), F32),
        compiler_params=_cparams(2),
        name="attn_out_mlp",
    )(x, ocb, mod, g2, wo_bf, w1_bf, w2_bf)


def kernel(x, c, ctx, c_ctx, ada_w, ada_b, norm_mix_g, norm_mlp_g, mlp_w1, mlp_w2, ab_w_in, ab_conv_w,
           ab_w_out, na_w_qkv, na_q_g, na_k_g, na_rpb, na_w_out):
    bsz = x.shape[0]
    ctx_row = bsz
    cond8 = jnp.zeros((8, D), F32).at[:bsz].set(c).at[ctx_row].set(c_ctx)
    mods = _modulation(cond8, ada_w, ada_b)

    bf = lambda w: w.astype(BF16)
    g_mix = norm_mix_g.reshape(DEPTH, 1, D)
    g_mlp = norm_mlp_g.reshape(DEPTH, 1, D)
    conv_w8 = jnp.zeros((8, CONV_CH), F32).at[:3].set(ab_conv_w[0])

    w_in, w_out0 = bf(ab_w_in[0]), bf(ab_w_out[0])
    w1_0, w2_0 = bf(mlp_w1[0]), bf(mlp_w2[0])
    ay, f = _inproj(x, mods[0], g_mix[0], w_in, conv_w8, None, TM)
    pq = _dft_positions(f)
    mix_x = _bf16_const(_channel_mix_const(SEQ))
    x = _mixout_mlp(x, ay, pq, mods[0], g_mlp[0], mix_x, w_out0, w1_0, w2_0, None, TM)

    ay_c, f_c = _inproj(ctx, mods[0], g_mix[0], w_in, conv_w8, ctx_row, CTX)
    pq_c = _dft_positions_small(f_c)
    mix_c = _bf16_const(_channel_mix_const(CTX))
    ctx = _mixout_mlp(ctx, ay_c, pq_c, mods[0], g_mlp[0], mix_c, w_out0, w1_0, w2_0, ctx_row, CTX)

    wqkv, wo = bf(na_w_qkv[0]), bf(na_w_out[0])
    qcb, kcb, vcb = _qkv_grid(x, mods[1], g_mix[1], wqkv, na_q_g[0], na_k_g[0])
    kc, vc = _qkv_ctx(ctx, mods[1], g_mix[1], wqkv, na_q_g[0], na_k_g[0], ctx_row)
    bt, rm = _bias_tables(na_rpb[0])
    ocb = _attention(qcb, kcb, vcb, kc, vc, bt, rm)
    x = _attnout_mlp(x, ocb, mods[1], g_mlp[1], wo, bf(mlp_w1[1]), bf(mlp_w2[1]))
    return x
```

```python
import functools

import numpy as np
import jax
import jax.numpy as jnp
from jax import lax
from jax.experimental import pallas as pl
from jax.experimental.pallas import tpu as pltpu

F32 = jnp.float32
BF16 = jnp.bfloat16

D = 1024
DEPTH = 2
SEQ = 8192
CTX = 256
GRID_W = 64
GRID_ROWS = SEQ // GRID_W
HEAD_DIM = 64
N_HEADS = 16
CONV_CH = 512
FOUR_CH = 512
FOUR_GROUP = 64
WIN_ROWS = 8
WIN_COLS = 16
D_FF = 4 * D
N_MOD = 6
EPS = 1e-6

TM = 512
HALO = 8
FF_CHUNK = 1024
VMEM_LIMIT = 56 * 1024 * 1024

DFT_A = 32
DFT_B = SEQ // DFT_A
SUB = 8

QCOLS = 16
NGRP = GRID_W // QCOLS
KCOLS = 32
KC0 = tuple(int(np.clip(QCOLS * g - 8, 0, GRID_W - KCOLS)) for g in range(NGRP))
RB = 8
NRB = GRID_ROWS // RB
WROWS = 16
KROWS_PER_TILE = 128 // KCOLS
NEG = -1e30
LOG2E = float(np.log2(np.e))
N_BT = 2 * WIN_ROWS - 1 + KROWS_PER_TILE - 1


def _cparams(n_axes):
    return pltpu.CompilerParams(dimension_semantics=("arbitrary",) * n_axes,
                                vmem_limit_bytes=VMEM_LIMIT)


def _const_spec(shape):
    nd = len(shape)
    return pl.BlockSpec(shape, lambda *_: (0,) * nd)


def _bf16_const(a):
    return jnp.asarray(a, F32).astype(BF16)


def _rms_mod(x, g, shift, scale):
    ms = jnp.mean(x * x, axis=-1, keepdims=True)
    y = x * lax.rsqrt(ms + EPS)
    return (y * g) * (1.0 + scale) + shift


def _mod_rows(mod_ref, row, ks):
    return [mod_ref[pl.ds(row, 1), k * D:(k + 1) * D] for k in ks]


def _mlp(x1, g, shift, scale, w1_ref, w2_ref):
    h = _rms_mod(x1, g, shift, scale).astype(BF16)
    acc = jnp.zeros(x1.shape, F32)
    for c in range(D_FF // FF_CHUNK):
        a = jnp.dot(h, w1_ref[:, c * FF_CHUNK:(c + 1) * FF_CHUNK], preferred_element_type=F32)
        a = jnp.maximum(a, 0.0)
        a = (a * a).astype(BF16)
        acc = acc + jnp.dot(a, w2_ref[c * FF_CHUNK:(c + 1) * FF_CHUNK, :], preferred_element_type=F32)
    return acc


def _mod_body(cond_ref, w_ref, b_ref, o_ref):
    c = cond_ref[...]
    s = c * jax.nn.sigmoid(c)
    o_ref[...] = jnp.dot(s.astype(BF16), w_ref[...].astype(BF16), preferred_element_type=F32) + b_ref[...]


def _modulation(cond8, ada_w, ada_b):
    nt = N_MOD * D // D
    return pl.pallas_call(
        _mod_body,
        grid=(DEPTH, nt),
        in_specs=[_const_spec((8, D)),
                  pl.BlockSpec((None, D, D), lambda l, n: (l, 0, n)),
                  pl.BlockSpec((None, 1, D), lambda l, n: (l, 0, n))],
        out_specs=pl.BlockSpec((None, 8, D), lambda l, n: (l, 0, n)),
        out_shape=jax.ShapeDtypeStruct((DEPTH, 8, N_MOD * D), F32),
        compiler_params=_cparams(2),
        name="adaln_mod",
    )(cond8, ada_w, ada_b.reshape(DEPTH, 1, N_MOD * D))


def _inproj_body(mod_row, seq_len, tm, xp_ref, x_ref, xn_ref, mod_ref, g_ref, w_ref, cw_ref,
                 ay_ref, f_ref):
    b = pl.program_id(0)
    t = pl.program_id(1)
    row = b if mod_row is None else mod_row
    xx = jnp.concatenate([xp_ref[...], x_ref[...], xn_ref[...]], axis=0)
    shift, scale = _mod_rows(mod_ref, row, (0, 1))
    h = _rms_mod(xx, g_ref[...], shift, scale).astype(BF16)
    u = jnp.dot(h, w_ref[...], preferred_element_type=F32)
    z = u[:, CONV_CH:2 * CONV_CH] * u[:, 0:CONV_CH]
    n = t * tm - HALO + lax.broadcasted_iota(jnp.int32, (tm + 2 * HALO, 1), 0)
    z = jnp.where((n >= 0) & (n < seq_len), z, 0.0)
    cw = cw_ref[...]
    zc = (z[HALO - 1:HALO - 1 + tm] * cw[0:1] + z[HALO:HALO + tm] * cw[1:2]
          + z[HALO + 1:HALO + 1 + tm] * cw[2:3])
    ay_ref[...] = (u[HALO:HALO + tm, 2 * CONV_CH:3 * CONV_CH] * zc).astype(BF16)
    f_ref[...] = u[HALO:HALO + tm, 3 * CONV_CH:]


def _inproj(x, mod, g, w_in_bf, conv_w8, mod_row, tm):
    bsz, seq_len, _ = x.shape
    nt = seq_len // tm
    nb8 = seq_len // HALO
    r8 = tm // HALO
    body = functools.partial(_inproj_body, mod_row, seq_len, tm)
    return pl.pallas_call(
        body,
        grid=(bsz, nt),
        in_specs=[pl.BlockSpec((None, HALO, D), lambda b, t: (b, jnp.maximum(t * r8 - 1, 0), 0)),
                  pl.BlockSpec((None, tm, D), lambda b, t: (b, t, 0)),
                  pl.BlockSpec((None, HALO, D), lambda b, t: (b, jnp.minimum((t + 1) * r8, nb8 - 1), 0)),
                  _const_spec((8, N_MOD * D)),
                  _const_spec((1, D)),
                  _const_spec((D, 3 * CONV_CH + FOUR_CH)),
                  _const_spec((8, CONV_CH))],
        out_specs=[pl.BlockSpec((None, tm, CONV_CH), lambda b, t: (b, t, 0)),
                   pl.BlockSpec((None, tm, FOUR_CH), lambda b, t: (b, t, 0))],
        out_shape=[jax.ShapeDtypeStruct((bsz, seq_len, CONV_CH), BF16),
                   jax.ShapeDtypeStruct((bsz, seq_len, FOUR_CH), F32)],
        compiler_params=_cparams(2),
        name="mixer_in",
    )(x, x, x, mod, g, w_in_bf, conv_w8)


def _dft_consts():
    a = np.arange(DFT_A)
    s = np.arange(SUB)
    ang = 2.0 * np.pi * np.outer(a, a) / DFT_A
    eye = np.eye(SUB)
    re = np.einsum("va,ts->vtas", np.cos(ang), eye).reshape(DFT_A * SUB, DFT_A * SUB)
    im = np.einsum("va,ts->vtas", -np.sin(ang), eye).reshape(DFT_A * SUB, DFT_A * SUB)
    l1 = np.concatenate([re, im], axis=0)
    m = np.arange(DFT_B // SUB)
    bb = (SUB * m[:, None, None] + s[None, None, :])
    tang = 2.0 * np.pi * a[None, :, None] * bb / SEQ
    twr = np.cos(tang).reshape(DFT_B // SUB, DFT_A * SUB, 1)
    twi = (-np.sin(tang)).reshape(DFT_B // SUB, DFT_A * SUB, 1)
    u = np.arange(DFT_B)
    ang2 = 2.0 * np.pi * np.outer(u, u) / DFT_B
    c2, s2 = np.cos(ang2), np.sin(ang2)
    l2 = np.block([[c2, s2], [s2, -c2]])
    return l1, twr, twi, l2


DFT_MPAIR = 2
DFT_VB = 4


def _dft_a_body(x_ref, l1_ref, twr_ref, twi_ref, br_ref, bi_ref):
    rows = DFT_A * SUB
    reps = FOUR_CH // 128
    x = x_ref[...]
    brs, bis = [], []
    for j in range(DFT_MPAIR):
        xj = x[:, j * SUB:(j + 1) * SUB, :].reshape(rows, FOUR_CH).astype(BF16)
        a = jnp.dot(l1_ref[...], xj, preferred_element_type=F32)
        ar, ai = a[:rows], a[rows:]
        twr = jnp.concatenate([twr_ref[j]] * reps, axis=1)
        twi = jnp.concatenate([twi_ref[j]] * reps, axis=1)
        brs.append((ar * twr - ai * twi).reshape(DFT_A, SUB, FOUR_CH))
        bis.append((ar * twi + ai * twr).reshape(DFT_A, SUB, FOUR_CH))
    br_ref[...] = jnp.concatenate(brs, axis=1).astype(BF16)
    bi_ref[...] = jnp.concatenate(bis, axis=1).astype(BF16)


def _channel_dft(pq, n, mix_ref):
    pq_b = jnp.concatenate([pq[:n], pq[n:]], axis=1).astype(BF16)
    return jnp.dot(pq_b, mix_ref[...], preferred_element_type=F32).astype(BF16)


def _dft_b_body(br_ref, bi_ref, l2_ref, mix_ref, fy_ref):
    for j in range(DFT_VB):
        bm = jnp.concatenate([br_ref[j], bi_ref[j]], axis=0)
        pq = jnp.dot(l2_ref[...], bm, preferred_element_type=F32)
        fy_ref[j] = _channel_dft(pq, DFT_B, mix_ref)


def _dft_2d(f, mix_bf):
    bsz = f.shape[0]
    l1, twr, twi, l2 = _dft_consts()
    nm = DFT_B // SUB
    nm2 = nm // DFT_MPAIR
    rows = DFT_A * SUB
    twr_b = jnp.broadcast_to(jnp.asarray(twr, F32), (nm, rows, 128))
    twi_b = jnp.broadcast_to(jnp.asarray(twi, F32), (nm, rows, 128))
    f5 = f.reshape(bsz, DFT_A, nm2, DFT_MPAIR * SUB, FOUR_CH)
    blk5 = pl.BlockSpec((None, DFT_A, None, DFT_MPAIR * SUB, FOUR_CH), lambda m, b: (b, 0, m, 0, 0))
    tw_spec = pl.BlockSpec((DFT_MPAIR, rows, 128), lambda m, b: (m, 0, 0))
    br, bi = pl.pallas_call(
        _dft_a_body,
        grid=(nm2, bsz),
        in_specs=[blk5, _const_spec((2 * rows, rows)), tw_spec, tw_spec],
        out_specs=[blk5, blk5],
        out_shape=[jax.ShapeDtypeStruct((bsz, DFT_A, nm2, DFT_MPAIR * SUB, FOUR_CH), BF16)] * 2,
        compiler_params=_cparams(2),
        name="dft_stage_a",
    )(f5, _bf16_const(l1), twr_b, twi_b)
    br = br.reshape(bsz, DFT_A, DFT_B, FOUR_CH)
    bi = bi.reshape(bsz, DFT_A, DFT_B, FOUR_CH)
    blk4 = pl.BlockSpec((None, DFT_VB, DFT_B, FOUR_CH), lambda b, v: (b, v, 0, 0))
    fy = pl.pallas_call(
        _dft_b_body,
        grid=(bsz, DFT_A // DFT_VB),
        in_specs=[blk4, blk4, _const_spec((2 * DFT_B, 2 * DFT_B)), _const_spec((2 * FOUR_CH, FOUR_CH))],
        out_specs=blk4,
        out_shape=jax.ShapeDtypeStruct((bsz, DFT_A, DFT_B, FOUR_CH), BF16),
        compiler_params=_cparams(2),
        name="dft_stage_b",
    )(br, bi, _bf16_const(l2), mix_bf)
    return jnp.transpose(fy, (0, 2, 1, 3)).reshape(bsz, SEQ, FOUR_CH)


def _dft_small_body(f_ref, lc_ref, mix_ref, fy_ref):
    n = f_ref.shape[0]
    pq = jnp.dot(lc_ref[...], f_ref[...].astype(BF16), preferred_element_type=F32)
    fy_ref[...] = _channel_dft(pq, n, mix_ref)


def _dft_2d_small(f, mix_bf):
    bsz, n, _ = f.shape
    k = np.arange(n)
    ang = 2.0 * np.pi * np.outer(k, k) / n
    lc = np.concatenate([np.cos(ang), np.sin(ang)], axis=0)
    spec = pl.BlockSpec((None, n, FOUR_CH), lambda b: (b, 0, 0))
    return pl.pallas_call(
        _dft_small_body,
        grid=(bsz,),
        in_specs=[spec, _const_spec((2 * n, n)), _const_spec((2 * FOUR_CH, FOUR_CH))],
        out_specs=spec,
        out_shape=jax.ShapeDtypeStruct((bsz, n, FOUR_CH), BF16),
        compiler_params=_cparams(1),
        name="dft_small",
    )(f, _bf16_const(lc), mix_bf)


def _channel_mix_const(seq_len):
    k = np.arange(FOUR_GROUP)
    ang = 2.0 * np.pi * np.outer(k, k) / FOUR_GROUP
    ng = FOUR_CH // FOUR_GROUP
    bdc = np.kron(np.eye(ng), np.cos(ang))
    bds = np.kron(np.eye(ng), np.sin(ang))
    return np.concatenate([bdc, -bds], axis=0) / np.sqrt(seq_len * FOUR_GROUP)


def _mixout_mlp_body(mod_row, x_ref, ay_ref, fy_ref, mod_ref, g2_ref, wout_ref, w1_ref, w2_ref, o_ref):
    b = pl.program_id(0)
    row = b if mod_row is None else mod_row
    gate1, shift2, scale2, gate2 = _mod_rows(mod_ref, row, (2, 3, 4, 5))
    cat = jnp.concatenate([ay_ref[...], fy_ref[...]], axis=1)
    mo = jnp.dot(cat, wout_ref[...], preferred_element_type=F32)
    x1 = x_ref[...] + gate1 * mo
    o_ref[...] = x1 + gate2 * _mlp(x1, g2_ref[...], shift2, scale2, w1_ref, w2_ref)


def _mixout_mlp(x, ay, fy, mod, g2, wout_bf, w1_bf, w2_bf, mod_row, tm):
    bsz, seq_len, _ = x.shape
    row_spec = lambda w: pl.BlockSpec((None, tm, w), lambda b, t: (b, t, 0))
    body = functools.partial(_mixout_mlp_body, mod_row)
    return pl.pallas_call(
        body,
        grid=(bsz, seq_len // tm),
        in_specs=[row_spec(D), row_spec(CONV_CH), row_spec(FOUR_CH),
                  _const_spec((8, N_MOD * D)), _const_spec((1, D)),
                  _const_spec((D, D)), _const_spec((D, D_FF)), _const_spec((D_FF, D))],
        out_specs=row_spec(D),
        out_shape=jax.ShapeDtypeStruct((bsz, seq_len, D), F32),
        compiler_params=_cparams(2),
        name="mixer_out_mlp",
    )(x, ay, fy, mod, g2, wout_bf, w1_bf, w2_bf)


def _head_rms(t, bd_ref, gt):
    tt = (t * t).astype(BF16)
    w = bd_ref.shape[0]
    ms = jnp.concatenate(
        [jnp.dot(tt[:, j * w:(j + 1) * w], bd_ref[...], preferred_element_type=F32)
         for j in range(D // w)], axis=1)
    return t * lax.rsqrt(ms + EPS) * gt


def _qkv_common(mod_row, x_ref, mod_ref, g_ref, w_ref, bd_ref, qg_ref, kg_ref):
    b = pl.program_id(0)
    row = b if mod_row is None else mod_row
    shift, scale = _mod_rows(mod_ref, row, (0, 1))
    h = _rms_mod(x_ref[...], g_ref[...], shift, scale).astype(BF16)
    qkv = jnp.dot(h, w_ref[...], preferred_element_type=F32)
    q = _head_rms(qkv[:, 0:D], bd_ref, qg_ref[...]) * (HEAD_DIM ** -0.5 * LOG2E)
    k = _head_rms(qkv[:, D:2 * D], bd_ref, kg_ref[...])
    v = qkv[:, 2 * D:]
    return q, k, v


def _qkv_grid_body(x_ref, mod_ref, g_ref, w_ref, bd_ref, qg_ref, kg_ref, q_ref, k_ref, v_ref):
    q, k, v = _qkv_common(None, x_ref, mod_ref, g_ref, w_ref, bd_ref, qg_ref, kg_ref)
    for rho in range(TM // GRID_W):
        for g in range(NGRP):
            q0 = rho * GRID_W + QCOLS * g
            q_ref[g, rho * QCOLS:(rho + 1) * QCOLS, :] = q[q0:q0 + QCOLS].astype(BF16)
            k0 = rho * GRID_W + KC0[g]
            k_ref[g, rho * KCOLS:(rho + 1) * KCOLS, :] = k[k0:k0 + KCOLS].astype(BF16)
            v_ref[g, rho * KCOLS:(rho + 1) * KCOLS, :] = v[k0:k0 + KCOLS].astype(BF16)


def _qkv_ctx_body(mod_row, x_ref, mod_ref, g_ref, w_ref, bd_ref, qg_ref, kg_ref, k_ref, v_ref):
    _, k, v = _qkv_common(mod_row, x_ref, mod_ref, g_ref, w_ref, bd_ref, qg_ref, kg_ref)
    k_ref[...] = k.astype(BF16)
    v_ref[...] = v.astype(BF16)


def _qkv_consts(q_g, k_g):
    w = 256
    bd = np.kron(np.eye(w // HEAD_DIM), np.ones((HEAD_DIM, HEAD_DIM))) / HEAD_DIM
    qg = jnp.tile(q_g, N_HEADS).reshape(1, D)
    kg = jnp.tile(k_g, N_HEADS).reshape(1, D)
    return _bf16_const(bd), qg, kg


def _qkv_in_specs(tm):
    return [pl.BlockSpec((None, tm, D), lambda b, t: (b, t, 0)),
            _const_spec((8, N_MOD * D)), _const_spec((1, D)), _const_spec((D, 3 * D)),
            _const_spec((256, 256)), _const_spec((1, D)), _const_spec((1, D))]


def _qkv_grid(x, mod, g, wqkv_bf, q_g, k_g):
    bsz = x.shape[0]
    bd, qg, kg = _qkv_consts(q_g, k_g)
    nq = TM // GRID_W * QCOLS
    nk = TM // GRID_W * KCOLS
    return pl.pallas_call(
        _qkv_grid_body,
        grid=(bsz, SEQ // TM),
        in_specs=_qkv_in_specs(TM),
        out_specs=[pl.BlockSpec((None, NGRP, nq, D), lambda b, t: (b, 0, t, 0)),
                   pl.BlockSpec((None, NGRP, nk, D), lambda b, t: (b, 0, t, 0)),
                   pl.BlockSpec((None, NGRP, nk, D), lambda b, t: (b, 0, t, 0))],
        out_shape=[jax.ShapeDtypeStruct((bsz, NGRP, GRID_ROWS * QCOLS, D), BF16),
                   jax.ShapeDtypeStruct((bsz, NGRP, GRID_ROWS * KCOLS, D), BF16),
                   jax.ShapeDtypeStruct((bsz, NGRP, GRID_ROWS * KCOLS, D), BF16)],
        compiler_params=_cparams(2),
        name="qkv_grid",
    )(x, mod, g, wqkv_bf, bd, qg, kg)


def _qkv_ctx(ctx, mod, g, wqkv_bf, q_g, k_g, mod_row):
    bsz, n, _ = ctx.shape
    bd, qg, kg = _qkv_consts(q_g, k_g)
    spec = pl.BlockSpec((None, n, D), lambda b, t: (b, t, 0))
    return pl.pallas_call(
        functools.partial(_qkv_ctx_body, mod_row),
        grid=(bsz, 1),
        in_specs=_qkv_in_specs(n),
        out_specs=[spec, spec],
        out_shape=[jax.ShapeDtypeStruct((bsz, n, D), BF16)] * 2,
        compiler_params=_cparams(2),
        name="qkv_ctx",
    )(ctx, mod, g, wqkv_bf, bd, qg, kg)


def _bias_tables(rpb):
    n_ro = 2 * WIN_ROWS - 1
    n_co = 2 * WIN_COLS - 1
    g = np.arange(NGRP)[:, None, None]
    cq = np.arange(QCOLS)[None, :, None]
    kcw = (np.arange(128) % KCOLS)[None, None, :]
    c = QCOLS * g + cq
    kc = np.asarray(KC0)[:, None, None] + kcw
    cs = np.clip(c - WIN_COLS // 2, 0, GRID_W - WIN_COLS)
    col_ok = (kc >= cs) & (kc < cs + WIN_COLS)
    co = np.where(col_ok, kc - c + (WIN_COLS - 1), -1)
    onehot = (np.arange(n_co)[:, None, None, None] == co[None]).astype(np.float32)
    sel = jnp.einsum("hrc,cn->hrn", rpb, jnp.asarray(onehot.reshape(n_co, -1)),
                     precision=lax.Precision.HIGHEST)
    sel = sel.reshape(N_HEADS, n_ro, NGRP, QCOLS, 128).transpose(0, 2, 1, 3, 4) * LOG2E
    sel = jnp.where(col_ok[None, :, None], sel, NEG)
    rt = jnp.pad(sel, ((0, 0), (0, 0), (3, KROWS_PER_TILE - 1), (0, 0), (0, 0)))
    rt = rt.reshape(N_HEADS // 2, 2, NGRP, n_ro + 2 + KROWS_PER_TILE, QCOLS, 128)
    delta = np.arange(13)[:, None, None] - 8
    jj2 = (np.arange(128) // KCOLS)[None, None, :]
    rm = np.where((jj2 >= delta) & (jj2 < delta + WIN_ROWS), 0.0, NEG)
    rm = np.broadcast_to(rm, (13, QCOLS, 128)).astype(np.float32)
    return rt.astype(F32), jnp.asarray(rm)


def _tile_indices(rb, i, jq):
    wr = int(np.clip(RB * rb - WIN_ROWS // 2, 0, GRID_ROWS - WROWS))
    r = RB * rb + i
    delta = int(np.clip(r - WIN_ROWS // 2, 0, GRID_ROWS - WIN_ROWS)) - wr - KROWS_PER_TILE * jq
    if not -WIN_ROWS < delta < KROWS_PER_TILE:
        return None
    bti = wr + KROWS_PER_TILE * jq - r + (WIN_ROWS - 1) + 3
    assert 0 <= bti < N_BT and 0 <= delta + 8 < 13
    return bti, delta + 8


def _attn_body(q_ref, k_ref, v_ref, kc_ref, vc_ref, rt_ref, rm_ref, o_ref, s_scr, bt_scr, bm_scr):
    kc = kc_ref[...]
    vc = vc_ref[...]
    nq = RB * QCOLS
    nk = WROWS * KCOLS
    lane = lax.broadcasted_iota(jnp.int32, (nq, 128), 1)
    nt = nk // 128
    dn = (((1,), (1,)), ((), ()))

    lgrp = lax.broadcasted_iota(jnp.int32, (QCOLS, 128), 1) // KCOLS
    for e in range(2):
        for t in range(N_BT):
            blk = rt_ref[e, t + KROWS_PER_TILE - 1]
            for jj in range(KROWS_PER_TILE - 2, -1, -1):
                blk = jnp.where(lgrp == jj, rt_ref[e, t + jj], blk)
            bt_scr[e, t] = blk
    for e in range(2):
        for i in range(RB):
            for jq in range(nt):
                idx = _tile_indices(1, i, jq)
                if idx is not None:
                    bm_scr[e, i * nt + jq] = bt_scr[e, idx[0]] + rm_ref[idx[1]]

    def window_row(rb):
        return jnp.clip(RB * rb - WIN_ROWS // 2, 0, GRID_ROWS - WROWS)

    def scores(rb, slot):
        rb = jnp.asarray(rb, jnp.int32)
        koff = pl.multiple_of(window_row(rb) * KCOLS, 128)
        kw = k_ref[pl.ds(koff, nk), :]
        q = q_ref[pl.ds(pl.multiple_of(rb * nq, nq), nq), :]
        for e in range(2):
            in_head = (lane >= HEAD_DIM * e) & (lane < HEAD_DIM * (e + 1))
            qm = jnp.where(in_head, q, jnp.zeros_like(q))
            s_scr[slot, e, :, 0:nk] = lax.dot_general(qm, kw, dn, preferred_element_type=F32)
            s_scr[slot, e, :, nk:] = lax.dot_general(qm, kc, dn, preferred_element_type=F32)

    def finish(rb, slot, edge_rb):
        rb = jnp.asarray(rb, jnp.int32)
        wr = window_row(rb)
        vw = v_ref[pl.ds(pl.multiple_of(wr * KCOLS, 128), nk), :]
        zero_tile = jnp.zeros((QCOLS, 128), F32)
        outs = []
        for e in range(2):
            p_rows = []
            l_rows = []
            for i in range(RB):
                rows = slice(i * QCOLS, (i + 1) * QCOLS)
                blks = {}
                for jq in range(nt):
                    idx = _tile_indices(1 if edge_rb is None else edge_rb, i, jq)
                    if idx is None:
                        continue
                    add = bm_scr[e, i * nt + jq] if edge_rb is None else bt_scr[e, idx[0]] + rm_ref[idx[1]]
                    blks[jq] = s_scr[slot, e, rows, jq * 128:(jq + 1) * 128] + add
                for jc in range(CTX // 128):
                    blks[nt + jc] = s_scr[slot, e, rows, nk + jc * 128:nk + (jc + 1) * 128]
                vals = list(blks.values())
                m = jnp.max(functools.reduce(jnp.maximum, vals), axis=-1, keepdims=True)
                ps = {j: jnp.exp2(sb - m) for j, sb in blks.items()}
                l_rows.append(jnp.sum(functools.reduce(jnp.add, list(ps.values())), axis=-1, keepdims=True))
                p_rows.append(jnp.concatenate([ps.get(j, zero_tile) for j in range(nt + CTX // 128)], axis=1))
            p = jnp.concatenate(p_rows, axis=0).astype(BF16)
            lsum = jnp.concatenate(l_rows, axis=0)
            o = (jnp.dot(p[:, :nk], vw, preferred_element_type=F32)
                 + jnp.dot(p[:, nk:], vc, preferred_element_type=F32))
            outs.append(o / lsum)
        o = jnp.where(lane < HEAD_DIM, outs[0], outs[1])
        o_ref[pl.ds(pl.multiple_of(rb * nq, nq), nq), :] = o.astype(BF16)

    scores(0, 0)
    scores(1, 1)
    finish(0, 0, 0)
    scores(2, 0)
    finish(1, 1, None)

    def pair(tt, carry):
        t0 = 2 * tt
        scores(t0 + 1, 1)
        finish(t0, 0, None)
        scores(t0 + 2, 0)
        finish(t0 + 1, 1, None)
        return carry

    lax.fori_loop(1, NRB // 2 - 1, pair, 0, unroll=True)
    scores(NRB - 1, 1)
    finish(NRB - 2, 0, None)
    finish(NRB - 1, 1, NRB - 1)


def _attention(qcb, kcb, vcb, kc, vc, bt, rm):
    bsz = qcb.shape[0]
    nhp = N_HEADS // 2
    n_t = bt.shape[3]
    return pl.pallas_call(
        _attn_body,
        grid=(bsz, nhp, NGRP),
        in_specs=[pl.BlockSpec((None, None, GRID_ROWS * QCOLS, 128), lambda b, h, g: (b, g, 0, h)),
                  pl.BlockSpec((None, None, GRID_ROWS * KCOLS, 128), lambda b, h, g: (b, g, 0, h)),
                  pl.BlockSpec((None, None, GRID_ROWS * KCOLS, 128), lambda b, h, g: (b, g, 0, h)),
                  pl.BlockSpec((None, CTX, 128), lambda b, h, g: (b, 0, h)),
                  pl.BlockSpec((None, CTX, 128), lambda b, h, g: (b, 0, h)),
                  pl.BlockSpec((None, 2, None, n_t, QCOLS, 128), lambda b, h, g: (h, 0, g, 0, 0, 0)),
                  _const_spec((13, QCOLS, 128))],
        out_specs=pl.BlockSpec((None, None, GRID_ROWS * QCOLS, 128), lambda b, h, g: (b, g, 0, h)),
        out_shape=jax.ShapeDtypeStruct((bsz, NGRP, GRID_ROWS * QCOLS, D), BF16),
        scratch_shapes=[pltpu.VMEM((2, 2, RB * QCOLS, WROWS * KCOLS + CTX), F32),
                        pltpu.VMEM((2, N_BT, QCOLS, 128), F32),
                        pltpu.VMEM((2, RB * WROWS * KCOLS // 128, QCOLS, 128), F32)],
        compiler_params=_cparams(3),
        name="nbr_attention",
    )(qcb, kcb, vcb, kc, vc, bt, rm)


def _attnout_mlp_body(x_ref, o_ref_in, mod_ref, g2_ref, wo_ref, w1_ref, w2_ref, out_ref):
    b = pl.program_id(0)
    gate1, shift2, scale2, gate2 = _mod_rows(mod_ref, b, (2, 3, 4, 5))
    chunks = [o_ref_in[g, rho * QCOLS:(rho + 1) * QCOLS, :]
              for rho in range(TM // GRID_W) for g in range(NGRP)]
    o_nat = jnp.concatenate(chunks, axis=0)
    mo = jnp.dot(o_nat, wo_ref[...], preferred_element_type=F32)
    x1 = x_ref[...] + gate1 * mo
    out_ref[...] = x1 + gate2 * _mlp(x1, g2_ref[...], shift2, scale2, w1_ref, w2_ref)


def _attnout_mlp(x, ocb, mod, g2, wo_bf, w1_bf, w2_bf):
    bsz = x.shape[0]
    nq = TM // GRID_W * QCOLS
    return pl.pallas_call(
        _attnout_mlp_body,
        grid=(bsz, SEQ // TM),
        in_specs=[pl.BlockSpec((None, TM, D), lambda b, t: (b, t, 0)),
                  pl.BlockSpec((None, NGRP, nq, D), lambda b, t: (b, 0, t, 0)),
                  _const_spec((8, N_MOD * D)), _const_spec((1, D)),
                  _const_spec((D, D)), _const_spec((D, D_FF)), _const_spec((D_FF, D))],
        out_specs=pl.BlockSpec((None, TM, D), lambda b, t: (b, t, 0)),
        out_shape=jax.ShapeDtypeStruct((bsz, SEQ, D), F32),
        compiler_params=_cparams(2),
        name="attn_out_mlp",
    )(x, ocb, mod, g2, wo_bf, w1_bf, w2_bf)


def kernel(x, c, ctx, c_ctx, ada_w, ada_b, norm_mix_g, norm_mlp_g, mlp_w1, mlp_w2, ab_w_in, ab_conv_w,
           ab_w_out, na_w_qkv, na_q_g, na_k_g, na_rpb, na_w_out):
    bsz = x.shape[0]
    ctx_row = bsz
    cond8 = jnp.zeros((8, D), F32).at[:bsz].set(c).at[ctx_row].set(c_ctx)
    mods = _modulation(cond8, ada_w, ada_b)

    bf = lambda w: w.astype(BF16)
    g_mix = norm_mix_g.reshape(DEPTH, 1, D)
    g_mlp = norm_mlp_g.reshape(DEPTH, 1, D)
    conv_w8 = jnp.zeros((8, CONV_CH), F32).at[:3].set(ab_conv_w[0])

    w_in, w_out0 = bf(ab_w_in[0]), bf(ab_w_out[0])
    w1_0, w2_0 = bf(mlp_w1[0]), bf(mlp_w2[0])
    ay, f = _inproj(x, mods[0], g_mix[0], w_in, conv_w8, None, TM)
    fy = _dft_2d(f, _bf16_const(_channel_mix_const(SEQ)))
    x = _mixout_mlp(x, ay, fy, mods[0], g_mlp[0], w_out0, w1_0, w2_0, None, TM)

    ay_c, f_c = _inproj(ctx, mods[0], g_mix[0], w_in, conv_w8, ctx_row, CTX)
    fy_c = _dft_2d_small(f_c, _bf16_const(_channel_mix_const(CTX)))
    flat = lambda a: a.reshape(1, bsz * CTX, a.shape[-1])
    ctx = _mixout_mlp(flat(ctx), flat(ay_c), flat(fy_c), mods[0], g_mlp[0], w_out0, w1_0, w2_0,
                      ctx_row, bsz * CTX)

    wqkv, wo = bf(na_w_qkv[0]), bf(na_w_out[0])
    qcb, kcb, vcb = _qkv_grid(x, mods[1], g_mix[1], wqkv, na_q_g[0], na_k_g[0])
    kc, vc = _qkv_ctx(ctx, mods[1], g_mix[1], wqkv, na_q_g[0], na_k_g[0], ctx_row)
    kc, vc = kc.reshape(bsz, CTX, D), vc.reshape(bsz, CTX, D)
    bt, rm = _bias_tables(na_rpb[0])
    ocb = _attention(qcb, kcb, vcb, kc, vc, bt, rm)
    x = _attnout_mlp(x, ocb, mods[1], g_mlp[1], wo, bf(mlp_w1[1]), bf(mlp_w2[1]))
    return x
```

```python
import functools

import numpy as np
import jax
import jax.numpy as jnp
from jax import lax
from jax.experimental import pallas as pl
from jax.experimental.pallas import tpu as pltpu

F32 = jnp.float32
BF16 = jnp.bfloat16

D = 1024
DEPTH = 2
SEQ = 8192
CTX = 256
GRID_W = 64
GRID_ROWS = SEQ // GRID_W
HEAD_DIM = 64
N_HEADS = 16
CONV_CH = 512
FOUR_CH = 512
FOUR_GROUP = 64
WIN_ROWS = 8
WIN_COLS = 16
D_FF = 4 * D
N_MOD = 6
EPS = 1e-6

TM = 512
TM_IN = 1024
HALO = 8
FF_CHUNK = 1024
VMEM_LIMIT = 56 * 1024 * 1024

DFT_A = 32
DFT_B = SEQ // DFT_A
SUB = 8

QCOLS = 16
NGRP = GRID_W // QCOLS
KCOLS = 32
KC0 = tuple(int(np.clip(QCOLS * g - 8, 0, GRID_W - KCOLS)) for g in range(NGRP))
RB = 8
NRB = GRID_ROWS // RB
WROWS = 16
KROWS_PER_TILE = 128 // KCOLS
NEG = -1e30
LOG2E = float(np.log2(np.e))
N_BT = 2 * WIN_ROWS - 1 + KROWS_PER_TILE - 1


def _cparams(n_axes):
    return pltpu.CompilerParams(dimension_semantics=("arbitrary",) * n_axes,
                                vmem_limit_bytes=VMEM_LIMIT)


def _const_spec(shape):
    nd = len(shape)
    return pl.BlockSpec(shape, lambda *_: (0,) * nd)


def _layer_spec(shape, layer):
    return pl.BlockSpec((None,) + tuple(shape), lambda *_: (layer,) + (0,) * len(shape))


def _bf16_const(a):
    return jnp.asarray(a, F32).astype(BF16)


def _rms_mod(x, g, shift, scale):
    ms = jnp.mean(x * x, axis=-1, keepdims=True)
    y = x * lax.rsqrt(ms + EPS)
    return (y * g) * (1.0 + scale) + shift


def _mod_rows(mod_ref, row, ks):
    return [mod_ref[pl.ds(row, 1), k * D:(k + 1) * D] for k in ks]


def _mlp(x1, g, shift, scale, w1_ref, w2_ref):
    h = _rms_mod(x1, g, shift, scale).astype(BF16)
    acc = jnp.zeros(x1.shape, F32)
    for c in range(D_FF // FF_CHUNK):
        a = jnp.dot(h, w1_ref[:, c * FF_CHUNK:(c + 1) * FF_CHUNK], preferred_element_type=F32)
        a = jnp.maximum(a, 0.0)
        a = (a * a).astype(BF16)
        acc = acc + jnp.dot(a, w2_ref[c * FF_CHUNK:(c + 1) * FF_CHUNK, :], preferred_element_type=F32)
    return acc


def _mod_body(cond_ref, w_ref, b_ref, o_ref):
    c = cond_ref[...]
    s = c * jax.nn.sigmoid(c)
    o_ref[...] = jnp.dot(s.astype(BF16), w_ref[...].astype(BF16), preferred_element_type=F32) + b_ref[...]


def _modulation(cond8, ada_w, ada_b):
    nt = N_MOD * D // D
    return pl.pallas_call(
        _mod_body,
        grid=(DEPTH, nt),
        in_specs=[_const_spec((8, D)),
                  pl.BlockSpec((None, D, D), lambda l, n: (l, 0, n)),
                  pl.BlockSpec((None, 1, D), lambda l, n: (l, 0, n))],
        out_specs=pl.BlockSpec((None, 8, D), lambda l, n: (l, 0, n)),
        out_shape=jax.ShapeDtypeStruct((DEPTH, 8, N_MOD * D), F32),
        compiler_params=_cparams(2),
        name="adaln_mod",
    )(cond8, ada_w, ada_b.reshape(DEPTH, 1, N_MOD * D))


def _inproj_body(mod_row, seq_len, tm, xp_ref, x_ref, xn_ref, mod_ref, g_ref, w_ref, cw_ref,
                 ay_ref, f_ref):
    b = pl.program_id(0)
    t = pl.program_id(1)
    row = b if mod_row is None else mod_row
    xx = jnp.concatenate([xp_ref[...], x_ref[...], xn_ref[...]], axis=0)
    shift, scale = _mod_rows(mod_ref, row, (0, 1))
    h = _rms_mod(xx, g_ref[...], shift, scale).astype(BF16)
    u = jnp.dot(h, w_ref[...], preferred_element_type=F32)
    z = u[:, CONV_CH:2 * CONV_CH] * u[:, 0:CONV_CH]
    n = t * tm - HALO + lax.broadcasted_iota(jnp.int32, (tm + 2 * HALO, 1), 0)
    z = jnp.where((n >= 0) & (n < seq_len), z, 0.0)
    cw = cw_ref[...]
    zc = (z[HALO - 1:HALO - 1 + tm] * cw[0:1] + z[HALO:HALO + tm] * cw[1:2]
          + z[HALO + 1:HALO + 1 + tm] * cw[2:3])
    ay_ref[...] = (u[HALO:HALO + tm, 2 * CONV_CH:3 * CONV_CH] * zc).astype(BF16)
    f_ref[...] = u[HALO:HALO + tm, 3 * CONV_CH:]


def _inproj(x, mod, g, w_in_bf, conv_w8, mod_row, tm):
    bsz, seq_len, _ = x.shape
    nt = seq_len // tm
    nb8 = seq_len // HALO
    r8 = tm // HALO
    body = functools.partial(_inproj_body, mod_row, seq_len, tm)
    return pl.pallas_call(
        body,
        grid=(bsz, nt),
        in_specs=[pl.BlockSpec((None, HALO, D), lambda b, t: (b, jnp.maximum(t * r8 - 1, 0), 0)),
                  pl.BlockSpec((None, tm, D), lambda b, t: (b, t, 0)),
                  pl.BlockSpec((None, HALO, D), lambda b, t: (b, jnp.minimum((t + 1) * r8, nb8 - 1), 0)),
                  _const_spec((8, N_MOD * D)),
                  _const_spec((1, D)),
                  _const_spec((D, 3 * CONV_CH + FOUR_CH)),
                  _const_spec((8, CONV_CH))],
        out_specs=[pl.BlockSpec((None, tm, CONV_CH), lambda b, t: (b, t, 0)),
                   pl.BlockSpec((None, tm, FOUR_CH), lambda b, t: (b, t, 0))],
        out_shape=[jax.ShapeDtypeStruct((bsz, seq_len, CONV_CH), BF16),
                   jax.ShapeDtypeStruct((bsz, seq_len, FOUR_CH), F32)],
        compiler_params=_cparams(2),
        name="mixer_in",
    )(x, x, x, mod, g, w_in_bf, conv_w8)


def _dft_consts():
    a = np.arange(DFT_A)
    s = np.arange(SUB)
    ang = 2.0 * np.pi * np.outer(a, a) / DFT_A
    eye = np.eye(SUB)
    re = np.einsum("va,ts->vtas", np.cos(ang), eye).reshape(DFT_A * SUB, DFT_A * SUB)
    im = np.einsum("va,ts->vtas", -np.sin(ang), eye).reshape(DFT_A * SUB, DFT_A * SUB)
    l1 = np.concatenate([re, im], axis=0)
    m = np.arange(DFT_B // SUB)
    bb = (SUB * m[:, None, None] + s[None, None, :])
    tang = 2.0 * np.pi * a[None, :, None] * bb / SEQ
    twr = np.cos(tang).reshape(DFT_B // SUB, DFT_A * SUB, 1)
    twi = (-np.sin(tang)).reshape(DFT_B // SUB, DFT_A * SUB, 1)
    u = np.arange(DFT_B)
    ang2 = 2.0 * np.pi * np.outer(u, u) / DFT_B
    c2, s2 = np.cos(ang2), np.sin(ang2)
    l2 = np.block([[c2, s2], [s2, -c2]])
    return l1, twr, twi, l2


DFT_MPAIR = 4
DFT_VB = 4


def _dft_a_body(x_ref, l1_ref, twr_ref, twi_ref, br_ref, bi_ref):
    rows = DFT_A * SUB
    reps = FOUR_CH // 128
    x = x_ref[...]
    brs, bis = [], []
    for j in range(DFT_MPAIR):
        xj = x[:, j * SUB:(j + 1) * SUB, :].reshape(rows, FOUR_CH).astype(BF16)
        a = jnp.dot(l1_ref[...], xj, preferred_element_type=F32)
        ar, ai = a[:rows], a[rows:]
        twr = jnp.concatenate([twr_ref[j]] * reps, axis=1)
        twi = jnp.concatenate([twi_ref[j]] * reps, axis=1)
        brs.append((ar * twr - ai * twi).reshape(DFT_A, SUB, FOUR_CH))
        bis.append((ar * twi + ai * twr).reshape(DFT_A, SUB, FOUR_CH))
    br_ref[...] = jnp.concatenate(brs, axis=1).astype(BF16)
    bi_ref[...] = jnp.concatenate(bis, axis=1).astype(BF16)


def _channel_dft(pq, n, mix_ref):
    w = 256
    outs = []
    for cb in range(FOUR_CH // w):
        cols = slice(cb * w, (cb + 1) * w)
        lhs = jnp.concatenate([pq[:n, cols], pq[n:, cols]], axis=1).astype(BF16)
        rhs = jnp.concatenate([mix_ref[cols, cols],
                               mix_ref[FOUR_CH + cb * w:FOUR_CH + (cb + 1) * w, cols]], axis=0)
        outs.append(jnp.dot(lhs, rhs, preferred_element_type=F32))
    return jnp.concatenate(outs, axis=1).astype(BF16)


def _dft_b_body(br_ref, bi_ref, l2_ref, mix_ref, fy_ref):
    for j in range(DFT_VB):
        bm = jnp.concatenate([br_ref[j], bi_ref[j]], axis=0)
        pq = jnp.dot(l2_ref[...], bm, preferred_element_type=F32)
        fy_ref[j] = _channel_dft(pq, DFT_B, mix_ref)


def _dft_2d(f, mix_bf):
    bsz = f.shape[0]
    l1, twr, twi, l2 = _dft_consts()
    nm = DFT_B // SUB
    nm2 = nm // DFT_MPAIR
    rows = DFT_A * SUB
    twr_b = jnp.broadcast_to(jnp.asarray(twr, F32), (nm, rows, 128))
    twi_b = jnp.broadcast_to(jnp.asarray(twi, F32), (nm, rows, 128))
    f5 = f.reshape(bsz, DFT_A, nm2, DFT_MPAIR * SUB, FOUR_CH)
    blk5 = pl.BlockSpec((None, DFT_A, None, DFT_MPAIR * SUB, FOUR_CH), lambda m, b: (b, 0, m, 0, 0))
    tw_spec = pl.BlockSpec((DFT_MPAIR, rows, 128), lambda m, b: (m, 0, 0))
    br, bi = pl.pallas_call(
        _dft_a_body,
        grid=(nm2, bsz),
        in_specs=[blk5, _const_spec((2 * rows, rows)), tw_spec, tw_spec],
        out_specs=[blk5, blk5],
        out_shape=[jax.ShapeDtypeStruct((bsz, DFT_A, nm2, DFT_MPAIR * SUB, FOUR_CH), BF16)] * 2,
        compiler_params=_cparams(2),
        name="dft_stage_a",
    )(f5, _bf16_const(l1), twr_b, twi_b)
    br = br.reshape(bsz, DFT_A, DFT_B, FOUR_CH)
    bi = bi.reshape(bsz, DFT_A, DFT_B, FOUR_CH)
    blk4 = pl.BlockSpec((None, DFT_VB, DFT_B, FOUR_CH), lambda b, v: (b, v, 0, 0))
    fy = pl.pallas_call(
        _dft_b_body,
        grid=(bsz, DFT_A // DFT_VB),
        in_specs=[blk4, blk4, _const_spec((2 * DFT_B, 2 * DFT_B)), _const_spec((2 * FOUR_CH, FOUR_CH))],
        out_specs=blk4,
        out_shape=jax.ShapeDtypeStruct((bsz, DFT_A, DFT_B, FOUR_CH), BF16),
        compiler_params=_cparams(2),
        name="dft_stage_b",
    )(br, bi, _bf16_const(l2), mix_bf)
    return jnp.transpose(fy, (0, 2, 1, 3)).reshape(bsz, SEQ, FOUR_CH)


def _dft_small_body(f_ref, lc_ref, mix_ref, fy_ref):
    n = f_ref.shape[0]
    pq = jnp.dot(lc_ref[...], f_ref[...].astype(BF16), preferred_element_type=F32)
    fy_ref[...] = _channel_dft(pq, n, mix_ref)


def _dft_2d_small(f, mix_bf):
    bsz, n, _ = f.shape
    k = np.arange(n)
    ang = 2.0 * np.pi * np.outer(k, k) / n
    lc = np.concatenate([np.cos(ang), np.sin(ang)], axis=0)
    spec = pl.BlockSpec((None, n, FOUR_CH), lambda b: (b, 0, 0))
    return pl.pallas_call(
        _dft_small_body,
        grid=(bsz,),
        in_specs=[spec, _const_spec((2 * n, n)), _const_spec((2 * FOUR_CH, FOUR_CH))],
        out_specs=spec,
        out_shape=jax.ShapeDtypeStruct((bsz, n, FOUR_CH), BF16),
        compiler_params=_cparams(1),
        name="dft_small",
    )(f, _bf16_const(lc), mix_bf)


def _channel_mix_const(seq_len):
    k = np.arange(FOUR_GROUP)
    ang = 2.0 * np.pi * np.outer(k, k) / FOUR_GROUP
    ng = FOUR_CH // FOUR_GROUP
    bdc = np.kron(np.eye(ng), np.cos(ang))
    bds = np.kron(np.eye(ng), np.sin(ang))
    return np.concatenate([bdc, -bds], axis=0) / np.sqrt(seq_len * FOUR_GROUP)


def _mixout_mlp_body(mod_row, x_ref, ay_ref, fy_ref, mod_ref, g2_ref, wout_ref, w1_ref, w2_ref, o_ref):
    b = pl.program_id(0)
    row = b if mod_row is None else mod_row
    gate1, shift2, scale2, gate2 = _mod_rows(mod_ref, row, (2, 3, 4, 5))
    cat = jnp.concatenate([ay_ref[...], fy_ref[...]], axis=1)
    mo = jnp.dot(cat, wout_ref[...], preferred_element_type=F32)
    x1 = x_ref[...] + gate1 * mo
    o_ref[...] = x1 + gate2 * _mlp(x1, g2_ref[...], shift2, scale2, w1_ref, w2_ref)


def _mixout_mlp(x, ay, fy, mod, g2, wout_bf, w1_bf, w2_bf, layer, mod_row, tm):
    bsz, seq_len, _ = x.shape
    row_spec = lambda w: pl.BlockSpec((None, tm, w), lambda b, t: (b, t, 0))
    body = functools.partial(_mixout_mlp_body, mod_row)
    return pl.pallas_call(
        body,
        grid=(bsz, seq_len // tm),
        in_specs=[row_spec(D), row_spec(CONV_CH), row_spec(FOUR_CH),
                  _const_spec((8, N_MOD * D)), _const_spec((1, D)),
                  _const_spec((D, D)), _layer_spec((D, D_FF), layer), _layer_spec((D_FF, D), layer)],
        out_specs=row_spec(D),
        out_shape=jax.ShapeDtypeStruct((bsz, seq_len, D), F32),
        compiler_params=_cparams(2),
        name="mixer_out_mlp",
    )(x, ay, fy, mod, g2, wout_bf, w1_bf, w2_bf)


def _head_rms(t, bd_ref, gt):
    tt = (t * t).astype(BF16)
    w = bd_ref.shape[0]
    ms = jnp.concatenate(
        [jnp.dot(tt[:, j * w:(j + 1) * w], bd_ref[...], preferred_element_type=F32)
         for j in range(D // w)], axis=1)
    return t * lax.rsqrt(ms + EPS) * gt


def _qkv_common(mod_row, x_ref, mod_ref, g_ref, w_ref, bd_ref, qg_ref, kg_ref):
    b = pl.program_id(0)
    row = b if mod_row is None else mod_row
    shift, scale = _mod_rows(mod_ref, row, (0, 1))
    h = _rms_mod(x_ref[...], g_ref[...], shift, scale).astype(BF16)
    qkv = jnp.dot(h, w_ref[...], preferred_element_type=F32)
    q = _head_rms(qkv[:, 0:D], bd_ref, qg_ref[...]) * (HEAD_DIM ** -0.5 * LOG2E)
    k = _head_rms(qkv[:, D:2 * D], bd_ref, kg_ref[...])
    v = qkv[:, 2 * D:]
    return q, k, v


def _qkv_grid_body(x_ref, mod_ref, g_ref, w_ref, bd_ref, qg_ref, kg_ref, q_ref, k_ref, v_ref):
    q, k, v = _qkv_common(None, x_ref, mod_ref, g_ref, w_ref, bd_ref, qg_ref, kg_ref)
    for rho in range(TM // GRID_W):
        for g in range(NGRP):
            q0 = rho * GRID_W + QCOLS * g
            q_ref[g, rho * QCOLS:(rho + 1) * QCOLS, :] = q[q0:q0 + QCOLS].astype(BF16)
            k0 = rho * GRID_W + KC0[g]
            k_ref[g, rho * KCOLS:(rho + 1) * KCOLS, :] = k[k0:k0 + KCOLS].astype(BF16)
            v_ref[g, rho * KCOLS:(rho + 1) * KCOLS, :] = v[k0:k0 + KCOLS].astype(BF16)


def _qkv_ctx_body(mod_row, x_ref, mod_ref, g_ref, w_ref, bd_ref, qg_ref, kg_ref, k_ref, v_ref):
    _, k, v = _qkv_common(mod_row, x_ref, mod_ref, g_ref, w_ref, bd_ref, qg_ref, kg_ref)
    k_ref[...] = k.astype(BF16)
    v_ref[...] = v.astype(BF16)


def _qkv_consts(q_g, k_g):
    w = 256
    bd = np.kron(np.eye(w // HEAD_DIM), np.ones((HEAD_DIM, HEAD_DIM))) / HEAD_DIM
    qg = jnp.tile(q_g, N_HEADS).reshape(1, D)
    kg = jnp.tile(k_g, N_HEADS).reshape(1, D)
    return _bf16_const(bd), qg, kg


def _qkv_in_specs(tm):
    return [pl.BlockSpec((None, tm, D), lambda b, t: (b, t, 0)),
            _const_spec((8, N_MOD * D)), _const_spec((1, D)), _const_spec((D, 3 * D)),
            _const_spec((256, 256)), _const_spec((1, D)), _const_spec((1, D))]


def _qkv_grid(x, mod, g, wqkv_bf, q_g, k_g):
    bsz = x.shape[0]
    bd, qg, kg = _qkv_consts(q_g, k_g)
    nq = TM // GRID_W * QCOLS
    nk = TM // GRID_W * KCOLS
    return pl.pallas_call(
        _qkv_grid_body,
        grid=(bsz, SEQ // TM),
        in_specs=_qkv_in_specs(TM),
        out_specs=[pl.BlockSpec((None, NGRP, nq, D), lambda b, t: (b, 0, t, 0)),
                   pl.BlockSpec((None, NGRP, nk, D), lambda b, t: (b, 0, t, 0)),
                   pl.BlockSpec((None, NGRP, nk, D), lambda b, t: (b, 0, t, 0))],
        out_shape=[jax.ShapeDtypeStruct((bsz, NGRP, GRID_ROWS * QCOLS, D), BF16),
                   jax.ShapeDtypeStruct((bsz, NGRP, GRID_ROWS * KCOLS, D), BF16),
                   jax.ShapeDtypeStruct((bsz, NGRP, GRID_ROWS * KCOLS, D), BF16)],
        compiler_params=_cparams(2),
        name="qkv_grid",
    )(x, mod, g, wqkv_bf, bd, qg, kg)


def _qkv_ctx(ctx, mod, g, wqkv_bf, q_g, k_g, mod_row):
    bsz, n, _ = ctx.shape
    bd, qg, kg = _qkv_consts(q_g, k_g)
    spec = pl.BlockSpec((None, n, D), lambda b, t: (b, t, 0))
    return pl.pallas_call(
        functools.partial(_qkv_ctx_body, mod_row),
        grid=(bsz, 1),
        in_specs=_qkv_in_specs(n),
        out_specs=[spec, spec],
        out_shape=[jax.ShapeDtypeStruct((bsz, n, D), BF16)] * 2,
        compiler_params=_cparams(2),
        name="qkv_ctx",
    )(ctx, mod, g, wqkv_bf, bd, qg, kg)


def _bias_tables(rpb):
    n_ro = 2 * WIN_ROWS - 1
    n_co = 2 * WIN_COLS - 1
    g = np.arange(NGRP)[:, None, None]
    cq = np.arange(QCOLS)[None, :, None]
    kcw = (np.arange(128) % KCOLS)[None, None, :]
    c = QCOLS * g + cq
    kc = np.asarray(KC0)[:, None, None] + kcw
    cs = np.clip(c - WIN_COLS // 2, 0, GRID_W - WIN_COLS)
    col_ok = (kc >= cs) & (kc < cs + WIN_COLS)
    co = np.where(col_ok, kc - c + (WIN_COLS - 1), -1)
    onehot = (np.arange(n_co)[:, None, None, None] == co[None]).astype(np.float32)
    sel = jnp.einsum("hrc,cn->hrn", rpb, jnp.asarray(onehot.reshape(n_co, -1)),
                     precision=lax.Precision.HIGHEST)
    sel = sel.reshape(N_HEADS, n_ro, NGRP, QCOLS, 128).transpose(0, 2, 1, 3, 4) * LOG2E
    sel = jnp.where(col_ok[None, :, None], sel, NEG)
    rt = jnp.pad(sel, ((0, 0), (0, 0), (3, KROWS_PER_TILE - 1), (0, 0), (0, 0)))
    rt = rt.reshape(N_HEADS // 2, 2, NGRP, n_ro + 2 + KROWS_PER_TILE, QCOLS, 128)
    delta = np.arange(13)[:, None, None] - 8
    jj2 = (np.arange(128) // KCOLS)[None, None, :]
    rm = np.where((jj2 >= delta) & (jj2 < delta + WIN_ROWS), 0.0, NEG)
    rm = np.broadcast_to(rm, (13, QCOLS, 128)).astype(np.float32)
    return rt.astype(F32), jnp.asarray(rm)


def _tile_indices(rb, i, jq):
    wr = int(np.clip(RB * rb - WIN_ROWS // 2, 0, GRID_ROWS - WROWS))
    r = RB * rb + i
    delta = int(np.clip(r - WIN_ROWS // 2, 0, GRID_ROWS - WIN_ROWS)) - wr - KROWS_PER_TILE * jq
    if not -WIN_ROWS < delta < KROWS_PER_TILE:
        return None
    bti = wr + KROWS_PER_TILE * jq - r + (WIN_ROWS - 1) + 3
    assert 0 <= bti < N_BT and 0 <= delta + 8 < 13
    return bti, delta + 8


def _attn_body(q_ref, k_ref, v_ref, kc_ref, vc_ref, rt_ref, rm_ref, o_ref, s_scr, bt_scr, bm_scr):
    kc = kc_ref[...]
    vc = vc_ref[...]
    nq = RB * QCOLS
    nk = WROWS * KCOLS
    lane = lax.broadcasted_iota(jnp.int32, (nq, 128), 1)
    nt = nk // 128
    dn = (((1,), (1,)), ((), ()))

    lgrp = lax.broadcasted_iota(jnp.int32, (QCOLS, 128), 1) // KCOLS
    for e in range(2):
        for t in range(N_BT):
            blk = rt_ref[e, t + KROWS_PER_TILE - 1]
            for jj in range(KROWS_PER_TILE - 2, -1, -1):
                blk = jnp.where(lgrp == jj, rt_ref[e, t + jj], blk)
            bt_scr[e, t] = blk
    for e in range(2):
        for i in range(RB):
            for jq in range(nt):
                idx = _tile_indices(1, i, jq)
                if idx is not None:
                    bm_scr[e, i * nt + jq] = bt_scr[e, idx[0]] + rm_ref[idx[1]]

    def window_row(rb):
        return jnp.clip(RB * rb - WIN_ROWS // 2, 0, GRID_ROWS - WROWS)

    def scores(rb, slot):
        rb = jnp.asarray(rb, jnp.int32)
        koff = pl.multiple_of(window_row(rb) * KCOLS, 128)
        kw = k_ref[pl.ds(koff, nk), :]
        q = q_ref[pl.ds(pl.multiple_of(rb * nq, nq), nq), :]
        for e in range(2):
            in_head = (lane >= HEAD_DIM * e) & (lane < HEAD_DIM * (e + 1))
            qm = jnp.where(in_head, q, jnp.zeros_like(q))
            s_scr[slot, e, :, 0:nk] = lax.dot_general(qm, kw, dn, preferred_element_type=F32)
            s_scr[slot, e, :, nk:] = lax.dot_general(qm, kc, dn, preferred_element_type=F32)

    def finish(rb, slot, edge_rb):
        rb = jnp.asarray(rb, jnp.int32)
        wr = window_row(rb)
        vw = v_ref[pl.ds(pl.multiple_of(wr * KCOLS, 128), nk), :]
        zero_tile = jnp.zeros((QCOLS, 128), F32)
        outs = []
        for e in range(2):
            p_rows = []
            l_rows = []
            for i in range(RB):
                rows = slice(i * QCOLS, (i + 1) * QCOLS)
                blks = {}
                for jq in range(nt):
                    idx = _tile_indices(1 if edge_rb is None else edge_rb, i, jq)
                    if idx is None:
                        continue
                    add = bm_scr[e, i * nt + jq] if edge_rb is None else bt_scr[e, idx[0]] + rm_ref[idx[1]]
                    blks[jq] = s_scr[slot, e, rows, jq * 128:(jq + 1) * 128] + add
                for jc in range(CTX // 128):
                    blks[nt + jc] = s_scr[slot, e, rows, nk + jc * 128:nk + (jc + 1) * 128]
                vals = list(blks.values())
                m = jnp.max(functools.reduce(jnp.maximum, vals), axis=-1, keepdims=True)
                ps = {j: jnp.exp2(sb - m) for j, sb in blks.items()}
                l_rows.append(jnp.sum(functools.reduce(jnp.add, list(ps.values())), axis=-1, keepdims=True))
                p_rows.append(jnp.concatenate([ps.get(j, zero_tile) for j in range(nt + CTX // 128)], axis=1))
            p = jnp.concatenate(p_rows, axis=0).astype(BF16)
            lsum = jnp.concatenate(l_rows, axis=0)
            o = (jnp.dot(p[:, :nk], vw, preferred_element_type=F32)
                 + jnp.dot(p[:, nk:], vc, preferred_element_type=F32))
            outs.append(o / lsum)
        o = jnp.where(lane < HEAD_DIM, outs[0], outs[1])
        o_ref[pl.ds(pl.multiple_of(rb * nq, nq), nq), :] = o.astype(BF16)

    scores(0, 0)
    scores(1, 1)
    finish(0, 0, 0)
    scores(2, 0)
    finish(1, 1, None)

    def pair(tt, carry):
        t0 = 2 * tt
        scores(t0 + 1, 1)
        finish(t0, 0, None)
        scores(t0 + 2, 0)
        finish(t0 + 1, 1, None)
        return carry

    lax.fori_loop(1, NRB // 2 - 1, pair, 0, unroll=True)
    scores(NRB - 1, 1)
    finish(NRB - 2, 0, None)
    finish(NRB - 1, 1, NRB - 1)


def _attention(qcb, kcb, vcb, kc, vc, bt, rm):
    bsz = qcb.shape[0]
    nhp = N_HEADS // 2
    n_t = bt.shape[3]
    return pl.pallas_call(
        _attn_body,
        grid=(bsz, nhp, NGRP),
        in_specs=[pl.BlockSpec((None, None, GRID_ROWS * QCOLS, 128), lambda b, h, g: (b, g, 0, h)),
                  pl.BlockSpec((None, None, GRID_ROWS * KCOLS, 128), lambda b, h, g: (b, g, 0, h)),
                  pl.BlockSpec((None, None, GRID_ROWS * KCOLS, 128), lambda b, h, g: (b, g, 0, h)),
                  pl.BlockSpec((None, CTX, 128), lambda b, h, g: (b, 0, h)),
                  pl.BlockSpec((None, CTX, 128), lambda b, h, g: (b, 0, h)),
                  pl.BlockSpec((None, 2, None, n_t, QCOLS, 128), lambda b, h, g: (h, 0, g, 0, 0, 0)),
                  _const_spec((13, QCOLS, 128))],
        out_specs=pl.BlockSpec((None, None, GRID_ROWS * QCOLS, 128), lambda b, h, g: (b, g, 0, h)),
        out_shape=jax.ShapeDtypeStruct((bsz, NGRP, GRID_ROWS * QCOLS, D), BF16),
        scratch_shapes=[pltpu.VMEM((2, 2, RB * QCOLS, WROWS * KCOLS + CTX), F32),
                        pltpu.VMEM((2, N_BT, QCOLS, 128), F32),
                        pltpu.VMEM((2, RB * WROWS * KCOLS // 128, QCOLS, 128), F32)],
        compiler_params=_cparams(3),
        name="nbr_attention",
    )(qcb, kcb, vcb, kc, vc, bt, rm)


def _attnout_mlp_body(x_ref, o_ref_in, mod_ref, g2_ref, wo_ref, w1_ref, w2_ref, out_ref):
    b = pl.program_id(0)
    gate1, shift2, scale2, gate2 = _mod_rows(mod_ref, b, (2, 3, 4, 5))
    chunks = [o_ref_in[g, rho * QCOLS:(rho + 1) * QCOLS, :]
              for rho in range(TM // GRID_W) for g in range(NGRP)]
    o_nat = jnp.concatenate(chunks, axis=0)
    mo = jnp.dot(o_nat, wo_ref[...], preferred_element_type=F32)
    x1 = x_ref[...] + gate1 * mo
    out_ref[...] = x1 + gate2 * _mlp(x1, g2_ref[...], shift2, scale2, w1_ref, w2_ref)


def _attnout_mlp(x, ocb, mod, g2, wo_bf, w1_bf, w2_bf, layer):
    bsz = x.shape[0]
    nq = TM // GRID_W * QCOLS
    return pl.pallas_call(
        _attnout_mlp_body,
        grid=(bsz, SEQ // TM),
        in_specs=[pl.BlockSpec((None, TM, D), lambda b, t: (b, t, 0)),
                  pl.BlockSpec((None, NGRP, nq, D), lambda b, t: (b, 0, t, 0)),
                  _const_spec((8, N_MOD * D)), _const_spec((1, D)),
                  _const_spec((D, D)), _layer_spec((D, D_FF), layer), _layer_spec((D_FF, D), layer)],
        out_specs=pl.BlockSpec((None, TM, D), lambda b, t: (b, t, 0)),
        out_shape=jax.ShapeDtypeStruct((bsz, SEQ, D), F32),
        compiler_params=_cparams(2),
        name="attn_out_mlp",
    )(x, ocb, mod, g2, wo_bf, w1_bf, w2_bf)


def kernel(x, c, ctx, c_ctx, ada_w, ada_b, norm_mix_g, norm_mlp_g, mlp_w1, mlp_w2, ab_w_in, ab_conv_w,
           ab_w_out, na_w_qkv, na_q_g, na_k_g, na_rpb, na_w_out):
    bsz = x.shape[0]
    ctx_row = bsz
    cond8 = jnp.zeros((8, D), F32).at[:bsz].set(c).at[ctx_row].set(c_ctx)
    mods = _modulation(cond8, ada_w, ada_b)

    bf = lambda w: w.astype(BF16)
    g_mix = norm_mix_g.reshape(DEPTH, 1, D)
    g_mlp = norm_mlp_g.reshape(DEPTH, 1, D)
    conv_w8 = jnp.zeros((8, CONV_CH), F32).at[:3].set(ab_conv_w[0])

    w_in, w_out0 = bf(ab_w_in[0]), bf(ab_w_out[0])
    w1_all, w2_all = bf(mlp_w1), bf(mlp_w2)
    ay, f = _inproj(x, mods[0], g_mix[0], w_in, conv_w8, None, TM_IN)
    fy = _dft_2d(f, _bf16_const(_channel_mix_const(SEQ)))
    x = _mixout_mlp(x, ay, fy, mods[0], g_mlp[0], w_out0, w1_all, w2_all, 0, None, TM)

    ay_c, f_c = _inproj(ctx, mods[0], g_mix[0], w_in, conv_w8, ctx_row, CTX)
    fy_c = _dft_2d_small(f_c, _bf16_const(_channel_mix_const(CTX)))
    flat = lambda a: a.reshape(1, bsz * CTX, a.shape[-1])
    ctx = _mixout_mlp(flat(ctx), flat(ay_c), flat(fy_c), mods[0], g_mlp[0], w_out0, w1_all, w2_all, 0,
                      ctx_row, bsz * CTX)

    wqkv, wo = bf(na_w_qkv[0]), bf(na_w_out[0])
    qcb, kcb, vcb = _qkv_grid(x, mods[1], g_mix[1], wqkv, na_q_g[0], na_k_g[0])
    kc, vc = _qkv_ctx(ctx, mods[1], g_mix[1], wqkv, na_q_g[0], na_k_g[0], ctx_row)
    kc, vc = kc.reshape(bsz, CTX, D), vc.reshape(bsz, CTX, D)
    bt, rm = _bias_tables(na_rpb[0])
    ocb = _attention(qcb, kcb, vcb, kc, vc, bt, rm)
    x = _attnout_mlp(x, ocb, mods[1], g_mlp[1], wo, w1_all, w2_all, 1)
    return x
```

```python
import functools

import numpy as np
import jax
import jax.numpy as jnp
from jax import lax
from jax.experimental import pallas as pl
from jax.experimental.pallas import tpu as pltpu

F32 = jnp.float32
BF16 = jnp.bfloat16

D = 1024
DEPTH = 2
SEQ = 8192
CTX = 256
GRID_W = 64
GRID_ROWS = SEQ // GRID_W
HEAD_DIM = 64
N_HEADS = 16
CONV_CH = 512
FOUR_CH = 512
FOUR_GROUP = 64
WIN_ROWS = 8
WIN_COLS = 16
D_FF = 4 * D
N_MOD = 6
EPS = 1e-6

TM = 512
TM_IN = 1024
HALO = 8
FF_CHUNK = 1024
VMEM_LIMIT = 56 * 1024 * 1024

DFT_A = 32
DFT_B = SEQ // DFT_A
SUB = 8

QCOLS = 16
NGRP = GRID_W // QCOLS
KCOLS = 32
KC0 = tuple(int(np.clip(QCOLS * g - 8, 0, GRID_W - KCOLS)) for g in range(NGRP))
RB = 8
NRB = GRID_ROWS // RB
WROWS = 16
KROWS_PER_TILE = 128 // KCOLS
NEG = -1e30
LOG2E = float(np.log2(np.e))
N_BT = 2 * WIN_ROWS - 1 + KROWS_PER_TILE - 1


def _cparams(n_axes):
    return pltpu.CompilerParams(dimension_semantics=("arbitrary",) * n_axes,
                                vmem_limit_bytes=VMEM_LIMIT)


def _const_spec(shape):
    nd = len(shape)
    return pl.BlockSpec(shape, lambda *_: (0,) * nd)


def _cast_jobs(jobs, n_steps, step_of):
    in_specs, out_specs, out_shapes = [], [], []
    for arr, layer in jobs:
        _, rows, cols = arr.shape
        chunk = rows // n_steps
        assert chunk * n_steps == rows and chunk % 16 == 0
        in_specs.append(pl.BlockSpec((None, chunk, cols), lambda *ids, layer=layer: (layer, step_of(*ids), 0)))
        out_specs.append(pl.BlockSpec((chunk, cols), lambda *ids: (step_of(*ids), 0)))
        out_shapes.append(jax.ShapeDtypeStruct((rows, cols), BF16))
    return in_specs, out_specs, out_shapes


def _with_casts(body, n_in, n_out, n_jobs):
    def wrapped(*refs):
        ins, refs = refs[:n_in], refs[n_in:]
        cast_ins, refs = refs[:n_jobs], refs[n_jobs:]
        outs, refs = refs[:n_out], refs[n_out:]
        cast_outs, scratch = refs[:n_jobs], refs[n_jobs:]
        for src, dst in zip(cast_ins, cast_outs):
            dst[...] = src[...].astype(BF16)
        body(*ins, *outs, *scratch)
    return wrapped


def _bf16_const(a):
    return jnp.asarray(a, F32).astype(BF16)


def _rms_mod(x, g, shift, scale):
    ms = jnp.mean(x * x, axis=-1, keepdims=True)
    y = x * lax.rsqrt(ms + EPS)
    return (y * g) * (1.0 + scale) + shift


def _mod_rows(mod_ref, row, ks):
    return [mod_ref[pl.ds(row, 1), k * D:(k + 1) * D] for k in ks]


def _mlp(x1, g, shift, scale, w1_ref, w2_ref):
    h = _rms_mod(x1, g, shift, scale).astype(BF16)
    acc = jnp.zeros(x1.shape, F32)
    for c in range(D_FF // FF_CHUNK):
        a = jnp.dot(h, w1_ref[:, c * FF_CHUNK:(c + 1) * FF_CHUNK], preferred_element_type=F32)
        a = jnp.maximum(a, 0.0)
        a = (a * a).astype(BF16)
        acc = acc + jnp.dot(a, w2_ref[c * FF_CHUNK:(c + 1) * FF_CHUNK, :], preferred_element_type=F32)
    return acc


def _mod_body(cond_ref, w_ref, b_ref, o_ref):
    c = cond_ref[...]
    s = c * jax.nn.sigmoid(c)
    o_ref[...] = jnp.dot(s.astype(BF16), w_ref[...].astype(BF16), preferred_element_type=F32) + b_ref[...]


def _modulation(cond8, ada_w, ada_b):
    nt = N_MOD * D // D
    return pl.pallas_call(
        _mod_body,
        grid=(DEPTH, nt),
        in_specs=[_const_spec((8, D)),
                  pl.BlockSpec((None, D, D), lambda l, n: (l, 0, n)),
                  pl.BlockSpec((None, 1, D), lambda l, n: (l, 0, n))],
        out_specs=pl.BlockSpec((None, 8, D), lambda l, n: (l, 0, n)),
        out_shape=jax.ShapeDtypeStruct((DEPTH, 8, N_MOD * D), F32),
        compiler_params=_cparams(2),
        name="adaln_mod",
    )(cond8, ada_w, ada_b.reshape(DEPTH, 1, N_MOD * D))


def _inproj_body(mod_row, seq_len, tm, xp_ref, x_ref, xn_ref, mod_ref, g_ref, w_ref, cw_ref,
                 ay_ref, f_ref):
    b = pl.program_id(0)
    t = pl.program_id(1)
    row = b if mod_row is None else mod_row
    xx = jnp.concatenate([xp_ref[...], x_ref[...], xn_ref[...]], axis=0)
    shift, scale = _mod_rows(mod_ref, row, (0, 1))
    h = _rms_mod(xx, g_ref[...], shift, scale).astype(BF16)
    u = jnp.dot(h, w_ref[...], preferred_element_type=F32)
    z = u[:, CONV_CH:2 * CONV_CH] * u[:, 0:CONV_CH]
    n = t * tm - HALO + lax.broadcasted_iota(jnp.int32, (tm + 2 * HALO, 1), 0)
    z = jnp.where((n >= 0) & (n < seq_len), z, 0.0)
    cw = cw_ref[...]
    zc = (z[HALO - 1:HALO - 1 + tm] * cw[0:1] + z[HALO:HALO + tm] * cw[1:2]
          + z[HALO + 1:HALO + 1 + tm] * cw[2:3])
    ay_ref[...] = (u[HALO:HALO + tm, 2 * CONV_CH:3 * CONV_CH] * zc).astype(BF16)
    f_ref[...] = u[HALO:HALO + tm, 3 * CONV_CH:]


def _inproj(x, mod, g, w_in_bf, conv_w8, mod_row, tm, casts=()):
    bsz, seq_len, _ = x.shape
    nt = seq_len // tm
    nb8 = seq_len // HALO
    r8 = tm // HALO
    c_in, c_out, c_shape = _cast_jobs(casts, bsz * nt, lambda b, t: b * nt + t)
    body = _with_casts(functools.partial(_inproj_body, mod_row, seq_len, tm), 7, 2, len(casts))
    return pl.pallas_call(
        body,
        grid=(bsz, nt),
        in_specs=[pl.BlockSpec((None, HALO, D), lambda b, t: (b, jnp.maximum(t * r8 - 1, 0), 0)),
                  pl.BlockSpec((None, tm, D), lambda b, t: (b, t, 0)),
                  pl.BlockSpec((None, HALO, D), lambda b, t: (b, jnp.minimum((t + 1) * r8, nb8 - 1), 0)),
                  _const_spec((8, N_MOD * D)),
                  _const_spec((1, D)),
                  _const_spec((D, 3 * CONV_CH + FOUR_CH)),
                  _const_spec((8, CONV_CH))] + c_in,
        out_specs=[pl.BlockSpec((None, tm, CONV_CH), lambda b, t: (b, t, 0)),
                   pl.BlockSpec((None, tm, FOUR_CH), lambda b, t: (b, t, 0))] + c_out,
        out_shape=[jax.ShapeDtypeStruct((bsz, seq_len, CONV_CH), BF16),
                   jax.ShapeDtypeStruct((bsz, seq_len, FOUR_CH), F32)] + c_shape,
        compiler_params=_cparams(2),
        name="mixer_in",
    )(x, x, x, mod, g, w_in_bf, conv_w8, *[a for a, _ in casts])


def _dft_consts():
    a = np.arange(DFT_A)
    s = np.arange(SUB)
    ang = 2.0 * np.pi * np.outer(a, a) / DFT_A
    eye = np.eye(SUB)
    re = np.einsum("va,ts->vtas", np.cos(ang), eye).reshape(DFT_A * SUB, DFT_A * SUB)
    im = np.einsum("va,ts->vtas", -np.sin(ang), eye).reshape(DFT_A * SUB, DFT_A * SUB)
    l1 = np.concatenate([re, im], axis=0)
    m = np.arange(DFT_B // SUB)
    bb = (SUB * m[:, None, None] + s[None, None, :])
    tang = 2.0 * np.pi * a[None, :, None] * bb / SEQ
    twr = np.cos(tang).reshape(DFT_B // SUB, DFT_A * SUB, 1)
    twi = (-np.sin(tang)).reshape(DFT_B // SUB, DFT_A * SUB, 1)
    u = np.arange(DFT_B)
    ang2 = 2.0 * np.pi * np.outer(u, u) / DFT_B
    c2, s2 = np.cos(ang2), np.sin(ang2)
    l2 = np.block([[c2, s2], [s2, -c2]])
    return l1, twr, twi, l2


DFT_MPAIR = 4
DFT_VB = 4


def _dft_a_body(x_ref, l1_ref, twr_ref, twi_ref, br_ref, bi_ref):
    rows = DFT_A * SUB
    reps = FOUR_CH // 128
    x = x_ref[...]
    brs, bis = [], []
    for j in range(DFT_MPAIR):
        xj = x[:, j * SUB:(j + 1) * SUB, :].reshape(rows, FOUR_CH).astype(BF16)
        a = jnp.dot(l1_ref[...], xj, preferred_element_type=F32)
        ar, ai = a[:rows], a[rows:]
        twr = jnp.concatenate([twr_ref[j]] * reps, axis=1)
        twi = jnp.concatenate([twi_ref[j]] * reps, axis=1)
        brs.append((ar * twr - ai * twi).reshape(DFT_A, SUB, FOUR_CH))
        bis.append((ar * twi + ai * twr).reshape(DFT_A, SUB, FOUR_CH))
    br_ref[...] = jnp.concatenate(brs, axis=1).astype(BF16)
    bi_ref[...] = jnp.concatenate(bis, axis=1).astype(BF16)


def _channel_dft(pq, n, mix_ref):
    w = 256
    outs = []
    for cb in range(FOUR_CH // w):
        cols = slice(cb * w, (cb + 1) * w)
        lhs = jnp.concatenate([pq[:n, cols], pq[n:, cols]], axis=1).astype(BF16)
        rhs = jnp.concatenate([mix_ref[cols, cols],
                               mix_ref[FOUR_CH + cb * w:FOUR_CH + (cb + 1) * w, cols]], axis=0)
        outs.append(jnp.dot(lhs, rhs, preferred_element_type=F32))
    return jnp.concatenate(outs, axis=1).astype(BF16)


def _dft_b_body(br_ref, bi_ref, l2_ref, mix_ref, fy_ref):
    for j in range(DFT_VB):
        bm = jnp.concatenate([br_ref[j], bi_ref[j]], axis=0)
        pq = jnp.dot(l2_ref[...], bm, preferred_element_type=F32)
        fy_ref[j] = _channel_dft(pq, DFT_B, mix_ref)


def _dft_2d(f, mix_bf):
    bsz = f.shape[0]
    l1, twr, twi, l2 = _dft_consts()
    nm = DFT_B // SUB
    nm2 = nm // DFT_MPAIR
    rows = DFT_A * SUB
    twr_b = jnp.broadcast_to(jnp.asarray(twr, F32), (nm, rows, 128))
    twi_b = jnp.broadcast_to(jnp.asarray(twi, F32), (nm, rows, 128))
    f5 = f.reshape(bsz, DFT_A, nm2, DFT_MPAIR * SUB, FOUR_CH)
    blk5 = pl.BlockSpec((None, DFT_A, None, DFT_MPAIR * SUB, FOUR_CH), lambda m, b: (b, 0, m, 0, 0))
    tw_spec = pl.BlockSpec((DFT_MPAIR, rows, 128), lambda m, b: (m, 0, 0))
    br, bi = pl.pallas_call(
        _dft_a_body,
        grid=(nm2, bsz),
        in_specs=[blk5, _const_spec((2 * rows, rows)), tw_spec, tw_spec],
        out_specs=[blk5, blk5],
        out_shape=[jax.ShapeDtypeStruct((bsz, DFT_A, nm2, DFT_MPAIR * SUB, FOUR_CH), BF16)] * 2,
        compiler_params=_cparams(2),
        name="dft_stage_a",
    )(f5, _bf16_const(l1), twr_b, twi_b)
    br = br.reshape(bsz, DFT_A, DFT_B, FOUR_CH)
    bi = bi.reshape(bsz, DFT_A, DFT_B, FOUR_CH)
    blk4 = pl.BlockSpec((None, DFT_VB, DFT_B, FOUR_CH), lambda b, v: (b, v, 0, 0))
    fy = pl.pallas_call(
        _dft_b_body,
        grid=(bsz, DFT_A // DFT_VB),
        in_specs=[blk4, blk4, _const_spec((2 * DFT_B, 2 * DFT_B)), _const_spec((2 * FOUR_CH, FOUR_CH))],
        out_specs=blk4,
        out_shape=jax.ShapeDtypeStruct((bsz, DFT_A, DFT_B, FOUR_CH), BF16),
        compiler_params=_cparams(2),
        name="dft_stage_b",
    )(br, bi, _bf16_const(l2), mix_bf)
    return jnp.transpose(fy, (0, 2, 1, 3)).reshape(bsz, SEQ, FOUR_CH)


def _dft_small_body(f_ref, lc_ref, mix_ref, fy_ref):
    n = f_ref.shape[0]
    pq = jnp.dot(lc_ref[...], f_ref[...].astype(BF16), preferred_element_type=F32)
    fy_ref[...] = _channel_dft(pq, n, mix_ref)


def _dft_2d_small(f, mix_bf):
    bsz, n, _ = f.shape
    k = np.arange(n)
    ang = 2.0 * np.pi * np.outer(k, k) / n
    lc = np.concatenate([np.cos(ang), np.sin(ang)], axis=0)
    spec = pl.BlockSpec((None, n, FOUR_CH), lambda b: (b, 0, 0))
    return pl.pallas_call(
        _dft_small_body,
        grid=(bsz,),
        in_specs=[spec, _const_spec((2 * n, n)), _const_spec((2 * FOUR_CH, FOUR_CH))],
        out_specs=spec,
        out_shape=jax.ShapeDtypeStruct((bsz, n, FOUR_CH), BF16),
        compiler_params=_cparams(1),
        name="dft_small",
    )(f, _bf16_const(lc), mix_bf)


def _channel_mix_const(seq_len):
    k = np.arange(FOUR_GROUP)
    ang = 2.0 * np.pi * np.outer(k, k) / FOUR_GROUP
    ng = FOUR_CH // FOUR_GROUP
    bdc = np.kron(np.eye(ng), np.cos(ang))
    bds = np.kron(np.eye(ng), np.sin(ang))
    return np.concatenate([bdc, -bds], axis=0) / np.sqrt(seq_len * FOUR_GROUP)


def _mixout_mlp_body(mod_row, x_ref, ay_ref, fy_ref, mod_ref, g2_ref, wout_ref, w1_ref, w2_ref, o_ref):
    b = pl.program_id(0)
    row = b if mod_row is None else mod_row
    gate1, shift2, scale2, gate2 = _mod_rows(mod_ref, row, (2, 3, 4, 5))
    cat = jnp.concatenate([ay_ref[...], fy_ref[...]], axis=1)
    mo = jnp.dot(cat, wout_ref[...], preferred_element_type=F32)
    x1 = x_ref[...] + gate1 * mo
    o_ref[...] = x1 + gate2 * _mlp(x1, g2_ref[...], shift2, scale2, w1_ref, w2_ref)


def _mixout_mlp(x, ay, fy, mod, g2, wout_bf, w1_bf, w2_bf, mod_row, tm, casts=()):
    bsz, seq_len, _ = x.shape
    nt = seq_len // tm
    row_spec = lambda w: pl.BlockSpec((None, tm, w), lambda b, t: (b, t, 0))
    c_in, c_out, c_shape = _cast_jobs(casts, bsz * nt, lambda b, t: b * nt + t)
    body = _with_casts(functools.partial(_mixout_mlp_body, mod_row), 8, 1, len(casts))
    return pl.pallas_call(
        body,
        grid=(bsz, nt),
        in_specs=[row_spec(D), row_spec(CONV_CH), row_spec(FOUR_CH),
                  _const_spec((8, N_MOD * D)), _const_spec((1, D)),
                  _const_spec((D, D)), _const_spec((D, D_FF)), _const_spec((D_FF, D))] + c_in,
        out_specs=[row_spec(D)] + c_out,
        out_shape=[jax.ShapeDtypeStruct((bsz, seq_len, D), F32)] + c_shape,
        compiler_params=_cparams(2),
        name="mixer_out_mlp",
    )(x, ay, fy, mod, g2, wout_bf, w1_bf, w2_bf, *[a for a, _ in casts])


def _head_rms(t, bd_ref, gt):
    tt = (t * t).astype(BF16)
    w = bd_ref.shape[0]
    ms = jnp.concatenate(
        [jnp.dot(tt[:, j * w:(j + 1) * w], bd_ref[...], preferred_element_type=F32)
         for j in range(D // w)], axis=1)
    return t * lax.rsqrt(ms + EPS) * gt


def _qkv_common(mod_row, x_ref, mod_ref, g_ref, w_ref, bd_ref, qg_ref, kg_ref):
    b = pl.program_id(0)
    row = b if mod_row is None else mod_row
    shift, scale = _mod_rows(mod_ref, row, (0, 1))
    h = _rms_mod(x_ref[...], g_ref[...], shift, scale).astype(BF16)
    qkv = jnp.dot(h, w_ref[...], preferred_element_type=F32)
    q = _head_rms(qkv[:, 0:D], bd_ref, qg_ref[...]) * (HEAD_DIM ** -0.5 * LOG2E)
    k = _head_rms(qkv[:, D:2 * D], bd_ref, kg_ref[...])
    v = qkv[:, 2 * D:]
    return q, k, v


def _qkv_grid_body(x_ref, mod_ref, g_ref, w_ref, bd_ref, qg_ref, kg_ref, q_ref, k_ref, v_ref):
    q, k, v = _qkv_common(None, x_ref, mod_ref, g_ref, w_ref, bd_ref, qg_ref, kg_ref)
    for rho in range(TM // GRID_W):
        for g in range(NGRP):
            q0 = rho * GRID_W + QCOLS * g
            q_ref[g, rho * QCOLS:(rho + 1) * QCOLS, :] = q[q0:q0 + QCOLS].astype(BF16)
            k0 = rho * GRID_W + KC0[g]
            k_ref[g, rho * KCOLS:(rho + 1) * KCOLS, :] = k[k0:k0 + KCOLS].astype(BF16)
            v_ref[g, rho * KCOLS:(rho + 1) * KCOLS, :] = v[k0:k0 + KCOLS].astype(BF16)


def _qkv_ctx_body(mod_row, x_ref, mod_ref, g_ref, w_ref, bd_ref, qg_ref, kg_ref, k_ref, v_ref):
    _, k, v = _qkv_common(mod_row, x_ref, mod_ref, g_ref, w_ref, bd_ref, qg_ref, kg_ref)
    k_ref[...] = k.astype(BF16)
    v_ref[...] = v.astype(BF16)


def _qkv_consts(q_g, k_g):
    w = 256
    bd = np.kron(np.eye(w // HEAD_DIM), np.ones((HEAD_DIM, HEAD_DIM))) / HEAD_DIM
    qg = jnp.tile(q_g, N_HEADS).reshape(1, D)
    kg = jnp.tile(k_g, N_HEADS).reshape(1, D)
    return _bf16_const(bd), qg, kg


def _qkv_in_specs(tm):
    return [pl.BlockSpec((None, tm, D), lambda b, t: (b, t, 0)),
            _const_spec((8, N_MOD * D)), _const_spec((1, D)), _const_spec((D, 3 * D)),
            _const_spec((256, 256)), _const_spec((1, D)), _const_spec((1, D))]


def _qkv_grid(x, mod, g, wqkv_bf, q_g, k_g):
    bsz = x.shape[0]
    bd, qg, kg = _qkv_consts(q_g, k_g)
    nq = TM // GRID_W * QCOLS
    nk = TM // GRID_W * KCOLS
    return pl.pallas_call(
        _qkv_grid_body,
        grid=(bsz, SEQ // TM),
        in_specs=_qkv_in_specs(TM),
        out_specs=[pl.BlockSpec((None, NGRP, nq, D), lambda b, t: (b, 0, t, 0)),
                   pl.BlockSpec((None, NGRP, nk, D), lambda b, t: (b, 0, t, 0)),
                   pl.BlockSpec((None, NGRP, nk, D), lambda b, t: (b, 0, t, 0))],
        out_shape=[jax.ShapeDtypeStruct((bsz, NGRP, GRID_ROWS * QCOLS, D), BF16),
                   jax.ShapeDtypeStruct((bsz, NGRP, GRID_ROWS * KCOLS, D), BF16),
                   jax.ShapeDtypeStruct((bsz, NGRP, GRID_ROWS * KCOLS, D), BF16)],
        compiler_params=_cparams(2),
        name="qkv_grid",
    )(x, mod, g, wqkv_bf, bd, qg, kg)


def _qkv_ctx(ctx, mod, g, wqkv_bf, q_g, k_g, mod_row):
    bsz, n, _ = ctx.shape
    bd, qg, kg = _qkv_consts(q_g, k_g)
    spec = pl.BlockSpec((None, n, D), lambda b, t: (b, t, 0))
    return pl.pallas_call(
        functools.partial(_qkv_ctx_body, mod_row),
        grid=(bsz, 1),
        in_specs=_qkv_in_specs(n),
        out_specs=[spec, spec],
        out_shape=[jax.ShapeDtypeStruct((bsz, n, D), BF16)] * 2,
        compiler_params=_cparams(2),
        name="qkv_ctx",
    )(ctx, mod, g, wqkv_bf, bd, qg, kg)


def _bias_tables(rpb):
    n_ro = 2 * WIN_ROWS - 1
    n_co = 2 * WIN_COLS - 1
    g = np.arange(NGRP)[:, None, None]
    cq = np.arange(QCOLS)[None, :, None]
    kcw = (np.arange(128) % KCOLS)[None, None, :]
    c = QCOLS * g + cq
    kc = np.asarray(KC0)[:, None, None] + kcw
    cs = np.clip(c - WIN_COLS // 2, 0, GRID_W - WIN_COLS)
    col_ok = (kc >= cs) & (kc < cs + WIN_COLS)
    co = np.where(col_ok, kc - c + (WIN_COLS - 1), -1)
    onehot = (np.arange(n_co)[:, None, None, None] == co[None]).astype(np.float32)
    sel = jnp.einsum("hrc,cn->hrn", rpb, jnp.asarray(onehot.reshape(n_co, -1)),
                     precision=lax.Precision.HIGHEST)
    sel = sel.reshape(N_HEADS, n_ro, NGRP, QCOLS, 128).transpose(0, 2, 1, 3, 4) * LOG2E
    sel = jnp.where(col_ok[None, :, None], sel, NEG)
    rt = jnp.pad(sel, ((0, 0), (0, 0), (3, KROWS_PER_TILE - 1), (0, 0), (0, 0)))
    rt = rt.reshape(N_HEADS // 2, 2, NGRP, n_ro + 2 + KROWS_PER_TILE, QCOLS, 128)
    delta = np.arange(13)[:, None, None] - 8
    jj2 = (np.arange(128) // KCOLS)[None, None, :]
    rm = np.where((jj2 >= delta) & (jj2 < delta + WIN_ROWS), 0.0, NEG)
    rm = np.broadcast_to(rm, (13, QCOLS, 128)).astype(np.float32)
    return rt.astype(F32), jnp.asarray(rm)


def _tile_indices(rb, i, jq):
    wr = int(np.clip(RB * rb - WIN_ROWS // 2, 0, GRID_ROWS - WROWS))
    r = RB * rb + i
    delta = int(np.clip(r - WIN_ROWS // 2, 0, GRID_ROWS - WIN_ROWS)) - wr - KROWS_PER_TILE * jq
    if not -WIN_ROWS < delta < KROWS_PER_TILE:
        return None
    bti = wr + KROWS_PER_TILE * jq - r + (WIN_ROWS - 1) + 3
    assert 0 <= bti < N_BT and 0 <= delta + 8 < 13
    return bti, delta + 8


def _attn_body(q_ref, k_ref, v_ref, kc_ref, vc_ref, rt_ref, rm_ref, o_ref, s_scr, bt_scr, bm_scr):
    kc = kc_ref[...]
    vc = vc_ref[...]
    nq = RB * QCOLS
    nk = WROWS * KCOLS
    lane = lax.broadcasted_iota(jnp.int32, (nq, 128), 1)
    nt = nk // 128
    dn = (((1,), (1,)), ((), ()))

    @pl.when(pl.program_id(2) == 0)
    def _build_tables():
        lgrp = lax.broadcasted_iota(jnp.int32, (QCOLS, 128), 1) // KCOLS
        for e in range(2):
            for t in range(N_BT):
                blk = rt_ref[e, t + KROWS_PER_TILE - 1]
                for jj in range(KROWS_PER_TILE - 2, -1, -1):
                    blk = jnp.where(lgrp == jj, rt_ref[e, t + jj], blk)
                bt_scr[e, t] = blk
        for e in range(2):
            for i in range(RB):
                for jq in range(nt):
                    idx = _tile_indices(1, i, jq)
                    if idx is not None:
                        bm_scr[e, i * nt + jq] = bt_scr[e, idx[0]] + rm_ref[idx[1]]

    def window_row(rb):
        return jnp.clip(RB * rb - WIN_ROWS // 2, 0, GRID_ROWS - WROWS)

    def scores(rb, slot):
        rb = jnp.asarray(rb, jnp.int32)
        koff = pl.multiple_of(window_row(rb) * KCOLS, 128)
        kw = k_ref[pl.ds(koff, nk), :]
        q = q_ref[pl.ds(pl.multiple_of(rb * nq, nq), nq), :]
        for e in range(2):
            in_head = (lane >= HEAD_DIM * e) & (lane < HEAD_DIM * (e + 1))
            qm = jnp.where(in_head, q, jnp.zeros_like(q))
            s_scr[slot, e, :, 0:nk] = lax.dot_general(qm, kw, dn, preferred_element_type=F32)
            s_scr[slot, e, :, nk:] = lax.dot_general(qm, kc, dn, preferred_element_type=F32)

    def finish(rb, slot, edge_rb):
        rb = jnp.asarray(rb, jnp.int32)
        wr = window_row(rb)
        vw = v_ref[pl.ds(pl.multiple_of(wr * KCOLS, 128), nk), :]
        zero_tile = jnp.zeros((QCOLS, 128), F32)
        outs = []
        for e in range(2):
            p_rows = []
            l_rows = []
            for i in range(RB):
                rows = slice(i * QCOLS, (i + 1) * QCOLS)
                blks = {}
                for jq in range(nt):
                    idx = _tile_indices(1 if edge_rb is None else edge_rb, i, jq)
                    if idx is None:
                        continue
                    add = bm_scr[e, i * nt + jq] if edge_rb is None else bt_scr[e, idx[0]] + rm_ref[idx[1]]
                    blks[jq] = s_scr[slot, e, rows, jq * 128:(jq + 1) * 128] + add
                for jc in range(CTX // 128):
                    blks[nt + jc] = s_scr[slot, e, rows, nk + jc * 128:nk + (jc + 1) * 128]
                vals = list(blks.values())
                m = jnp.max(functools.reduce(jnp.maximum, vals), axis=-1, keepdims=True)
                ps = {j: jnp.exp2(sb - m) for j, sb in blks.items()}
                l_rows.append(jnp.sum(functools.reduce(jnp.add, list(ps.values())), axis=-1, keepdims=True))
                p_rows.append(jnp.concatenate([ps.get(j, zero_tile) for j in range(nt + CTX // 128)], axis=1))
            p = jnp.concatenate(p_rows, axis=0).astype(BF16)
            lsum = jnp.concatenate(l_rows, axis=0)
            o = (jnp.dot(p[:, :nk], vw, preferred_element_type=F32)
                 + jnp.dot(p[:, nk:], vc, preferred_element_type=F32))
            outs.append(o / lsum)
        o = jnp.where(lane < HEAD_DIM, outs[0], outs[1])
        o_ref[pl.ds(pl.multiple_of(rb * nq, nq), nq), :] = o.astype(BF16)

    scores(0, 0)
    scores(1, 1)
    finish(0, 0, 0)
    scores(2, 0)
    finish(1, 1, None)

    def pair(tt, carry):
        t0 = 2 * tt
        scores(t0 + 1, 1)
        finish(t0, 0, None)
        scores(t0 + 2, 0)
        finish(t0 + 1, 1, None)
        return carry

    lax.fori_loop(1, NRB // 2 - 1, pair, 0, unroll=True)
    scores(NRB - 1, 1)
    finish(NRB - 2, 0, None)
    finish(NRB - 1, 1, NRB - 1)


def _attention(qcb, kcb, vcb, kc, vc, bt, rm):
    bsz = qcb.shape[0]
    nhp = N_HEADS // 2
    n_t = bt.shape[3]
    return pl.pallas_call(
        _attn_body,
        grid=(nhp, NGRP, bsz),
        in_specs=[pl.BlockSpec((None, None, GRID_ROWS * QCOLS, 128), lambda h, g, b: (b, g, 0, h)),
                  pl.BlockSpec((None, None, GRID_ROWS * KCOLS, 128), lambda h, g, b: (b, g, 0, h)),
                  pl.BlockSpec((None, None, GRID_ROWS * KCOLS, 128), lambda h, g, b: (b, g, 0, h)),
                  pl.BlockSpec((None, CTX, 128), lambda h, g, b: (b, 0, h)),
                  pl.BlockSpec((None, CTX, 128), lambda h, g, b: (b, 0, h)),
                  pl.BlockSpec((None, 2, None, n_t, QCOLS, 128), lambda h, g, b: (h, 0, g, 0, 0, 0)),
                  _const_spec((13, QCOLS, 128))],
        out_specs=pl.BlockSpec((None, None, GRID_ROWS * QCOLS, 128), lambda h, g, b: (b, g, 0, h)),
        out_shape=jax.ShapeDtypeStruct((bsz, NGRP, GRID_ROWS * QCOLS, D), BF16),
        scratch_shapes=[pltpu.VMEM((2, 2, RB * QCOLS, WROWS * KCOLS + CTX), F32),
                        pltpu.VMEM((2, N_BT, QCOLS, 128), F32),
                        pltpu.VMEM((2, RB * WROWS * KCOLS // 128, QCOLS, 128), F32)],
        compiler_params=_cparams(3),
        name="nbr_attention",
    )(qcb, kcb, vcb, kc, vc, bt, rm)


def _attnout_mlp_body(x_ref, o_ref_in, mod_ref, g2_ref, wo_ref, w1_ref, w2_ref, out_ref):
    b = pl.program_id(0)
    gate1, shift2, scale2, gate2 = _mod_rows(mod_ref, b, (2, 3, 4, 5))
    chunks = [o_ref_in[g, rho * QCOLS:(rho + 1) * QCOLS, :]
              for rho in range(TM // GRID_W) for g in range(NGRP)]
    o_nat = jnp.concatenate(chunks, axis=0)
    mo = jnp.dot(o_nat, wo_ref[...], preferred_element_type=F32)
    x1 = x_ref[...] + gate1 * mo
    out_ref[...] = x1 + gate2 * _mlp(x1, g2_ref[...], shift2, scale2, w1_ref, w2_ref)


def _attnout_mlp(x, ocb, mod, g2, wo_bf, w1_bf, w2_bf):
    bsz = x.shape[0]
    nq = TM // GRID_W * QCOLS
    return pl.pallas_call(
        _attnout_mlp_body,
        grid=(bsz, SEQ // TM),
        in_specs=[pl.BlockSpec((None, TM, D), lambda b, t: (b, t, 0)),
                  pl.BlockSpec((None, NGRP, nq, D), lambda b, t: (b, 0, t, 0)),
                  _const_spec((8, N_MOD * D)), _const_spec((1, D)),
                  _const_spec((D, D)), _const_spec((D, D_FF)), _const_spec((D_FF, D))],
        out_specs=pl.BlockSpec((None, TM, D), lambda b, t: (b, t, 0)),
        out_shape=jax.ShapeDtypeStruct((bsz, SEQ, D), F32),
        compiler_params=_cparams(2),
        name="attn_out_mlp",
    )(x, ocb, mod, g2, wo_bf, w1_bf, w2_bf)


def kernel(x, c, ctx, c_ctx, ada_w, ada_b, norm_mix_g, norm_mlp_g, mlp_w1, mlp_w2, ab_w_in, ab_conv_w,
           ab_w_out, na_w_qkv, na_q_g, na_k_g, na_rpb, na_w_out):
    bsz = x.shape[0]
    ctx_row = bsz
    cond8 = jnp.zeros((8, D), F32).at[:bsz].set(c).at[ctx_row].set(c_ctx)
    mods = _modulation(cond8, ada_w, ada_b)

    bf = lambda w: w.astype(BF16)
    g_mix = norm_mix_g.reshape(DEPTH, 1, D)
    g_mlp = norm_mlp_g.reshape(DEPTH, 1, D)
    conv_w8 = jnp.zeros((8, CONV_CH), F32).at[:3].set(ab_conv_w[0])

    w_in = bf(ab_w_in[0])
    ay, f, w1_0, w2_0, w_out0 = _inproj(x, mods[0], g_mix[0], w_in, conv_w8, None, TM_IN,
                                        casts=((mlp_w1, 0), (mlp_w2, 0), (ab_w_out, 0)))
    fy = _dft_2d(f, _bf16_const(_channel_mix_const(SEQ)))
    x, w1_1, w2_1, wqkv, wo = _mixout_mlp(x, ay, fy, mods[0], g_mlp[0], w_out0, w1_0, w2_0, None, TM,
                                          casts=((mlp_w1, 1), (mlp_w2, 1), (na_w_qkv, 0), (na_w_out, 0)))

    ay_c, f_c = _inproj(ctx, mods[0], g_mix[0], w_in, conv_w8, ctx_row, CTX)
    fy_c = _dft_2d_small(f_c, _bf16_const(_channel_mix_const(CTX)))
    flat = lambda a: a.reshape(1, bsz * CTX, a.shape[-1])
    ctx, = _mixout_mlp(flat(ctx), flat(ay_c), flat(fy_c), mods[0], g_mlp[0], w_out0, w1_0, w2_0,
                       ctx_row, bsz * CTX)

    qcb, kcb, vcb = _qkv_grid(x, mods[1], g_mix[1], wqkv, na_q_g[0], na_k_g[0])
    kc, vc = _qkv_ctx(ctx, mods[1], g_mix[1], wqkv, na_q_g[0], na_k_g[0], ctx_row)
    kc, vc = kc.reshape(bsz, CTX, D), vc.reshape(bsz, CTX, D)
    bt, rm = _bias_tables(na_rpb[0])
    ocb = _attention(qcb, kcb, vcb, kc, vc, bt, rm)
    x = _attnout_mlp(x, ocb, mods[1], g_mlp[1], wo, w1_1, w2_1)
    return x
```

```python
import functools

import numpy as np
import jax
import jax.numpy as jnp
from jax import lax
from jax.experimental import pallas as pl
from jax.experimental.pallas import tpu as pltpu

F32 = jnp.float32
BF16 = jnp.bfloat16

D = 1024
DEPTH = 2
SEQ = 8192
CTX = 256
GRID_W = 64
GRID_ROWS = SEQ // GRID_W
HEAD_DIM = 64
N_HEADS = 16
CONV_CH = 512
FOUR_CH = 512
FOUR_GROUP = 64
WIN_ROWS = 8
WIN_COLS = 16
D_FF = 4 * D
N_MOD = 6
EPS = 1e-6

TM = 512
TM_IN = 1024
HALO = 8
FF_CHUNK = 1024
VMEM_LIMIT = 56 * 1024 * 1024

DFT_A = 32
DFT_B = SEQ // DFT_A
SUB = 8

QCOLS = 16
NGRP = GRID_W // QCOLS
KCOLS = 32
KC0 = tuple(int(np.clip(QCOLS * g - 8, 0, GRID_W - KCOLS)) for g in range(NGRP))
RB = 8
NRB = GRID_ROWS // RB
WROWS = 16
KROWS_PER_TILE = 128 // KCOLS
NEG = -1e30
LOG2E = float(np.log2(np.e))
N_BT = 2 * WIN_ROWS - 1 + KROWS_PER_TILE - 1


def _cparams(n_axes):
    return pltpu.CompilerParams(dimension_semantics=("arbitrary",) * n_axes,
                                vmem_limit_bytes=VMEM_LIMIT)


def _const_spec(shape):
    nd = len(shape)
    return pl.BlockSpec(shape, lambda *_: (0,) * nd)


def _cast_jobs(jobs, n_steps, step_of):
    in_specs, out_specs, out_shapes = [], [], []
    for arr, layer in jobs:
        _, rows, cols = arr.shape
        chunk = rows // n_steps
        assert chunk * n_steps == rows and chunk % 16 == 0
        in_specs.append(pl.BlockSpec((None, chunk, cols), lambda *ids, layer=layer: (layer, step_of(*ids), 0)))
        out_specs.append(pl.BlockSpec((chunk, cols), lambda *ids: (step_of(*ids), 0)))
        out_shapes.append(jax.ShapeDtypeStruct((rows, cols), BF16))
    return in_specs, out_specs, out_shapes


def _with_casts(body, n_in, n_out, n_jobs):
    def wrapped(*refs):
        ins, refs = refs[:n_in], refs[n_in:]
        cast_ins, refs = refs[:n_jobs], refs[n_jobs:]
        outs, refs = refs[:n_out], refs[n_out:]
        cast_outs, scratch = refs[:n_jobs], refs[n_jobs:]
        for src, dst in zip(cast_ins, cast_outs):
            dst[...] = src[...].astype(BF16)
        body(*ins, *outs, *scratch)
    return wrapped


def _bf16_const(a):
    return jnp.asarray(a, F32).astype(BF16)


def _rms_mod(x, g, shift, scale):
    ms = jnp.mean(x * x, axis=-1, keepdims=True)
    y = x * lax.rsqrt(ms + EPS)
    return (y * g) * (1.0 + scale) + shift


def _mod_rows(mod_ref, row, ks):
    return [mod_ref[pl.ds(row, 1), k * D:(k + 1) * D] for k in ks]


def _mlp(x1, g, shift, scale, w1_ref, w2_ref):
    h = _rms_mod(x1, g, shift, scale).astype(BF16)
    acc = jnp.zeros(x1.shape, F32)
    for c in range(D_FF // FF_CHUNK):
        a = jnp.dot(h, w1_ref[:, c * FF_CHUNK:(c + 1) * FF_CHUNK], preferred_element_type=F32)
        a = jnp.maximum(a, 0.0)
        a = (a * a).astype(BF16)
        acc = acc + jnp.dot(a, w2_ref[c * FF_CHUNK:(c + 1) * FF_CHUNK, :], preferred_element_type=F32)
    return acc


def _mod_body(cond_ref, w_ref, b_ref, o_ref):
    c = cond_ref[...]
    s = c * jax.nn.sigmoid(c)
    o_ref[...] = jnp.dot(s.astype(BF16), w_ref[...].astype(BF16), preferred_element_type=F32) + b_ref[...]


def _modulation(cond8, ada_w, ada_b):
    nt = N_MOD * D // D
    return pl.pallas_call(
        _mod_body,
        grid=(DEPTH, nt),
        in_specs=[_const_spec((8, D)),
                  pl.BlockSpec((None, D, D), lambda l, n: (l, 0, n)),
                  pl.BlockSpec((None, 1, D), lambda l, n: (l, 0, n))],
        out_specs=pl.BlockSpec((None, 8, D), lambda l, n: (l, 0, n)),
        out_shape=jax.ShapeDtypeStruct((DEPTH, 8, N_MOD * D), F32),
        compiler_params=_cparams(2),
        name="adaln_mod",
    )(cond8, ada_w, ada_b.reshape(DEPTH, 1, N_MOD * D))


def _inproj_body(mod_row, seq_len, tm, xp_ref, x_ref, xn_ref, mod_ref, g_ref, w_ref, cw_ref,
                 ay_ref, f_ref):
    b = pl.program_id(0)
    t = pl.program_id(1)
    row = b if mod_row is None else mod_row
    xx = jnp.concatenate([xp_ref[...], x_ref[...], xn_ref[...]], axis=0)
    shift, scale = _mod_rows(mod_ref, row, (0, 1))
    h = _rms_mod(xx, g_ref[...], shift, scale).astype(BF16)
    u = jnp.dot(h, w_ref[...], preferred_element_type=F32)
    z = u[:, CONV_CH:2 * CONV_CH] * u[:, 0:CONV_CH]
    n = t * tm - HALO + lax.broadcasted_iota(jnp.int32, (tm + 2 * HALO, 1), 0)
    z = jnp.where((n >= 0) & (n < seq_len), z, 0.0)
    cw = cw_ref[...]
    zc = (z[HALO - 1:HALO - 1 + tm] * cw[0:1] + z[HALO:HALO + tm] * cw[1:2]
          + z[HALO + 1:HALO + 1 + tm] * cw[2:3])
    ay_ref[...] = (u[HALO:HALO + tm, 2 * CONV_CH:3 * CONV_CH] * zc).astype(BF16)
    f_ref[...] = u[HALO:HALO + tm, 3 * CONV_CH:]


def _inproj(x, mod, g, w_in_bf, conv_w8, mod_row, tm, casts=()):
    bsz, seq_len, _ = x.shape
    nt = seq_len // tm
    nb8 = seq_len // HALO
    r8 = tm // HALO
    c_in, c_out, c_shape = _cast_jobs(casts, bsz * nt, lambda b, t: b * nt + t)
    body = _with_casts(functools.partial(_inproj_body, mod_row, seq_len, tm), 7, 2, len(casts))
    return pl.pallas_call(
        body,
        grid=(bsz, nt),
        in_specs=[pl.BlockSpec((None, HALO, D), lambda b, t: (b, jnp.maximum(t * r8 - 1, 0), 0)),
                  pl.BlockSpec((None, tm, D), lambda b, t: (b, t, 0)),
                  pl.BlockSpec((None, HALO, D), lambda b, t: (b, jnp.minimum((t + 1) * r8, nb8 - 1), 0)),
                  _const_spec((8, N_MOD * D)),
                  _const_spec((1, D)),
                  _const_spec((D, 3 * CONV_CH + FOUR_CH)),
                  _const_spec((8, CONV_CH))] + c_in,
        out_specs=[pl.BlockSpec((None, tm, CONV_CH), lambda b, t: (b, t, 0)),
                   pl.BlockSpec((None, tm, FOUR_CH), lambda b, t: (b, t, 0))] + c_out,
        out_shape=[jax.ShapeDtypeStruct((bsz, seq_len, CONV_CH), BF16),
                   jax.ShapeDtypeStruct((bsz, seq_len, FOUR_CH), F32)] + c_shape,
        compiler_params=_cparams(2),
        name="mixer_in",
    )(x, x, x, mod, g, w_in_bf, conv_w8, *[a for a, _ in casts])


def _dft_consts():
    a = np.arange(DFT_A)
    s = np.arange(SUB)
    ang = 2.0 * np.pi * np.outer(a, a) / DFT_A
    eye = np.eye(SUB)
    re = np.einsum("va,ts->vtas", np.cos(ang), eye).reshape(DFT_A * SUB, DFT_A * SUB)
    im = np.einsum("va,ts->vtas", -np.sin(ang), eye).reshape(DFT_A * SUB, DFT_A * SUB)
    l1 = np.concatenate([re, im], axis=0)
    m = np.arange(DFT_B // SUB)
    bb = (SUB * m[:, None, None] + s[None, None, :])
    tang = 2.0 * np.pi * a[None, :, None] * bb / SEQ
    twr = np.cos(tang).reshape(DFT_B // SUB, DFT_A * SUB, 1)
    twi = (-np.sin(tang)).reshape(DFT_B // SUB, DFT_A * SUB, 1)
    u = np.arange(DFT_B)
    ang2 = 2.0 * np.pi * np.outer(u, u) / DFT_B
    c2, s2 = np.cos(ang2), np.sin(ang2)
    l2 = np.block([[c2, s2], [s2, -c2]])
    return l1, twr, twi, l2


DFT_MPAIR = 4
DFT_VB = 4
DFT_NA = DFT_B // SUB // DFT_MPAIR
DFT_NB = DFT_A // DFT_VB


def _channel_dft(pq, n, mix_ref):
    w = 256
    outs = []
    for cb in range(FOUR_CH // w):
        cols = slice(cb * w, (cb + 1) * w)
        lhs = jnp.concatenate([pq[:n, cols], pq[n:, cols]], axis=1).astype(BF16)
        rhs = jnp.concatenate([mix_ref[cols, cols],
                               mix_ref[FOUR_CH + cb * w:FOUR_CH + (cb + 1) * w, cols]], axis=0)
        outs.append(jnp.dot(lhs, rhs, preferred_element_type=F32))
    return jnp.concatenate(outs, axis=1).astype(BF16)


def _dft_body(x_ref, l1_ref, twr_ref, twi_ref, l2_ref, mix_ref, fy_ref, br_scr, bi_scr):
    t = pl.program_id(1)
    rows = DFT_A * SUB
    reps = FOUR_CH // 128

    @pl.when(t < DFT_NA)
    def _stage_a():
        x = x_ref[...]
        brs, bis = [], []
        for j in range(DFT_MPAIR):
            xj = x[:, j * SUB:(j + 1) * SUB, :].reshape(rows, FOUR_CH).astype(BF16)
            a = jnp.dot(l1_ref[...], xj, preferred_element_type=F32)
            ar, ai = a[:rows], a[rows:]
            twr = jnp.concatenate([twr_ref[j]] * reps, axis=1)
            twi = jnp.concatenate([twi_ref[j]] * reps, axis=1)
            brs.append((ar * twr - ai * twi).reshape(DFT_A, SUB, FOUR_CH))
            bis.append((ar * twi + ai * twr).reshape(DFT_A, SUB, FOUR_CH))
        shape = (DFT_A, 1, DFT_MPAIR * SUB, FOUR_CH)
        br_scr[:, pl.ds(t, 1)] = jnp.concatenate(brs, axis=1).astype(BF16).reshape(shape)
        bi_scr[:, pl.ds(t, 1)] = jnp.concatenate(bis, axis=1).astype(BF16).reshape(shape)

    @pl.when(t >= DFT_NA)
    def _stage_b():
        v0 = (t - DFT_NA) * DFT_VB
        for j in range(DFT_VB):
            br = br_scr[v0 + j].reshape(DFT_B, FOUR_CH)
            bi = bi_scr[v0 + j].reshape(DFT_B, FOUR_CH)
            pq = jnp.dot(l2_ref[...], jnp.concatenate([br, bi], axis=0),
                         preferred_element_type=F32)
            fy_ref[j] = _channel_dft(pq, DFT_B, mix_ref)


def _dft_2d(f, mix_bf):
    bsz = f.shape[0]
    l1, twr, twi, l2 = _dft_consts()
    nm = DFT_B // SUB
    rows = DFT_A * SUB
    twr_b = jnp.asarray(np.broadcast_to(twr, (nm, rows, 128)), F32)
    twi_b = jnp.asarray(np.broadcast_to(twi, (nm, rows, 128)), F32)
    f5 = f.reshape(bsz, DFT_A, DFT_NA, DFT_MPAIR * SUB, FOUR_CH)
    a_step = lambda t: jnp.minimum(t, DFT_NA - 1)
    b_step = lambda t: jnp.maximum(t - DFT_NA, 0)
    tw_spec = pl.BlockSpec((DFT_MPAIR, rows, 128), lambda b, t: (a_step(t), 0, 0))
    scr = pltpu.VMEM((DFT_A, DFT_NA, DFT_MPAIR * SUB, FOUR_CH), BF16)
    fy = pl.pallas_call(
        _dft_body,
        grid=(bsz, DFT_NA + DFT_NB),
        in_specs=[pl.BlockSpec((None, DFT_A, None, DFT_MPAIR * SUB, FOUR_CH),
                               lambda b, t: (b, 0, a_step(t), 0, 0)),
                  _const_spec((2 * rows, rows)), tw_spec, tw_spec,
                  _const_spec((2 * DFT_B, 2 * DFT_B)), _const_spec((2 * FOUR_CH, FOUR_CH))],
        out_specs=pl.BlockSpec((None, DFT_VB, DFT_B, FOUR_CH), lambda b, t: (b, b_step(t), 0, 0)),
        out_shape=jax.ShapeDtypeStruct((bsz, DFT_A, DFT_B, FOUR_CH), BF16),
        scratch_shapes=[scr, scr],
        compiler_params=_cparams(2),
        name="dft_2d",
    )(f5, _bf16_const(l1), twr_b, twi_b, _bf16_const(l2), mix_bf)
    return jnp.transpose(fy, (0, 2, 1, 3)).reshape(bsz, SEQ, FOUR_CH)


def _dft_small_body(f_ref, lc_ref, mix_ref, fy_ref):
    n = f_ref.shape[0]
    pq = jnp.dot(lc_ref[...], f_ref[...].astype(BF16), preferred_element_type=F32)
    fy_ref[...] = _channel_dft(pq, n, mix_ref)


def _dft_2d_small(f, mix_bf):
    bsz, n, _ = f.shape
    k = np.arange(n)
    ang = 2.0 * np.pi * np.outer(k, k) / n
    lc = np.concatenate([np.cos(ang), np.sin(ang)], axis=0)
    spec = pl.BlockSpec((None, n, FOUR_CH), lambda b: (b, 0, 0))
    return pl.pallas_call(
        _dft_small_body,
        grid=(bsz,),
        in_specs=[spec, _const_spec((2 * n, n)), _const_spec((2 * FOUR_CH, FOUR_CH))],
        out_specs=spec,
        out_shape=jax.ShapeDtypeStruct((bsz, n, FOUR_CH), BF16),
        compiler_params=_cparams(1),
        name="dft_small",
    )(f, _bf16_const(lc), mix_bf)


def _channel_mix_const(seq_len):
    k = np.arange(FOUR_GROUP)
    ang = 2.0 * np.pi * np.outer(k, k) / FOUR_GROUP
    ng = FOUR_CH // FOUR_GROUP
    bdc = np.kron(np.eye(ng), np.cos(ang))
    bds = np.kron(np.eye(ng), np.sin(ang))
    return np.concatenate([bdc, -bds], axis=0) / np.sqrt(seq_len * FOUR_GROUP)


def _mixout_mlp_body(mod_row, x_ref, ay_ref, fy_ref, mod_ref, g2_ref, wout_ref, w1_ref, w2_ref, o_ref):
    b = pl.program_id(0)
    row = b if mod_row is None else mod_row
    gate1, shift2, scale2, gate2 = _mod_rows(mod_ref, row, (2, 3, 4, 5))
    cat = jnp.concatenate([ay_ref[...], fy_ref[...]], axis=1)
    mo = jnp.dot(cat, wout_ref[...], preferred_element_type=F32)
    x1 = x_ref[...] + gate1 * mo
    o_ref[...] = x1 + gate2 * _mlp(x1, g2_ref[...], shift2, scale2, w1_ref, w2_ref)


def _mixout_mlp(x, ay, fy, mod, g2, wout_bf, w1_bf, w2_bf, mod_row, tm, casts=()):
    bsz, seq_len, _ = x.shape
    nt = seq_len // tm
    row_spec = lambda w: pl.BlockSpec((None, tm, w), lambda b, t: (b, t, 0))
    c_in, c_out, c_shape = _cast_jobs(casts, bsz * nt, lambda b, t: b * nt + t)
    body = _with_casts(functools.partial(_mixout_mlp_body, mod_row), 8, 1, len(casts))
    return pl.pallas_call(
        body,
        grid=(bsz, nt),
        in_specs=[row_spec(D), row_spec(CONV_CH), row_spec(FOUR_CH),
                  _const_spec((8, N_MOD * D)), _const_spec((1, D)),
                  _const_spec((D, D)), _const_spec((D, D_FF)), _const_spec((D_FF, D))] + c_in,
        out_specs=[row_spec(D)] + c_out,
        out_shape=[jax.ShapeDtypeStruct((bsz, seq_len, D), F32)] + c_shape,
        compiler_params=_cparams(2),
        name="mixer_out_mlp",
    )(x, ay, fy, mod, g2, wout_bf, w1_bf, w2_bf, *[a for a, _ in casts])


def _head_rms(t, bd_ref, gt):
    tt = (t * t).astype(BF16)
    w = bd_ref.shape[0]
    ms = jnp.concatenate(
        [jnp.dot(tt[:, j * w:(j + 1) * w], bd_ref[...], preferred_element_type=F32)
         for j in range(D // w)], axis=1)
    return t * lax.rsqrt(ms + EPS) * gt


def _qkv_common(mod_row, x_ref, mod_ref, g_ref, w_ref, bd_ref, qg_ref, kg_ref):
    b = pl.program_id(0)
    row = b if mod_row is None else mod_row
    shift, scale = _mod_rows(mod_ref, row, (0, 1))
    h = _rms_mod(x_ref[...], g_ref[...], shift, scale).astype(BF16)
    qkv = jnp.dot(h, w_ref[...], preferred_element_type=F32)
    q = _head_rms(qkv[:, 0:D], bd_ref, qg_ref[...]) * (HEAD_DIM ** -0.5 * LOG2E)
    k = _head_rms(qkv[:, D:2 * D], bd_ref, kg_ref[...])
    v = qkv[:, 2 * D:]
    return q, k, v


def _qkv_grid_body(x_ref, mod_ref, g_ref, w_ref, bd_ref, qg_ref, kg_ref, q_ref, k_ref, v_ref):
    q, k, v = _qkv_common(None, x_ref, mod_ref, g_ref, w_ref, bd_ref, qg_ref, kg_ref)
    for rho in range(TM // GRID_W):
        for g in range(NGRP):
            q0 = rho * GRID_W + QCOLS * g
            q_ref[g, rho * QCOLS:(rho + 1) * QCOLS, :] = q[q0:q0 + QCOLS].astype(BF16)
            k0 = rho * GRID_W + KC0[g]
            k_ref[g, rho * KCOLS:(rho + 1) * KCOLS, :] = k[k0:k0 + KCOLS].astype(BF16)
            v_ref[g, rho * KCOLS:(rho + 1) * KCOLS, :] = v[k0:k0 + KCOLS].astype(BF16)


def _qkv_ctx_body(mod_row, x_ref, mod_ref, g_ref, w_ref, bd_ref, qg_ref, kg_ref, k_ref, v_ref):
    _, k, v = _qkv_common(mod_row, x_ref, mod_ref, g_ref, w_ref, bd_ref, qg_ref, kg_ref)
    k_ref[...] = k.astype(BF16)
    v_ref[...] = v.astype(BF16)


def _qkv_consts(q_g, k_g):
    w = 256
    bd = np.kron(np.eye(w // HEAD_DIM), np.ones((HEAD_DIM, HEAD_DIM))) / HEAD_DIM
    qg = jnp.tile(q_g, N_HEADS).reshape(1, D)
    kg = jnp.tile(k_g, N_HEADS).reshape(1, D)
    return _bf16_const(bd), qg, kg


def _qkv_in_specs(tm):
    return [pl.BlockSpec((None, tm, D), lambda b, t: (b, t, 0)),
            _const_spec((8, N_MOD * D)), _const_spec((1, D)), _const_spec((D, 3 * D)),
            _const_spec((256, 256)), _const_spec((1, D)), _const_spec((1, D))]


def _qkv_grid(x, mod, g, wqkv_bf, q_g, k_g):
    bsz = x.shape[0]
    bd, qg, kg = _qkv_consts(q_g, k_g)
    nq = TM // GRID_W * QCOLS
    nk = TM // GRID_W * KCOLS
    return pl.pallas_call(
        _qkv_grid_body,
        grid=(bsz, SEQ // TM),
        in_specs=_qkv_in_specs(TM),
        out_specs=[pl.BlockSpec((None, NGRP, nq, D), lambda b, t: (b, 0, t, 0)),
                   pl.BlockSpec((None, NGRP, nk, D), lambda b, t: (b, 0, t, 0)),
                   pl.BlockSpec((None, NGRP, nk, D), lambda b, t: (b, 0, t, 0))],
        out_shape=[jax.ShapeDtypeStruct((bsz, NGRP, GRID_ROWS * QCOLS, D), BF16),
                   jax.ShapeDtypeStruct((bsz, NGRP, GRID_ROWS * KCOLS, D), BF16),
                   jax.ShapeDtypeStruct((bsz, NGRP, GRID_ROWS * KCOLS, D), BF16)],
        compiler_params=_cparams(2),
        name="qkv_grid",
    )(x, mod, g, wqkv_bf, bd, qg, kg)


def _qkv_ctx(ctx, mod, g, wqkv_bf, q_g, k_g, mod_row):
    bsz, n, _ = ctx.shape
    bd, qg, kg = _qkv_consts(q_g, k_g)
    spec = pl.BlockSpec((None, n, D), lambda b, t: (b, t, 0))
    return pl.pallas_call(
        functools.partial(_qkv_ctx_body, mod_row),
        grid=(bsz, 1),
        in_specs=_qkv_in_specs(n),
        out_specs=[spec, spec],
        out_shape=[jax.ShapeDtypeStruct((bsz, n, D), BF16)] * 2,
        compiler_params=_cparams(2),
        name="qkv_ctx",
    )(ctx, mod, g, wqkv_bf, bd, qg, kg)


def _bias_tables(rpb):
    n_ro = 2 * WIN_ROWS - 1
    n_co = 2 * WIN_COLS - 1
    g = np.arange(NGRP)[:, None, None]
    cq = np.arange(QCOLS)[None, :, None]
    kcw = (np.arange(128) % KCOLS)[None, None, :]
    c = QCOLS * g + cq
    kc = np.asarray(KC0)[:, None, None] + kcw
    cs = np.clip(c - WIN_COLS // 2, 0, GRID_W - WIN_COLS)
    col_ok = (kc >= cs) & (kc < cs + WIN_COLS)
    co = np.where(col_ok, kc - c + (WIN_COLS - 1), -1)
    onehot = (np.arange(n_co)[:, None, None, None] == co[None]).astype(np.float32)
    sel = jnp.einsum("hrc,cn->hrn", rpb, jnp.asarray(onehot.reshape(n_co, -1)),
                     precision=lax.Precision.HIGHEST)
    sel = sel.reshape(N_HEADS, n_ro, NGRP, QCOLS, 128).transpose(0, 2, 1, 3, 4) * LOG2E
    sel = jnp.where(col_ok[None, :, None], sel, NEG)
    rt = jnp.pad(sel, ((0, 0), (0, 0), (3, KROWS_PER_TILE - 1), (0, 0), (0, 0)))
    rt = rt.reshape(N_HEADS // 2, 2, NGRP, n_ro + 2 + KROWS_PER_TILE, QCOLS, 128)
    delta = np.arange(13)[:, None, None] - 8
    jj2 = (np.arange(128) // KCOLS)[None, None, :]
    rm = np.where((jj2 >= delta) & (jj2 < delta + WIN_ROWS), 0.0, NEG)
    rm = np.broadcast_to(rm, (13, QCOLS, 128)).astype(np.float32)
    return rt.astype(F32), jnp.asarray(rm)


def _tile_indices(rb, i, jq):
    wr = int(np.clip(RB * rb - WIN_ROWS // 2, 0, GRID_ROWS - WROWS))
    r = RB * rb + i
    delta = int(np.clip(r - WIN_ROWS // 2, 0, GRID_ROWS - WIN_ROWS)) - wr - KROWS_PER_TILE * jq
    if not -WIN_ROWS < delta < KROWS_PER_TILE:
        return None
    bti = wr + KROWS_PER_TILE * jq - r + (WIN_ROWS - 1) + 3
    assert 0 <= bti < N_BT and 0 <= delta + 8 < 13
    return bti, delta + 8


def _attn_body(q_ref, k_ref, v_ref, kc_ref, vc_ref, rt_ref, rm_ref, o_ref, s_scr, bt_scr, bm_scr):
    kc = kc_ref[...]
    vc = vc_ref[...]
    nq = RB * QCOLS
    nk = WROWS * KCOLS
    lane = lax.broadcasted_iota(jnp.int32, (nq, 128), 1)
    nt = nk // 128
    dn = (((1,), (1,)), ((), ()))

    @pl.when(pl.program_id(2) == 0)
    def _build_tables():
        lgrp = lax.broadcasted_iota(jnp.int32, (QCOLS, 128), 1) // KCOLS
        for e in range(2):
            for t in range(N_BT):
                blk = rt_ref[e, t + KROWS_PER_TILE - 1]
                for jj in range(KROWS_PER_TILE - 2, -1, -1):
                    blk = jnp.where(lgrp == jj, rt_ref[e, t + jj], blk)
                bt_scr[e, t] = blk
        for e in range(2):
            for i in range(RB):
                for jq in range(nt):
                    idx = _tile_indices(1, i, jq)
                    if idx is not None:
                        bm_scr[e, i * nt + jq] = bt_scr[e, idx[0]] + rm_ref[idx[1]]

    def window_row(rb):
        return jnp.clip(RB * rb - WIN_ROWS // 2, 0, GRID_ROWS - WROWS)

    def scores(rb, slot):
        rb = jnp.asarray(rb, jnp.int32)
        koff = pl.multiple_of(window_row(rb) * KCOLS, 128)
        kw = k_ref[pl.ds(koff, nk), :]
        q = q_ref[pl.ds(pl.multiple_of(rb * nq, nq), nq), :]
        for e in range(2):
            in_head = (lane >= HEAD_DIM * e) & (lane < HEAD_DIM * (e + 1))
            qm = jnp.where(in_head, q, jnp.zeros_like(q))
            s_scr[slot, e, :, 0:nk] = lax.dot_general(qm, kw, dn, preferred_element_type=F32)
            s_scr[slot, e, :, nk:] = lax.dot_general(qm, kc, dn, preferred_element_type=F32)

    def finish(rb, slot, edge_rb):
        rb = jnp.asarray(rb, jnp.int32)
        wr = window_row(rb)
        vw = v_ref[pl.ds(pl.multiple_of(wr * KCOLS, 128), nk), :]
        zero_tile = jnp.zeros((QCOLS, 128), F32)
        outs = []
        for e in range(2):
            p_rows = []
            l_rows = []
            for i in range(RB):
                rows = slice(i * QCOLS, (i + 1) * QCOLS)
                blks = {}
                for jq in range(nt):
                    idx = _tile_indices(1 if edge_rb is None else edge_rb, i, jq)
                    if idx is None:
                        continue
                    add = bm_scr[e, i * nt + jq] if edge_rb is None else bt_scr[e, idx[0]] + rm_ref[idx[1]]
                    blks[jq] = s_scr[slot, e, rows, jq * 128:(jq + 1) * 128] + add
                for jc in range(CTX // 128):
                    blks[nt + jc] = s_scr[slot, e, rows, nk + jc * 128:nk + (jc + 1) * 128]
                vals = list(blks.values())
                m = jnp.max(functools.reduce(jnp.maximum, vals), axis=-1, keepdims=True)
                ps = {j: jnp.exp2(sb - m) for j, sb in blks.items()}
                l_rows.append(jnp.sum(functools.reduce(jnp.add, list(ps.values())), axis=-1, keepdims=True))
                p_rows.append(jnp.concatenate([ps.get(j, zero_tile) for j in range(nt + CTX // 128)], axis=1))
            p = jnp.concatenate(p_rows, axis=0).astype(BF16)
            lsum = jnp.concatenate(l_rows, axis=0)
            o = (jnp.dot(p[:, :nk], vw, preferred_element_type=F32)
                 + jnp.dot(p[:, nk:], vc, preferred_element_type=F32))
            outs.append(o / lsum)
        o = jnp.where(lane < HEAD_DIM, outs[0], outs[1])
        o_ref[pl.ds(pl.multiple_of(rb * nq, nq), nq), :] = o.astype(BF16)

    scores(0, 0)
    scores(1, 1)
    finish(0, 0, 0)
    scores(2, 0)
    finish(1, 1, None)

    def pair(tt, carry):
        t0 = 2 * tt
        scores(t0 + 1, 1)
        finish(t0, 0, None)
        scores(t0 + 2, 0)
        finish(t0 + 1, 1, None)
        return carry

    lax.fori_loop(1, NRB // 2 - 1, pair, 0, unroll=True)
    scores(NRB - 1, 1)
    finish(NRB - 2, 0, None)
    finish(NRB - 1, 1, NRB - 1)


def _attention(qcb, kcb, vcb, kc, vc, bt, rm):
    bsz = qcb.shape[0]
    nhp = N_HEADS // 2
    n_t = bt.shape[3]
    return pl.pallas_call(
        _attn_body,
        grid=(nhp, NGRP, bsz),
        in_specs=[pl.BlockSpec((None, None, GRID_ROWS * QCOLS, 128), lambda h, g, b: (b, g, 0, h)),
                  pl.BlockSpec((None, None, GRID_ROWS * KCOLS, 128), lambda h, g, b: (b, g, 0, h)),
                  pl.BlockSpec((None, None, GRID_ROWS * KCOLS, 128), lambda h, g, b: (b, g, 0, h)),
                  pl.BlockSpec((None, CTX, 128), lambda h, g, b: (b, 0, h)),
                  pl.BlockSpec((None, CTX, 128), lambda h, g, b: (b, 0, h)),
                  pl.BlockSpec((None, 2, None, n_t, QCOLS, 128), lambda h, g, b: (h, 0, g, 0, 0, 0)),
                  _const_spec((13, QCOLS, 128))],
        out_specs=pl.BlockSpec((None, None, GRID_ROWS * QCOLS, 128), lambda h, g, b: (b, g, 0, h)),
        out_shape=jax.ShapeDtypeStruct((bsz, NGRP, GRID_ROWS * QCOLS, D), BF16),
        scratch_shapes=[pltpu.VMEM((2, 2, RB * QCOLS, WROWS * KCOLS + CTX), F32),
                        pltpu.VMEM((2, N_BT, QCOLS, 128), F32),
                        pltpu.VMEM((2, RB * WROWS * KCOLS // 128, QCOLS, 128), F32)],
        compiler_params=_cparams(3),
        name="nbr_attention",
    )(qcb, kcb, vcb, kc, vc, bt, rm)


def _attnout_mlp_body(x_ref, o_ref_in, mod_ref, g2_ref, wo_ref, w1_ref, w2_ref, out_ref):
    b = pl.program_id(0)
    gate1, shift2, scale2, gate2 = _mod_rows(mod_ref, b, (2, 3, 4, 5))
    chunks = [o_ref_in[g, rho * QCOLS:(rho + 1) * QCOLS, :]
              for rho in range(TM // GRID_W) for g in range(NGRP)]
    o_nat = jnp.concatenate(chunks, axis=0)
    mo = jnp.dot(o_nat, wo_ref[...], preferred_element_type=F32)
    x1 = x_ref[...] + gate1 * mo
    out_ref[...] = x1 + gate2 * _mlp(x1, g2_ref[...], shift2, scale2, w1_ref, w2_ref)


def _attnout_mlp(x, ocb, mod, g2, wo_bf, w1_bf, w2_bf):
    bsz = x.shape[0]
    nq = TM // GRID_W * QCOLS
    return pl.pallas_call(
        _attnout_mlp_body,
        grid=(bsz, SEQ // TM),
        in_specs=[pl.BlockSpec((None, TM, D), lambda b, t: (b, t, 0)),
                  pl.BlockSpec((None, NGRP, nq, D), lambda b, t: (b, 0, t, 0)),
                  _const_spec((8, N_MOD * D)), _const_spec((1, D)),
                  _const_spec((D, D)), _const_spec((D, D_FF)), _const_spec((D_FF, D))],
        out_specs=pl.BlockSpec((None, TM, D), lambda b, t: (b, t, 0)),
        out_shape=jax.ShapeDtypeStruct((bsz, SEQ, D), F32),
        compiler_params=_cparams(2),
        name="attn_out_mlp",
    )(x, ocb, mod, g2, wo_bf, w1_bf, w2_bf)


def kernel(x, c, ctx, c_ctx, ada_w, ada_b, norm_mix_g, norm_mlp_g, mlp_w1, mlp_w2, ab_w_in, ab_conv_w,
           ab_w_out, na_w_qkv, na_q_g, na_k_g, na_rpb, na_w_out):
    bsz = x.shape[0]
    ctx_row = bsz
    cond8 = jnp.zeros((8, D), F32).at[:bsz].set(c).at[ctx_row].set(c_ctx)
    mods = _modulation(cond8, ada_w, ada_b)

    bf = lambda w: w.astype(BF16)
    g_mix = norm_mix_g.reshape(DEPTH, 1, D)
    g_mlp = norm_mlp_g.reshape(DEPTH, 1, D)
    conv_w8 = jnp.zeros((8, CONV_CH), F32).at[:3].set(ab_conv_w[0])

    w_in = bf(ab_w_in[0])
    ay, f, w1_0, w2_0, w_out0 = _inproj(x, mods[0], g_mix[0], w_in, conv_w8, None, TM_IN,
                                        casts=((mlp_w1, 0), (mlp_w2, 0), (ab_w_out, 0)))
    fy = _dft_2d(f, _bf16_const(_channel_mix_const(SEQ)))
    x, w1_1, w2_1, wqkv, wo = _mixout_mlp(x, ay, fy, mods[0], g_mlp[0], w_out0, w1_0, w2_0, None, TM,
                                          casts=((mlp_w1, 1), (mlp_w2, 1), (na_w_qkv, 0), (na_w_out, 0)))

    ay_c, f_c = _inproj(ctx, mods[0], g_mix[0], w_in, conv_w8, ctx_row, CTX)
    fy_c = _dft_2d_small(f_c, _bf16_const(_channel_mix_const(CTX)))
    flat = lambda a: a.reshape(1, bsz * CTX, a.shape[-1])
    ctx, = _mixout_mlp(flat(ctx), flat(ay_c), flat(fy_c), mods[0], g_mlp[0], w_out0, w1_0, w2_0,
                       ctx_row, bsz * CTX)

    qcb, kcb, vcb = _qkv_grid(x, mods[1], g_mix[1], wqkv, na_q_g[0], na_k_g[0])
    kc, vc = _qkv_ctx(ctx, mods[1], g_mix[1], wqkv, na_q_g[0], na_k_g[0], ctx_row)
    kc, vc = kc.reshape(bsz, CTX, D), vc.reshape(bsz, CTX, D)
    bt, rm = _bias_tables(na_rpb[0])
    ocb = _attention(qcb, kcb, vcb, kc, vc, bt, rm)
    x = _attnout_mlp(x, ocb, mods[1], g_mlp[1], wo, w1_1, w2_1)
    return x
```

```python
import functools

import numpy as np
import jax
import jax.numpy as jnp
from jax import lax
from jax.experimental import pallas as pl
from jax.experimental.pallas import tpu as pltpu

F32 = jnp.float32
BF16 = jnp.bfloat16

D = 1024
DEPTH = 2
SEQ = 8192
CTX = 256
GRID_W = 64
GRID_ROWS = SEQ // GRID_W
HEAD_DIM = 64
N_HEADS = 16
CONV_CH = 512
FOUR_CH = 512
FOUR_GROUP = 64
WIN_ROWS = 8
WIN_COLS = 16
D_FF = 4 * D
N_MOD = 6
EPS = 1e-6

TM = 512
TM_IN = 1024
HALO = 8
FF_CHUNK = 1024
VMEM_LIMIT = 56 * 1024 * 1024

DFT_A = 32
DFT_B = SEQ // DFT_A
SUB = 8

QCOLS = 16
NGRP = GRID_W // QCOLS
KCOLS = 32
KC0 = tuple(int(np.clip(QCOLS * g - 8, 0, GRID_W - KCOLS)) for g in range(NGRP))
RB = 8
NRB = GRID_ROWS // RB
WROWS = 16
KROWS_PER_TILE = 128 // KCOLS
NEG = -1e30
LOG2E = float(np.log2(np.e))
N_BT = 2 * WIN_ROWS - 1 + KROWS_PER_TILE - 1


def _cparams(n_axes):
    return pltpu.CompilerParams(dimension_semantics=("arbitrary",) * n_axes,
                                vmem_limit_bytes=VMEM_LIMIT)


def _const_spec(shape):
    nd = len(shape)
    return pl.BlockSpec(shape, lambda *_: (0,) * nd)


def _cast_jobs(jobs, n_steps, step_of):
    in_specs, out_specs, out_shapes = [], [], []
    for arr, layer in jobs:
        _, rows, cols = arr.shape
        chunk = rows // n_steps
        assert chunk * n_steps == rows and chunk % 16 == 0
        in_specs.append(pl.BlockSpec((None, chunk, cols), lambda *ids, layer=layer: (layer, step_of(*ids), 0)))
        out_specs.append(pl.BlockSpec((chunk, cols), lambda *ids: (step_of(*ids), 0)))
        out_shapes.append(jax.ShapeDtypeStruct((rows, cols), BF16))
    return in_specs, out_specs, out_shapes


def _with_casts(body, n_in, n_out, n_jobs):
    def wrapped(*refs):
        ins, refs = refs[:n_in], refs[n_in:]
        cast_ins, refs = refs[:n_jobs], refs[n_jobs:]
        outs, refs = refs[:n_out], refs[n_out:]
        cast_outs, scratch = refs[:n_jobs], refs[n_jobs:]
        for src, dst in zip(cast_ins, cast_outs):
            dst[...] = src[...].astype(BF16)
        body(*ins, *outs, *scratch)
    return wrapped


def _bf16_const(a):
    return jnp.asarray(a, F32).astype(BF16)


def _rms_mod(x, g, shift, scale):
    ms = jnp.mean(x * x, axis=-1, keepdims=True)
    y = x * lax.rsqrt(ms + EPS)
    return (y * g) * (1.0 + scale) + shift


def _mod_rows(mod_ref, row, ks):
    return [mod_ref[pl.ds(row, 1), k * D:(k + 1) * D] for k in ks]


def _mlp(x1, g, shift, scale, w1_ref, w2_ref):
    h = _rms_mod(x1, g, shift, scale).astype(BF16)
    acc = jnp.zeros(x1.shape, F32)
    for c in range(D_FF // FF_CHUNK):
        a = jnp.dot(h, w1_ref[:, c * FF_CHUNK:(c + 1) * FF_CHUNK], preferred_element_type=F32)
        a = jnp.maximum(a, 0.0)
        a = (a * a).astype(BF16)
        acc = acc + jnp.dot(a, w2_ref[c * FF_CHUNK:(c + 1) * FF_CHUNK, :], preferred_element_type=F32)
    return acc


def _mod_body(cond_ref, w_ref, b_ref, o_ref):
    c = cond_ref[...]
    s = c * jax.nn.sigmoid(c)
    o_ref[...] = jnp.dot(s.astype(BF16), w_ref[...].astype(BF16), preferred_element_type=F32) + b_ref[...]


MOD_TILE = 768


def _modulation(cond8, ada_w, ada_b, casts=()):
    nt = N_MOD * D // MOD_TILE
    c_in, c_out, c_shape = _cast_jobs(casts, DEPTH * nt, lambda l, n: l * nt + n)
    return pl.pallas_call(
        _with_casts(_mod_body, 3, 1, len(casts)),
        grid=(DEPTH, nt),
        in_specs=[_const_spec((8, D)),
                  pl.BlockSpec((None, D, MOD_TILE), lambda l, n: (l, 0, n)),
                  pl.BlockSpec((None, 1, MOD_TILE), lambda l, n: (l, 0, n))] + c_in,
        out_specs=[pl.BlockSpec((None, 8, MOD_TILE), lambda l, n: (l, 0, n))] + c_out,
        out_shape=[jax.ShapeDtypeStruct((DEPTH, 8, N_MOD * D), F32)] + c_shape,
        compiler_params=_cparams(2),
        name="adaln_mod",
    )(cond8, ada_w, ada_b.reshape(DEPTH, 1, N_MOD * D), *[a for a, _ in casts])


def _inproj_body(mod_row, seq_len, tm, xp_ref, x_ref, xn_ref, mod_ref, g_ref, w_ref, cw_ref,
                 ay_ref, f_ref):
    b = pl.program_id(0)
    t = pl.program_id(1)
    row = b if mod_row is None else mod_row
    xx = jnp.concatenate([xp_ref[...], x_ref[...], xn_ref[...]], axis=0)
    shift, scale = _mod_rows(mod_ref, row, (0, 1))
    h = _rms_mod(xx, g_ref[...], shift, scale).astype(BF16)
    u = jnp.dot(h, w_ref[...], preferred_element_type=F32)
    z = u[:, CONV_CH:2 * CONV_CH] * u[:, 0:CONV_CH]
    n = t * tm - HALO + lax.broadcasted_iota(jnp.int32, (tm + 2 * HALO, 1), 0)
    z = jnp.where((n >= 0) & (n < seq_len), z, 0.0)
    cw = cw_ref[...]
    zc = (z[HALO - 1:HALO - 1 + tm] * cw[0:1] + z[HALO:HALO + tm] * cw[1:2]
          + z[HALO + 1:HALO + 1 + tm] * cw[2:3])
    ay_ref[...] = (u[HALO:HALO + tm, 2 * CONV_CH:3 * CONV_CH] * zc).astype(BF16)
    f_ref[...] = u[HALO:HALO + tm, 3 * CONV_CH:]


def _inproj(x, mod, g, w_in_bf, conv_w8, mod_row, tm, casts=()):
    bsz, seq_len, _ = x.shape
    nt = seq_len // tm
    nb8 = seq_len // HALO
    r8 = tm // HALO
    c_in, c_out, c_shape = _cast_jobs(casts, bsz * nt, lambda b, t: b * nt + t)
    body = _with_casts(functools.partial(_inproj_body, mod_row, seq_len, tm), 7, 2, len(casts))
    return pl.pallas_call(
        body,
        grid=(bsz, nt),
        in_specs=[pl.BlockSpec((None, HALO, D), lambda b, t: (b, jnp.maximum(t * r8 - 1, 0), 0)),
                  pl.BlockSpec((None, tm, D), lambda b, t: (b, t, 0)),
                  pl.BlockSpec((None, HALO, D), lambda b, t: (b, jnp.minimum((t + 1) * r8, nb8 - 1), 0)),
                  _const_spec((8, N_MOD * D)),
                  _const_spec((1, D)),
                  _const_spec((D, 3 * CONV_CH + FOUR_CH)),
                  _const_spec((8, CONV_CH))] + c_in,
        out_specs=[pl.BlockSpec((None, tm, CONV_CH), lambda b, t: (b, t, 0)),
                   pl.BlockSpec((None, tm, FOUR_CH), lambda b, t: (b, t, 0))] + c_out,
        out_shape=[jax.ShapeDtypeStruct((bsz, seq_len, CONV_CH), BF16),
                   jax.ShapeDtypeStruct((bsz, seq_len, FOUR_CH), F32)] + c_shape,
        compiler_params=_cparams(2),
        name="mixer_in",
    )(x, x, x, mod, g, w_in_bf, conv_w8, *[a for a, _ in casts])


def _dft_consts():
    a = np.arange(DFT_A)
    s = np.arange(SUB)
    ang = 2.0 * np.pi * np.outer(a, a) / DFT_A
    eye = np.eye(SUB)
    re = np.einsum("va,ts->vtas", np.cos(ang), eye).reshape(DFT_A * SUB, DFT_A * SUB)
    im = np.einsum("va,ts->vtas", -np.sin(ang), eye).reshape(DFT_A * SUB, DFT_A * SUB)
    l1 = np.concatenate([re, im], axis=0)
    m = np.arange(DFT_B // SUB)
    bb = (SUB * m[:, None, None] + s[None, None, :])
    tang = 2.0 * np.pi * a[None, :, None] * bb / SEQ
    twr = np.cos(tang).reshape(DFT_B // SUB, DFT_A * SUB, 1)
    twi = (-np.sin(tang)).reshape(DFT_B // SUB, DFT_A * SUB, 1)
    u = np.arange(DFT_B)
    ang2 = 2.0 * np.pi * np.outer(u, u) / DFT_B
    c2, s2 = np.cos(ang2), np.sin(ang2)
    l2 = np.block([[c2, s2], [s2, -c2]])
    return l1, twr, twi, l2


DFT_MPAIR = 4
DFT_VB = 4
DFT_NA = DFT_B // SUB // DFT_MPAIR
DFT_NB = DFT_A // DFT_VB


def _channel_dft(pq, n, mix_ref):
    w = 256
    outs = []
    for cb in range(FOUR_CH // w):
        cols = slice(cb * w, (cb + 1) * w)
        lhs = jnp.concatenate([pq[:n, cols], pq[n:, cols]], axis=1).astype(BF16)
        rhs = jnp.concatenate([mix_ref[cols, cols],
                               mix_ref[FOUR_CH + cb * w:FOUR_CH + (cb + 1) * w, cols]], axis=0)
        outs.append(jnp.dot(lhs, rhs, preferred_element_type=F32))
    return jnp.concatenate(outs, axis=1).astype(BF16)


def _dft_body(x_ref, l1_ref, twr_ref, twi_ref, l2_ref, mix_ref, fy_ref, br_scr, bi_scr):
    t = pl.program_id(1)
    rows = DFT_A * SUB
    reps = FOUR_CH // 128

    @pl.when(t < DFT_NA)
    def _stage_a():
        x = x_ref[...]
        brs, bis = [], []
        for j in range(DFT_MPAIR):
            xj = x[:, j * SUB:(j + 1) * SUB, :].reshape(rows, FOUR_CH).astype(BF16)
            a = jnp.dot(l1_ref[...], xj, preferred_element_type=F32)
            ar, ai = a[:rows], a[rows:]
            twr = jnp.concatenate([twr_ref[j]] * reps, axis=1)
            twi = jnp.concatenate([twi_ref[j]] * reps, axis=1)
            brs.append((ar * twr - ai * twi).reshape(DFT_A, SUB, FOUR_CH))
            bis.append((ar * twi + ai * twr).reshape(DFT_A, SUB, FOUR_CH))
        shape = (DFT_A, 1, DFT_MPAIR * SUB, FOUR_CH)
        br_scr[:, pl.ds(t, 1)] = jnp.concatenate(brs, axis=1).astype(BF16).reshape(shape)
        bi_scr[:, pl.ds(t, 1)] = jnp.concatenate(bis, axis=1).astype(BF16).reshape(shape)

    @pl.when(t >= DFT_NA)
    def _stage_b():
        v0 = (t - DFT_NA) * DFT_VB
        for j in range(DFT_VB):
            br = br_scr[v0 + j].reshape(DFT_B, FOUR_CH)
            bi = bi_scr[v0 + j].reshape(DFT_B, FOUR_CH)
            pq = jnp.dot(l2_ref[...], jnp.concatenate([br, bi], axis=0),
                         preferred_element_type=F32)
            fy_ref[j] = _channel_dft(pq, DFT_B, mix_ref)


def _dft_2d(f, mix_bf):
    bsz = f.shape[0]
    l1, twr, twi, l2 = _dft_consts()
    nm = DFT_B // SUB
    rows = DFT_A * SUB
    twr_b = jnp.asarray(np.broadcast_to(twr, (nm, rows, 128)), F32)
    twi_b = jnp.asarray(np.broadcast_to(twi, (nm, rows, 128)), F32)
    f5 = f.reshape(bsz, DFT_A, DFT_NA, DFT_MPAIR * SUB, FOUR_CH)
    a_step = lambda t: jnp.minimum(t, DFT_NA - 1)
    b_step = lambda t: jnp.maximum(t - DFT_NA, 0)
    tw_spec = pl.BlockSpec((DFT_MPAIR, rows, 128), lambda b, t: (a_step(t), 0, 0))
    scr = pltpu.VMEM((DFT_A, DFT_NA, DFT_MPAIR * SUB, FOUR_CH), BF16)
    fy = pl.pallas_call(
        _dft_body,
        grid=(bsz, DFT_NA + DFT_NB),
        in_specs=[pl.BlockSpec((None, DFT_A, None, DFT_MPAIR * SUB, FOUR_CH),
                               lambda b, t: (b, 0, a_step(t), 0, 0)),
                  _const_spec((2 * rows, rows)), tw_spec, tw_spec,
                  _const_spec((2 * DFT_B, 2 * DFT_B)), _const_spec((2 * FOUR_CH, FOUR_CH))],
        out_specs=pl.BlockSpec((None, DFT_VB, DFT_B, FOUR_CH), lambda b, t: (b, b_step(t), 0, 0)),
        out_shape=jax.ShapeDtypeStruct((bsz, DFT_A, DFT_B, FOUR_CH), BF16),
        scratch_shapes=[scr, scr],
        compiler_params=_cparams(2),
        name="dft_2d",
    )(f5, _bf16_const(l1), twr_b, twi_b, _bf16_const(l2), mix_bf)
    return jnp.transpose(fy, (0, 2, 1, 3)).reshape(bsz, SEQ, FOUR_CH)


def _dft_small_body(f_ref, lc_ref, mix_ref, fy_ref):
    n = f_ref.shape[0]
    pq = jnp.dot(lc_ref[...], f_ref[...].astype(BF16), preferred_element_type=F32)
    fy_ref[...] = _channel_dft(pq, n, mix_ref)


def _dft_2d_small(f, mix_bf):
    bsz, n, _ = f.shape
    k = np.arange(n)
    ang = 2.0 * np.pi * np.outer(k, k) / n
    lc = np.concatenate([np.cos(ang), np.sin(ang)], axis=0)
    spec = pl.BlockSpec((None, n, FOUR_CH), lambda b: (b, 0, 0))
    return pl.pallas_call(
        _dft_small_body,
        grid=(bsz,),
        in_specs=[spec, _const_spec((2 * n, n)), _const_spec((2 * FOUR_CH, FOUR_CH))],
        out_specs=spec,
        out_shape=jax.ShapeDtypeStruct((bsz, n, FOUR_CH), BF16),
        compiler_params=_cparams(1),
        name="dft_small",
    )(f, _bf16_const(lc), mix_bf)


def _channel_mix_const(seq_len):
    k = np.arange(FOUR_GROUP)
    ang = 2.0 * np.pi * np.outer(k, k) / FOUR_GROUP
    ng = FOUR_CH // FOUR_GROUP
    bdc = np.kron(np.eye(ng), np.cos(ang))
    bds = np.kron(np.eye(ng), np.sin(ang))
    return np.concatenate([bdc, -bds], axis=0) / np.sqrt(seq_len * FOUR_GROUP)


def _mixout_mlp_body(mod_row, x_ref, ay_ref, fy_ref, mod_ref, g2_ref, wout_ref, w1_ref, w2_ref, o_ref):
    b = pl.program_id(0)
    row = b if mod_row is None else mod_row
    gate1, shift2, scale2, gate2 = _mod_rows(mod_ref, row, (2, 3, 4, 5))
    cat = jnp.concatenate([ay_ref[...], fy_ref[...]], axis=1)
    mo = jnp.dot(cat, wout_ref[...], preferred_element_type=F32)
    x1 = x_ref[...] + gate1 * mo
    o_ref[...] = x1 + gate2 * _mlp(x1, g2_ref[...], shift2, scale2, w1_ref, w2_ref)


def _mixout_mlp(x, ay, fy, mod, g2, wout_bf, w1_bf, w2_bf, mod_row, tm, casts=()):
    bsz, seq_len, _ = x.shape
    nt = seq_len // tm
    row_spec = lambda w: pl.BlockSpec((None, tm, w), lambda b, t: (b, t, 0))
    c_in, c_out, c_shape = _cast_jobs(casts, bsz * nt, lambda b, t: b * nt + t)
    body = _with_casts(functools.partial(_mixout_mlp_body, mod_row), 8, 1, len(casts))
    return pl.pallas_call(
        body,
        grid=(bsz, nt),
        in_specs=[row_spec(D), row_spec(CONV_CH), row_spec(FOUR_CH),
                  _const_spec((8, N_MOD * D)), _const_spec((1, D)),
                  _const_spec((D, D)), _const_spec((D, D_FF)), _const_spec((D_FF, D))] + c_in,
        out_specs=[row_spec(D)] + c_out,
        out_shape=[jax.ShapeDtypeStruct((bsz, seq_len, D), F32)] + c_shape,
        compiler_params=_cparams(2),
        name="mixer_out_mlp",
    )(x, ay, fy, mod, g2, wout_bf, w1_bf, w2_bf, *[a for a, _ in casts])


def _head_rms(t, bd_ref, gt):
    tt = (t * t).astype(BF16)
    w = bd_ref.shape[0]
    ms = jnp.concatenate(
        [jnp.dot(tt[:, j * w:(j + 1) * w], bd_ref[...], preferred_element_type=F32)
         for j in range(D // w)], axis=1)
    return t * lax.rsqrt(ms + EPS) * gt


def _qkv_common(mod_row, x_ref, mod_ref, g_ref, w_ref, bd_ref, qg_ref, kg_ref):
    b = pl.program_id(0)
    row = b if mod_row is None else mod_row
    shift, scale = _mod_rows(mod_ref, row, (0, 1))
    h = _rms_mod(x_ref[...], g_ref[...], shift, scale).astype(BF16)
    qkv = jnp.dot(h, w_ref[...], preferred_element_type=F32)
    q = _head_rms(qkv[:, 0:D], bd_ref, qg_ref[...]) * (HEAD_DIM ** -0.5 * LOG2E)
    k = _head_rms(qkv[:, D:2 * D], bd_ref, kg_ref[...])
    v = qkv[:, 2 * D:]
    return q, k, v


def _qkv_grid_body(x_ref, mod_ref, g_ref, w_ref, bd_ref, qg_ref, kg_ref, q_ref, k_ref, v_ref):
    q, k, v = _qkv_common(None, x_ref, mod_ref, g_ref, w_ref, bd_ref, qg_ref, kg_ref)
    for rho in range(TM // GRID_W):
        for g in range(NGRP):
            q0 = rho * GRID_W + QCOLS * g
            q_ref[g, rho * QCOLS:(rho + 1) * QCOLS, :] = q[q0:q0 + QCOLS].astype(BF16)
            k0 = rho * GRID_W + KC0[g]
            k_ref[g, rho * KCOLS:(rho + 1) * KCOLS, :] = k[k0:k0 + KCOLS].astype(BF16)
            v_ref[g, rho * KCOLS:(rho + 1) * KCOLS, :] = v[k0:k0 + KCOLS].astype(BF16)


def _qkv_ctx_body(mod_row, x_ref, mod_ref, g_ref, w_ref, bd_ref, qg_ref, kg_ref, k_ref, v_ref):
    _, k, v = _qkv_common(mod_row, x_ref, mod_ref, g_ref, w_ref, bd_ref, qg_ref, kg_ref)
    k_ref[...] = k.astype(BF16)
    v_ref[...] = v.astype(BF16)


def _qkv_consts(q_g, k_g):
    w = 256
    bd = np.kron(np.eye(w // HEAD_DIM), np.ones((HEAD_DIM, HEAD_DIM))) / HEAD_DIM
    qg = jnp.tile(q_g, N_HEADS).reshape(1, D)
    kg = jnp.tile(k_g, N_HEADS).reshape(1, D)
    return _bf16_const(bd), qg, kg


def _qkv_in_specs(tm):
    return [pl.BlockSpec((None, tm, D), lambda b, t: (b, t, 0)),
            _const_spec((8, N_MOD * D)), _const_spec((1, D)), _const_spec((D, 3 * D)),
            _const_spec((256, 256)), _const_spec((1, D)), _const_spec((1, D))]


def _qkv_grid(x, mod, g, wqkv_bf, q_g, k_g):
    bsz = x.shape[0]
    bd, qg, kg = _qkv_consts(q_g, k_g)
    nq = TM // GRID_W * QCOLS
    nk = TM // GRID_W * KCOLS
    return pl.pallas_call(
        _qkv_grid_body,
        grid=(bsz, SEQ // TM),
        in_specs=_qkv_in_specs(TM),
        out_specs=[pl.BlockSpec((None, NGRP, nq, D), lambda b, t: (b, 0, t, 0)),
                   pl.BlockSpec((None, NGRP, nk, D), lambda b, t: (b, 0, t, 0)),
                   pl.BlockSpec((None, NGRP, nk, D), lambda b, t: (b, 0, t, 0))],
        out_shape=[jax.ShapeDtypeStruct((bsz, NGRP, GRID_ROWS * QCOLS, D), BF16),
                   jax.ShapeDtypeStruct((bsz, NGRP, GRID_ROWS * KCOLS, D), BF16),
                   jax.ShapeDtypeStruct((bsz, NGRP, GRID_ROWS * KCOLS, D), BF16)],
        compiler_params=_cparams(2),
        name="qkv_grid",
    )(x, mod, g, wqkv_bf, bd, qg, kg)


def _qkv_ctx(ctx, mod, g, wqkv_bf, q_g, k_g, mod_row):
    bsz, n, _ = ctx.shape
    bd, qg, kg = _qkv_consts(q_g, k_g)
    spec = pl.BlockSpec((None, n, D), lambda b, t: (b, t, 0))
    return pl.pallas_call(
        functools.partial(_qkv_ctx_body, mod_row),
        grid=(bsz, 1),
        in_specs=_qkv_in_specs(n),
        out_specs=[spec, spec],
        out_shape=[jax.ShapeDtypeStruct((bsz, n, D), BF16)] * 2,
        compiler_params=_cparams(2),
        name="qkv_ctx",
    )(ctx, mod, g, wqkv_bf, bd, qg, kg)


def _bias_tables(rpb):
    n_ro = 2 * WIN_ROWS - 1
    n_co = 2 * WIN_COLS - 1
    g = np.arange(NGRP)[:, None, None]
    cq = np.arange(QCOLS)[None, :, None]
    kcw = (np.arange(128) % KCOLS)[None, None, :]
    c = QCOLS * g + cq
    kc = np.asarray(KC0)[:, None, None] + kcw
    cs = np.clip(c - WIN_COLS // 2, 0, GRID_W - WIN_COLS)
    col_ok = (kc >= cs) & (kc < cs + WIN_COLS)
    co = np.where(col_ok, kc - c + (WIN_COLS - 1), -1)
    onehot = (np.arange(n_co)[:, None, None, None] == co[None]).astype(np.float32)
    n_pad = n_ro + 2 + KROWS_PER_TILE
    rpb_p = jnp.pad(rpb * LOG2E, ((0, 0), (3, KROWS_PER_TILE - 1), (0, 0)))
    sel = jnp.einsum("hrc,cn->hrn", rpb_p, jnp.asarray(onehot.reshape(n_co, -1)),
                     precision=lax.Precision.HIGHEST)
    sel = sel.reshape(N_HEADS, n_pad, NGRP, QCOLS, 128).transpose(0, 2, 1, 3, 4)
    rt = jnp.where(col_ok[None, :, None], sel, NEG)
    rt = rt.reshape(N_HEADS // 2, 2, NGRP, n_pad, QCOLS, 128)
    delta = np.arange(13)[:, None, None] - 8
    jj2 = (np.arange(128) // KCOLS)[None, None, :]
    rm = np.where((jj2 >= delta) & (jj2 < delta + WIN_ROWS), 0.0, NEG)
    rm = np.broadcast_to(rm, (13, QCOLS, 128)).astype(np.float32)
    return rt.astype(F32), jnp.asarray(rm)


def _tile_indices(rb, i, jq):
    wr = int(np.clip(RB * rb - WIN_ROWS // 2, 0, GRID_ROWS - WROWS))
    r = RB * rb + i
    delta = int(np.clip(r - WIN_ROWS // 2, 0, GRID_ROWS - WIN_ROWS)) - wr - KROWS_PER_TILE * jq
    if not -WIN_ROWS < delta < KROWS_PER_TILE:
        return None
    bti = wr + KROWS_PER_TILE * jq - r + (WIN_ROWS - 1) + 3
    assert 0 <= bti < N_BT and 0 <= delta + 8 < 13
    return bti, delta + 8


def _attn_body(q_ref, k_ref, v_ref, kc_ref, vc_ref, rt_ref, rm_ref, o_ref, s_scr, bt_scr, bm_scr):
    kc = kc_ref[...]
    vc = vc_ref[...]
    nq = RB * QCOLS
    nk = WROWS * KCOLS
    lane = lax.broadcasted_iota(jnp.int32, (nq, 128), 1)
    nt = nk // 128
    dn = (((1,), (1,)), ((), ()))

    @pl.when(pl.program_id(2) == 0)
    def _build_tables():
        lgrp = lax.broadcasted_iota(jnp.int32, (QCOLS, 128), 1) // KCOLS
        for e in range(2):
            for t in range(N_BT):
                blk = rt_ref[e, t + KROWS_PER_TILE - 1]
                for jj in range(KROWS_PER_TILE - 2, -1, -1):
                    blk = jnp.where(lgrp == jj, rt_ref[e, t + jj], blk)
                bt_scr[e, t] = blk
        for e in range(2):
            for i in range(RB):
                for jq in range(nt):
                    idx = _tile_indices(1, i, jq)
                    if idx is not None:
                        bm_scr[e, i * nt + jq] = bt_scr[e, idx[0]] + rm_ref[idx[1]]

    def window_row(rb):
        return jnp.clip(RB * rb - WIN_ROWS // 2, 0, GRID_ROWS - WROWS)

    def scores(rb, slot):
        rb = jnp.asarray(rb, jnp.int32)
        koff = pl.multiple_of(window_row(rb) * KCOLS, 128)
        kw = k_ref[pl.ds(koff, nk), :]
        q = q_ref[pl.ds(pl.multiple_of(rb * nq, nq), nq), :]
        for e in range(2):
            in_head = (lane >= HEAD_DIM * e) & (lane < HEAD_DIM * (e + 1))
            qm = jnp.where(in_head, q, jnp.zeros_like(q))
            s_scr[slot, e, :, 0:nk] = lax.dot_general(qm, kw, dn, preferred_element_type=F32)
            s_scr[slot, e, :, nk:] = lax.dot_general(qm, kc, dn, preferred_element_type=F32)

    def finish(rb, slot, edge_rb):
        rb = jnp.asarray(rb, jnp.int32)
        wr = window_row(rb)
        vw = v_ref[pl.ds(pl.multiple_of(wr * KCOLS, 128), nk), :]
        zero_tile = jnp.zeros((QCOLS, 128), F32)
        outs = []
        for e in range(2):
            p_rows = []
            l_rows = []
            for i in range(RB):
                rows = slice(i * QCOLS, (i + 1) * QCOLS)
                blks = {}
                for jq in range(nt):
                    idx = _tile_indices(1 if edge_rb is None else edge_rb, i, jq)
                    if idx is None:
                        continue
                    add = bm_scr[e, i * nt + jq] if edge_rb is None else bt_scr[e, idx[0]] + rm_ref[idx[1]]
                    blks[jq] = s_scr[slot, e, rows, jq * 128:(jq + 1) * 128] + add
                for jc in range(CTX // 128):
                    blks[nt + jc] = s_scr[slot, e, rows, nk + jc * 128:nk + (jc + 1) * 128]
                vals = list(blks.values())
                m = jnp.max(functools.reduce(jnp.maximum, vals), axis=-1, keepdims=True)
                ps = {j: jnp.exp2(sb - m) for j, sb in blks.items()}
                l_rows.append(jnp.sum(functools.reduce(jnp.add, list(ps.values())), axis=-1, keepdims=True))
                p_rows.append(jnp.concatenate([ps.get(j, zero_tile) for j in range(nt + CTX // 128)], axis=1))
            p = jnp.concatenate(p_rows, axis=0).astype(BF16)
            lsum = jnp.concatenate(l_rows, axis=0)
            o = (jnp.dot(p[:, :nk], vw, preferred_element_type=F32)
                 + jnp.dot(p[:, nk:], vc, preferred_element_type=F32))
            outs.append(o / lsum)
        o = jnp.where(lane < HEAD_DIM, outs[0], outs[1])
        o_ref[pl.ds(pl.multiple_of(rb * nq, nq), nq), :] = o.astype(BF16)

    scores(0, 0)
    scores(1, 1)
    finish(0, 0, 0)
    scores(2, 0)
    finish(1, 1, None)

    def pair(tt, carry):
        t0 = 2 * tt
        scores(t0 + 1, 1)
        finish(t0, 0, None)
        scores(t0 + 2, 0)
        finish(t0 + 1, 1, None)
        return carry

    lax.fori_loop(1, NRB // 2 - 1, pair, 0, unroll=True)
    scores(NRB - 1, 1)
    finish(NRB - 2, 0, None)
    finish(NRB - 1, 1, NRB - 1)


def _attention(qcb, kcb, vcb, kc, vc, bt, rm):
    bsz = qcb.shape[0]
    nhp = N_HEADS // 2
    n_t = bt.shape[3]
    return pl.pallas_call(
        _attn_body,
        grid=(nhp, NGRP, bsz),
        in_specs=[pl.BlockSpec((None, None, GRID_ROWS * QCOLS, 128), lambda h, g, b: (b, g, 0, h)),
                  pl.BlockSpec((None, None, GRID_ROWS * KCOLS, 128), lambda h, g, b: (b, g, 0, h)),
                  pl.BlockSpec((None, None, GRID_ROWS * KCOLS, 128), lambda h, g, b: (b, g, 0, h)),
                  pl.BlockSpec((None, CTX, 128), lambda h, g, b: (b, 0, h)),
                  pl.BlockSpec((None, CTX, 128), lambda h, g, b: (b, 0, h)),
                  pl.BlockSpec((None, 2, None, n_t, QCOLS, 128), lambda h, g, b: (h, 0, g, 0, 0, 0)),
                  _const_spec((13, QCOLS, 128))],
        out_specs=pl.BlockSpec((None, None, GRID_ROWS * QCOLS, 128), lambda h, g, b: (b, g, 0, h)),
        out_shape=jax.ShapeDtypeStruct((bsz, NGRP, GRID_ROWS * QCOLS, D), BF16),
        scratch_shapes=[pltpu.VMEM((2, 2, RB * QCOLS, WROWS * KCOLS + CTX), F32),
                        pltpu.VMEM((2, N_BT, QCOLS, 128), F32),
                        pltpu.VMEM((2, RB * WROWS * KCOLS // 128, QCOLS, 128), F32)],
        compiler_params=_cparams(3),
        name="nbr_attention",
    )(qcb, kcb, vcb, kc, vc, bt, rm)


def _attnout_mlp_body(x_ref, o_ref_in, mod_ref, g2_ref, wo_ref, w1_ref, w2_ref, out_ref):
    b = pl.program_id(0)
    gate1, shift2, scale2, gate2 = _mod_rows(mod_ref, b, (2, 3, 4, 5))
    chunks = [o_ref_in[g, rho * QCOLS:(rho + 1) * QCOLS, :]
              for rho in range(TM // GRID_W) for g in range(NGRP)]
    o_nat = jnp.concatenate(chunks, axis=0)
    mo = jnp.dot(o_nat, wo_ref[...], preferred_element_type=F32)
    x1 = x_ref[...] + gate1 * mo
    out_ref[...] = x1 + gate2 * _mlp(x1, g2_ref[...], shift2, scale2, w1_ref, w2_ref)


def _attnout_mlp(x, ocb, mod, g2, wo_bf, w1_bf, w2_bf):
    bsz = x.shape[0]
    nq = TM // GRID_W * QCOLS
    return pl.pallas_call(
        _attnout_mlp_body,
        grid=(bsz, SEQ // TM),
        in_specs=[pl.BlockSpec((None, TM, D), lambda b, t: (b, t, 0)),
                  pl.BlockSpec((None, NGRP, nq, D), lambda b, t: (b, 0, t, 0)),
                  _const_spec((8, N_MOD * D)), _const_spec((1, D)),
                  _const_spec((D, D)), _const_spec((D, D_FF)), _const_spec((D_FF, D))],
        out_specs=pl.BlockSpec((None, TM, D), lambda b, t: (b, t, 0)),
        out_shape=jax.ShapeDtypeStruct((bsz, SEQ, D), F32),
        compiler_params=_cparams(2),
        name="attn_out_mlp",
    )(x, ocb, mod, g2, wo_bf, w1_bf, w2_bf)


def kernel(x, c, ctx, c_ctx, ada_w, ada_b, norm_mix_g, norm_mlp_g, mlp_w1, mlp_w2, ab_w_in, ab_conv_w,
           ab_w_out, na_w_qkv, na_q_g, na_k_g, na_rpb, na_w_out):
    bsz = x.shape[0]
    ctx_row = bsz
    cond8 = jnp.zeros((8, D), F32).at[:bsz].set(c).at[ctx_row].set(c_ctx)
    mods, w_in = _modulation(cond8, ada_w, ada_b, casts=((ab_w_in, 0),))

    g_mix = norm_mix_g.reshape(DEPTH, 1, D)
    g_mlp = norm_mlp_g.reshape(DEPTH, 1, D)
    conv_w8 = jnp.zeros((8, CONV_CH), F32).at[:3].set(ab_conv_w[0])

    ay, f, w1_0, w2_0, w_out0 = _inproj(x, mods[0], g_mix[0], w_in, conv_w8, None, TM_IN,
                                        casts=((mlp_w1, 0), (mlp_w2, 0), (ab_w_out, 0)))
    fy = _dft_2d(f, _bf16_const(_channel_mix_const(SEQ)))
    x, w1_1, w2_1, wqkv, wo = _mixout_mlp(x, ay, fy, mods[0], g_mlp[0], w_out0, w1_0, w2_0, None, TM,
                                          casts=((mlp_w1, 1), (mlp_w2, 1), (na_w_qkv, 0), (na_w_out, 0)))

    ay_c, f_c = _inproj(ctx, mods[0], g_mix[0], w_in, conv_w8, ctx_row, CTX)
    fy_c = _dft_2d_small(f_c, _bf16_const(_channel_mix_const(CTX)))
    flat = lambda a: a.reshape(1, bsz * CTX, a.shape[-1])
    ctx, = _mixout_mlp(flat(ctx), flat(ay_c), flat(fy_c), mods[0], g_mlp[0], w_out0, w1_0, w2_0,
                       ctx_row, bsz * CTX)

    qcb, kcb, vcb = _qkv_grid(x, mods[1], g_mix[1], wqkv, na_q_g[0], na_k_g[0])
    kc, vc = _qkv_ctx(ctx, mods[1], g_mix[1], wqkv, na_q_g[0], na_k_g[0], ctx_row)
    kc, vc = kc.reshape(bsz, CTX, D), vc.reshape(bsz, CTX, D)
    bt, rm = _bias_tables(na_rpb[0])
    ocb = _attention(qcb, kcb, vcb, kc, vc, bt, rm)
    x = _attnout_mlp(x, ocb, mods[1], g_mlp[1], wo, w1_1, w2_1)
    return x
```

```python
import functools

import numpy as np
import jax
import jax.numpy as jnp
from jax import lax
from jax.experimental import pallas as pl
from jax.experimental.pallas import tpu as pltpu

F32 = jnp.float32
BF16 = jnp.bfloat16

D = 1024
DEPTH = 2
SEQ = 8192
CTX = 256
GRID_W = 64
GRID_ROWS = SEQ // GRID_W
HEAD_DIM = 64
N_HEADS = 16
CONV_CH = 512
FOUR_CH = 512
FOUR_GROUP = 64
WIN_ROWS = 8
WIN_COLS = 16
D_FF = 4 * D
N_MOD = 6
EPS = 1e-6

TM = 512
TM_IN = 1024
HALO = 8
FF_CHUNK = 1024
VMEM_LIMIT = 56 * 1024 * 1024

DFT_A = 32
DFT_B = SEQ // DFT_A
SUB = 8

QCOLS = 16
NGRP = GRID_W // QCOLS
KCOLS = 32
KC0 = tuple(int(np.clip(QCOLS * g - 8, 0, GRID_W - KCOLS)) for g in range(NGRP))
RB = 8
NRB = GRID_ROWS // RB
WROWS = 16
KROWS_PER_TILE = 128 // KCOLS
NEG = -1e30
LOG2E = float(np.log2(np.e))
N_BT = 2 * WIN_ROWS - 1 + KROWS_PER_TILE - 1


def _cparams(n_axes):
    return pltpu.CompilerParams(dimension_semantics=("arbitrary",) * n_axes,
                                vmem_limit_bytes=VMEM_LIMIT)


def _const_spec(shape):
    nd = len(shape)
    return pl.BlockSpec(shape, lambda *_: (0,) * nd)


def _cast_jobs(jobs, n_steps, step_of):
    in_specs, out_specs, out_shapes = [], [], []
    for arr, layer in jobs:
        _, rows, cols = arr.shape
        chunk = rows // n_steps
        assert chunk * n_steps == rows and chunk % 16 == 0
        in_specs.append(pl.BlockSpec((None, chunk, cols), lambda *ids, layer=layer: (layer, step_of(*ids), 0)))
        out_specs.append(pl.BlockSpec((chunk, cols), lambda *ids: (step_of(*ids), 0)))
        out_shapes.append(jax.ShapeDtypeStruct((rows, cols), BF16))
    return in_specs, out_specs, out_shapes


def _with_casts(body, n_in, n_out, n_jobs):
    def wrapped(*refs):
        ins, refs = refs[:n_in], refs[n_in:]
        cast_ins, refs = refs[:n_jobs], refs[n_jobs:]
        outs, refs = refs[:n_out], refs[n_out:]
        cast_outs, scratch = refs[:n_jobs], refs[n_jobs:]
        for src, dst in zip(cast_ins, cast_outs):
            dst[...] = src[...].astype(BF16)
        body(*ins, *outs, *scratch)
    return wrapped


def _bf16_const(a):
    return jnp.asarray(a, F32).astype(BF16)


def _rms_mod(x, g, shift, scale):
    ms = jnp.mean(x * x, axis=-1, keepdims=True)
    y = x * lax.rsqrt(ms + EPS)
    return (y * g) * (1.0 + scale) + shift


def _mod_rows(mod_ref, row, ks):
    return [mod_ref[pl.ds(row, 1), k * D:(k + 1) * D] for k in ks]


def _mlp(x1, g, shift, scale, w1_ref, w2_ref):
    h = _rms_mod(x1, g, shift, scale).astype(BF16)
    acc = jnp.zeros(x1.shape, F32)
    for c in range(D_FF // FF_CHUNK):
        a = jnp.dot(h, w1_ref[:, c * FF_CHUNK:(c + 1) * FF_CHUNK], preferred_element_type=F32)
        a = jnp.maximum(a, 0.0)
        a = (a * a).astype(BF16)
        acc = acc + jnp.dot(a, w2_ref[c * FF_CHUNK:(c + 1) * FF_CHUNK, :], preferred_element_type=F32)
    return acc


MOD_KB = 128


def _mod_body(cond_ref, w_ref, b_ref, o_ref):
    @pl.when(pl.program_id(1) == 0)
    def _init():
        o_ref[...] = jnp.broadcast_to(b_ref[...], o_ref.shape)

    c = cond_ref[...]
    s = c * jax.nn.sigmoid(c)
    o_ref[...] += jnp.dot(s.astype(BF16), w_ref[...].astype(BF16), preferred_element_type=F32)


def _modulation(cond8, ada_w, ada_b, casts=()):
    nk = D // MOD_KB
    c_in, c_out, c_shape = _cast_jobs(casts, DEPTH * nk, lambda l, k: l * nk + k)
    return pl.pallas_call(
        _with_casts(_mod_body, 3, 1, len(casts)),
        grid=(DEPTH, nk),
        in_specs=[pl.BlockSpec((8, MOD_KB), lambda l, k: (0, k)),
                  pl.BlockSpec((None, MOD_KB, N_MOD * D), lambda l, k: (l, k, 0)),
                  pl.BlockSpec((None, 1, N_MOD * D), lambda l, k: (l, 0, 0))] + c_in,
        out_specs=[pl.BlockSpec((None, 8, N_MOD * D), lambda l, k: (l, 0, 0))] + c_out,
        out_shape=[jax.ShapeDtypeStruct((DEPTH, 8, N_MOD * D), F32)] + c_shape,
        compiler_params=_cparams(2),
        name="adaln_mod",
    )(cond8, ada_w, ada_b.reshape(DEPTH, 1, N_MOD * D), *[a for a, _ in casts])


def _inproj_body(mod_row, seq_len, tm, xp_ref, x_ref, xn_ref, mod_ref, g_ref, w_ref, cw_ref,
                 ay_ref, f_ref):
    b = pl.program_id(0)
    t = pl.program_id(1)
    row = b if mod_row is None else mod_row
    xx = jnp.concatenate([xp_ref[...], x_ref[...], xn_ref[...]], axis=0)
    shift, scale = _mod_rows(mod_ref, row, (0, 1))
    h = _rms_mod(xx, g_ref[...], shift, scale).astype(BF16)
    u = jnp.dot(h, w_ref[...], preferred_element_type=F32)
    z = u[:, CONV_CH:2 * CONV_CH] * u[:, 0:CONV_CH]
    n = t * tm - HALO + lax.broadcasted_iota(jnp.int32, (tm + 2 * HALO, 1), 0)
    z = jnp.where((n >= 0) & (n < seq_len), z, 0.0)
    cw = cw_ref[...]
    zc = (z[HALO - 1:HALO - 1 + tm] * cw[0:1] + z[HALO:HALO + tm] * cw[1:2]
          + z[HALO + 1:HALO + 1 + tm] * cw[2:3])
    ay_ref[...] = (u[HALO:HALO + tm, 2 * CONV_CH:3 * CONV_CH] * zc).astype(BF16)
    f_ref[...] = u[HALO:HALO + tm, 3 * CONV_CH:]


def _inproj(x, mod, g, w_in_bf, conv_w8, mod_row, tm, casts=()):
    bsz, seq_len, _ = x.shape
    nt = seq_len // tm
    nb8 = seq_len // HALO
    r8 = tm // HALO
    c_in, c_out, c_shape = _cast_jobs(casts, bsz * nt, lambda b, t: b * nt + t)
    body = _with_casts(functools.partial(_inproj_body, mod_row, seq_len, tm), 7, 2, len(casts))
    return pl.pallas_call(
        body,
        grid=(bsz, nt),
        in_specs=[pl.BlockSpec((None, HALO, D), lambda b, t: (b, jnp.maximum(t * r8 - 1, 0), 0)),
                  pl.BlockSpec((None, tm, D), lambda b, t: (b, t, 0)),
                  pl.BlockSpec((None, HALO, D), lambda b, t: (b, jnp.minimum((t + 1) * r8, nb8 - 1), 0)),
                  _const_spec((8, N_MOD * D)),
                  _const_spec((1, D)),
                  _const_spec((D, 3 * CONV_CH + FOUR_CH)),
                  _const_spec((8, CONV_CH))] + c_in,
        out_specs=[pl.BlockSpec((None, tm, CONV_CH), lambda b, t: (b, t, 0)),
                   pl.BlockSpec((None, tm, FOUR_CH), lambda b, t: (b, t, 0))] + c_out,
        out_shape=[jax.ShapeDtypeStruct((bsz, seq_len, CONV_CH), BF16),
                   jax.ShapeDtypeStruct((bsz, seq_len, FOUR_CH), F32)] + c_shape,
        compiler_params=_cparams(2),
        name="mixer_in",
    )(x, x, x, mod, g, w_in_bf, conv_w8, *[a for a, _ in casts])


def _dft_consts():
    a = np.arange(DFT_A)
    s = np.arange(SUB)
    ang = 2.0 * np.pi * np.outer(a, a) / DFT_A
    eye = np.eye(SUB)
    re = np.einsum("va,ts->vtas", np.cos(ang), eye).reshape(DFT_A * SUB, DFT_A * SUB)
    im = np.einsum("va,ts->vtas", -np.sin(ang), eye).reshape(DFT_A * SUB, DFT_A * SUB)
    l1 = np.concatenate([re, im], axis=0)
    m = np.arange(DFT_B // SUB)
    bb = (SUB * m[:, None, None] + s[None, None, :])
    tang = 2.0 * np.pi * a[None, :, None] * bb / SEQ
    twr = np.cos(tang).reshape(DFT_B // SUB, DFT_A * SUB, 1)
    twi = (-np.sin(tang)).reshape(DFT_B // SUB, DFT_A * SUB, 1)
    u = np.arange(DFT_B)
    ang2 = 2.0 * np.pi * np.outer(u, u) / DFT_B
    c2, s2 = np.cos(ang2), np.sin(ang2)
    l2 = np.block([[c2, s2], [s2, -c2]])
    return l1, twr, twi, l2


DFT_MPAIR = 4
DFT_VB = 4
DFT_NA = DFT_B // SUB // DFT_MPAIR
DFT_NB = DFT_A // DFT_VB


def _channel_dft(pq, n, mix_ref):
    w = 256
    outs = []
    for cb in range(FOUR_CH // w):
        cols = slice(cb * w, (cb + 1) * w)
        lhs = jnp.concatenate([pq[:n, cols], pq[n:, cols]], axis=1).astype(BF16)
        rhs = jnp.concatenate([mix_ref[cols, cols],
                               mix_ref[FOUR_CH + cb * w:FOUR_CH + (cb + 1) * w, cols]], axis=0)
        outs.append(jnp.dot(lhs, rhs, preferred_element_type=F32))
    return jnp.concatenate(outs, axis=1).astype(BF16)


def _dft_body(x_ref, l1_ref, twr_ref, twi_ref, l2_ref, mix_ref, fy_ref, br_scr, bi_scr):
    t = pl.program_id(1)
    rows = DFT_A * SUB
    reps = FOUR_CH // 128

    @pl.when(t < DFT_NA)
    def _stage_a():
        x = x_ref[...]
        brs, bis = [], []
        for j in range(DFT_MPAIR):
            xj = x[:, j * SUB:(j + 1) * SUB, :].reshape(rows, FOUR_CH).astype(BF16)
            a = jnp.dot(l1_ref[...], xj, preferred_element_type=F32)
            ar, ai = a[:rows], a[rows:]
            twr = jnp.concatenate([twr_ref[j]] * reps, axis=1)
            twi = jnp.concatenate([twi_ref[j]] * reps, axis=1)
            brs.append((ar * twr - ai * twi).reshape(DFT_A, SUB, FOUR_CH))
            bis.append((ar * twi + ai * twr).reshape(DFT_A, SUB, FOUR_CH))
        shape = (DFT_A, 1, DFT_MPAIR * SUB, FOUR_CH)
        br_scr[:, pl.ds(t, 1)] = jnp.concatenate(brs, axis=1).astype(BF16).reshape(shape)
        bi_scr[:, pl.ds(t, 1)] = jnp.concatenate(bis, axis=1).astype(BF16).reshape(shape)

    @pl.when(t >= DFT_NA)
    def _stage_b():
        v0 = (t - DFT_NA) * DFT_VB
        for j in range(DFT_VB):
            br = br_scr[v0 + j].reshape(DFT_B, FOUR_CH)
            bi = bi_scr[v0 + j].reshape(DFT_B, FOUR_CH)
            pq = jnp.dot(l2_ref[...], jnp.concatenate([br, bi], axis=0),
                         preferred_element_type=F32)
            fy_ref[j] = _channel_dft(pq, DFT_B, mix_ref)


def _dft_2d(f, mix_bf):
    bsz = f.shape[0]
    l1, twr, twi, l2 = _dft_consts()
    nm = DFT_B // SUB
    rows = DFT_A * SUB
    twr_b = jnp.asarray(np.broadcast_to(twr, (nm, rows, 128)), F32)
    twi_b = jnp.asarray(np.broadcast_to(twi, (nm, rows, 128)), F32)
    f5 = f.reshape(bsz, DFT_A, DFT_NA, DFT_MPAIR * SUB, FOUR_CH)
    a_step = lambda t: jnp.minimum(t, DFT_NA - 1)
    b_step = lambda t: jnp.maximum(t - DFT_NA, 0)
    tw_spec = pl.BlockSpec((DFT_MPAIR, rows, 128), lambda b, t: (a_step(t), 0, 0))
    scr = pltpu.VMEM((DFT_A, DFT_NA, DFT_MPAIR * SUB, FOUR_CH), BF16)
    fy = pl.pallas_call(
        _dft_body,
        grid=(bsz, DFT_NA + DFT_NB),
        in_specs=[pl.BlockSpec((None, DFT_A, None, DFT_MPAIR * SUB, FOUR_CH),
                               lambda b, t: (b, 0, a_step(t), 0, 0)),
                  _const_spec((2 * rows, rows)), tw_spec, tw_spec,
                  _const_spec((2 * DFT_B, 2 * DFT_B)), _const_spec((2 * FOUR_CH, FOUR_CH))],
        out_specs=pl.BlockSpec((None, DFT_VB, DFT_B, FOUR_CH), lambda b, t: (b, b_step(t), 0, 0)),
        out_shape=jax.ShapeDtypeStruct((bsz, DFT_A, DFT_B, FOUR_CH), BF16),
        scratch_shapes=[scr, scr],
        compiler_params=_cparams(2),
        name="dft_2d",
    )(f5, _bf16_const(l1), twr_b, twi_b, _bf16_const(l2), mix_bf)
    return jnp.transpose(fy, (0, 2, 1, 3)).reshape(bsz, SEQ, FOUR_CH)


def _dft_small_body(f_ref, lc_ref, mix_ref, fy_ref):
    n = f_ref.shape[0]
    pq = jnp.dot(lc_ref[...], f_ref[...].astype(BF16), preferred_element_type=F32)
    fy_ref[...] = _channel_dft(pq, n, mix_ref)


def _dft_2d_small(f, mix_bf):
    bsz, n, _ = f.shape
    k = np.arange(n)
    ang = 2.0 * np.pi * np.outer(k, k) / n
    lc = np.concatenate([np.cos(ang), np.sin(ang)], axis=0)
    spec = pl.BlockSpec((None, n, FOUR_CH), lambda b: (b, 0, 0))
    return pl.pallas_call(
        _dft_small_body,
        grid=(bsz,),
        in_specs=[spec, _const_spec((2 * n, n)), _const_spec((2 * FOUR_CH, FOUR_CH))],
        out_specs=spec,
        out_shape=jax.ShapeDtypeStruct((bsz, n, FOUR_CH), BF16),
        compiler_params=_cparams(1),
        name="dft_small",
    )(f, _bf16_const(lc), mix_bf)


def _channel_mix_const(seq_len):
    k = np.arange(FOUR_GROUP)
    ang = 2.0 * np.pi * np.outer(k, k) / FOUR_GROUP
    ng = FOUR_CH // FOUR_GROUP
    bdc = np.kron(np.eye(ng), np.cos(ang))
    bds = np.kron(np.eye(ng), np.sin(ang))
    return np.concatenate([bdc, -bds], axis=0) / np.sqrt(seq_len * FOUR_GROUP)


def _mixout_mlp_body(mod_row, x_ref, ay_ref, fy_ref, mod_ref, g2_ref, wout_ref, w1_ref, w2_ref, o_ref):
    b = pl.program_id(0)
    row = b if mod_row is None else mod_row
    gate1, shift2, scale2, gate2 = _mod_rows(mod_ref, row, (2, 3, 4, 5))
    cat = jnp.concatenate([ay_ref[...], fy_ref[...]], axis=1)
    mo = jnp.dot(cat, wout_ref[...], preferred_element_type=F32)
    x1 = x_ref[...] + gate1 * mo
    o_ref[...] = x1 + gate2 * _mlp(x1, g2_ref[...], shift2, scale2, w1_ref, w2_ref)


def _mixout_mlp(x, ay, fy, mod, g2, wout_bf, w1_bf, w2_bf, mod_row, tm, casts=()):
    bsz, seq_len, _ = x.shape
    nt = seq_len // tm
    row_spec = lambda w: pl.BlockSpec((None, tm, w), lambda b, t: (b, t, 0))
    c_in, c_out, c_shape = _cast_jobs(casts, bsz * nt, lambda b, t: b * nt + t)
    body = _with_casts(functools.partial(_mixout_mlp_body, mod_row), 8, 1, len(casts))
    return pl.pallas_call(
        body,
        grid=(bsz, nt),
        in_specs=[row_spec(D), row_spec(CONV_CH), row_spec(FOUR_CH),
                  _const_spec((8, N_MOD * D)), _const_spec((1, D)),
                  _const_spec((D, D)), _const_spec((D, D_FF)), _const_spec((D_FF, D))] + c_in,
        out_specs=[row_spec(D)] + c_out,
        out_shape=[jax.ShapeDtypeStruct((bsz, seq_len, D), F32)] + c_shape,
        compiler_params=_cparams(2),
        name="mixer_out_mlp",
    )(x, ay, fy, mod, g2, wout_bf, w1_bf, w2_bf, *[a for a, _ in casts])


def _head_rms(t, bd_ref, gt):
    tt = (t * t).astype(BF16)
    w = bd_ref.shape[0]
    ms = jnp.concatenate(
        [jnp.dot(tt[:, j * w:(j + 1) * w], bd_ref[...], preferred_element_type=F32)
         for j in range(D // w)], axis=1)
    return t * lax.rsqrt(ms + EPS) * gt


def _qkv_common(mod_row, x_ref, mod_ref, g_ref, w_ref, bd_ref, qg_ref, kg_ref):
    b = pl.program_id(0)
    row = b if mod_row is None else mod_row
    shift, scale = _mod_rows(mod_ref, row, (0, 1))
    h = _rms_mod(x_ref[...], g_ref[...], shift, scale).astype(BF16)
    qkv = jnp.dot(h, w_ref[...], preferred_element_type=F32)
    q = _head_rms(qkv[:, 0:D], bd_ref, qg_ref[...]) * (HEAD_DIM ** -0.5 * LOG2E)
    k = _head_rms(qkv[:, D:2 * D], bd_ref, kg_ref[...])
    v = qkv[:, 2 * D:]
    return q, k, v


def _qkv_grid_body(x_ref, mod_ref, g_ref, w_ref, bd_ref, qg_ref, kg_ref, q_ref, k_ref, v_ref):
    q, k, v = _qkv_common(None, x_ref, mod_ref, g_ref, w_ref, bd_ref, qg_ref, kg_ref)
    for rho in range(TM // GRID_W):
        for g in range(NGRP):
            q0 = rho * GRID_W + QCOLS * g
            q_ref[g, rho * QCOLS:(rho + 1) * QCOLS, :] = q[q0:q0 + QCOLS].astype(BF16)
            k0 = rho * GRID_W + KC0[g]
            k_ref[g, rho * KCOLS:(rho + 1) * KCOLS, :] = k[k0:k0 + KCOLS].astype(BF16)
            v_ref[g, rho * KCOLS:(rho + 1) * KCOLS, :] = v[k0:k0 + KCOLS].astype(BF16)


def _qkv_ctx_body(mod_row, x_ref, mod_ref, g_ref, w_ref, bd_ref, qg_ref, kg_ref, k_ref, v_ref):
    _, k, v = _qkv_common(mod_row, x_ref, mod_ref, g_ref, w_ref, bd_ref, qg_ref, kg_ref)
    k_ref[...] = k.astype(BF16)
    v_ref[...] = v.astype(BF16)


def _qkv_consts(q_g, k_g):
    w = 256
    bd = np.kron(np.eye(w // HEAD_DIM), np.ones((HEAD_DIM, HEAD_DIM))) / HEAD_DIM
    qg = jnp.tile(q_g, N_HEADS).reshape(1, D)
    kg = jnp.tile(k_g, N_HEADS).reshape(1, D)
    return _bf16_const(bd), qg, kg


def _qkv_in_specs(tm):
    return [pl.BlockSpec((None, tm, D), lambda b, t: (b, t, 0)),
            _const_spec((8, N_MOD * D)), _const_spec((1, D)), _const_spec((D, 3 * D)),
            _const_spec((256, 256)), _const_spec((1, D)), _const_spec((1, D))]


def _qkv_grid(x, mod, g, wqkv_bf, q_g, k_g):
    bsz = x.shape[0]
    bd, qg, kg = _qkv_consts(q_g, k_g)
    nq = TM // GRID_W * QCOLS
    nk = TM // GRID_W * KCOLS
    return pl.pallas_call(
        _qkv_grid_body,
        grid=(bsz, SEQ // TM),
        in_specs=_qkv_in_specs(TM),
        out_specs=[pl.BlockSpec((None, NGRP, nq, D), lambda b, t: (b, 0, t, 0)),
                   pl.BlockSpec((None, NGRP, nk, D), lambda b, t: (b, 0, t, 0)),
                   pl.BlockSpec((None, NGRP, nk, D), lambda b, t: (b, 0, t, 0))],
        out_shape=[jax.ShapeDtypeStruct((bsz, NGRP, GRID_ROWS * QCOLS, D), BF16),
                   jax.ShapeDtypeStruct((bsz, NGRP, GRID_ROWS * KCOLS, D), BF16),
                   jax.ShapeDtypeStruct((bsz, NGRP, GRID_ROWS * KCOLS, D), BF16)],
        compiler_params=_cparams(2),
        name="qkv_grid",
    )(x, mod, g, wqkv_bf, bd, qg, kg)


def _qkv_ctx(ctx, mod, g, wqkv_bf, q_g, k_g, mod_row):
    bsz, n, _ = ctx.shape
    bd, qg, kg = _qkv_consts(q_g, k_g)
    spec = pl.BlockSpec((None, n, D), lambda b, t: (b, t, 0))
    return pl.pallas_call(
        functools.partial(_qkv_ctx_body, mod_row),
        grid=(bsz, 1),
        in_specs=_qkv_in_specs(n),
        out_specs=[spec, spec],
        out_shape=[jax.ShapeDtypeStruct((bsz, n, D), BF16)] * 2,
        compiler_params=_cparams(2),
        name="qkv_ctx",
    )(ctx, mod, g, wqkv_bf, bd, qg, kg)


def _bias_tables(rpb):
    n_ro = 2 * WIN_ROWS - 1
    n_co = 2 * WIN_COLS - 1
    g = np.arange(NGRP)[:, None, None]
    cq = np.arange(QCOLS)[None, :, None]
    kcw = (np.arange(128) % KCOLS)[None, None, :]
    c = QCOLS * g + cq
    kc = np.asarray(KC0)[:, None, None] + kcw
    cs = np.clip(c - WIN_COLS // 2, 0, GRID_W - WIN_COLS)
    col_ok = (kc >= cs) & (kc < cs + WIN_COLS)
    co = np.where(col_ok, kc - c + (WIN_COLS - 1), -1)
    onehot = (np.arange(n_co)[:, None, None, None] == co[None]).astype(np.float32)
    n_pad = n_ro + 2 + KROWS_PER_TILE
    rpb_p = jnp.pad(rpb * LOG2E, ((0, 0), (3, KROWS_PER_TILE - 1), (0, 0)))
    sel = jnp.einsum("hrc,cn->hrn", rpb_p, jnp.asarray(onehot.reshape(n_co, -1)),
                     precision=lax.Precision.HIGHEST)
    sel = sel.reshape(N_HEADS, n_pad, NGRP, QCOLS, 128).transpose(0, 2, 1, 3, 4)
    rt = jnp.where(col_ok[None, :, None], sel, NEG)
    rt = rt.reshape(N_HEADS // 2, 2, NGRP, n_pad, QCOLS, 128)
    delta = np.arange(13)[:, None, None] - 8
    jj2 = (np.arange(128) // KCOLS)[None, None, :]
    rm = np.where((jj2 >= delta) & (jj2 < delta + WIN_ROWS), 0.0, NEG)
    rm = np.broadcast_to(rm, (13, QCOLS, 128)).astype(np.float32)
    return rt.astype(F32), jnp.asarray(rm)


def _tile_indices(rb, i, jq):
    wr = int(np.clip(RB * rb - WIN_ROWS // 2, 0, GRID_ROWS - WROWS))
    r = RB * rb + i
    delta = int(np.clip(r - WIN_ROWS // 2, 0, GRID_ROWS - WIN_ROWS)) - wr - KROWS_PER_TILE * jq
    if not -WIN_ROWS < delta < KROWS_PER_TILE:
        return None
    bti = wr + KROWS_PER_TILE * jq - r + (WIN_ROWS - 1) + 3
    assert 0 <= bti < N_BT and 0 <= delta + 8 < 13
    return bti, delta + 8


def _attn_body(q_ref, k_ref, v_ref, kc_ref, vc_ref, rt_ref, rm_ref, o_ref, s_scr, bt_scr, bm_scr):
    kc = kc_ref[...]
    vc = vc_ref[...]
    nq = RB * QCOLS
    nk = WROWS * KCOLS
    lane = lax.broadcasted_iota(jnp.int32, (nq, 128), 1)
    nt = nk // 128
    dn = (((1,), (1,)), ((), ()))

    @pl.when(pl.program_id(2) == 0)
    def _build_tables():
        lgrp = lax.broadcasted_iota(jnp.int32, (QCOLS, 128), 1) // KCOLS
        for e in range(2):
            for t in range(N_BT):
                blk = rt_ref[e, t + KROWS_PER_TILE - 1]
                for jj in range(KROWS_PER_TILE - 2, -1, -1):
                    blk = jnp.where(lgrp == jj, rt_ref[e, t + jj], blk)
                bt_scr[e, t] = blk
        for e in range(2):
            for i in range(RB):
                for jq in range(nt):
                    idx = _tile_indices(1, i, jq)
                    if idx is not None:
                        bm_scr[e, i * nt + jq] = bt_scr[e, idx[0]] + rm_ref[idx[1]]

    def window_row(rb):
        return jnp.clip(RB * rb - WIN_ROWS // 2, 0, GRID_ROWS - WROWS)

    def scores(rb, slot):
        rb = jnp.asarray(rb, jnp.int32)
        koff = pl.multiple_of(window_row(rb) * KCOLS, 128)
        kw = k_ref[pl.ds(koff, nk), :]
        q = q_ref[pl.ds(pl.multiple_of(rb * nq, nq), nq), :]
        for e in range(2):
            in_head = (lane >= HEAD_DIM * e) & (lane < HEAD_DIM * (e + 1))
            qm = jnp.where(in_head, q, jnp.zeros_like(q))
            s_scr[slot, e, :, 0:nk] = lax.dot_general(qm, kw, dn, preferred_element_type=F32)
            s_scr[slot, e, :, nk:] = lax.dot_general(qm, kc, dn, preferred_element_type=F32)

    def finish(rb, slot, edge_rb):
        rb = jnp.asarray(rb, jnp.int32)
        wr = window_row(rb)
        vw = v_ref[pl.ds(pl.multiple_of(wr * KCOLS, 128), nk), :]
        zero_tile = jnp.zeros((QCOLS, 128), F32)
        outs = []
        for e in range(2):
            p_rows = []
            l_rows = []
            for i in range(RB):
                rows = slice(i * QCOLS, (i + 1) * QCOLS)
                blks = {}
                for jq in range(nt):
                    idx = _tile_indices(1 if edge_rb is None else edge_rb, i, jq)
                    if idx is None:
                        continue
                    add = bm_scr[e, i * nt + jq] if edge_rb is None else bt_scr[e, idx[0]] + rm_ref[idx[1]]
                    blks[jq] = s_scr[slot, e, rows, jq * 128:(jq + 1) * 128] + add
                for jc in range(CTX // 128):
                    blks[nt + jc] = s_scr[slot, e, rows, nk + jc * 128:nk + (jc + 1) * 128]
                vals = list(blks.values())
                m = jnp.max(functools.reduce(jnp.maximum, vals), axis=-1, keepdims=True)
                ps = {j: jnp.exp2(sb - m) for j, sb in blks.items()}
                l_rows.append(jnp.sum(functools.reduce(jnp.add, list(ps.values())), axis=-1, keepdims=True))
                p_rows.append(jnp.concatenate([ps.get(j, zero_tile) for j in range(nt + CTX // 128)], axis=1))
            p = jnp.concatenate(p_rows, axis=0).astype(BF16)
            lsum = jnp.concatenate(l_rows, axis=0)
            o = (jnp.dot(p[:, :nk], vw, preferred_element_type=F32)
                 + jnp.dot(p[:, nk:], vc, preferred_element_type=F32))
            outs.append(o / lsum)
        o = jnp.where(lane < HEAD_DIM, outs[0], outs[1])
        o_ref[pl.ds(pl.multiple_of(rb * nq, nq), nq), :] = o.astype(BF16)

    scores(0, 0)
    scores(1, 1)
    finish(0, 0, 0)
    scores(2, 0)
    finish(1, 1, None)

    def pair(tt, carry):
        t0 = 2 * tt
        scores(t0 + 1, 1)
        finish(t0, 0, None)
        scores(t0 + 2, 0)
        finish(t0 + 1, 1, None)
        return carry

    lax.fori_loop(1, NRB // 2 - 1, pair, 0, unroll=True)
    scores(NRB - 1, 1)
    finish(NRB - 2, 0, None)
    finish(NRB - 1, 1, NRB - 1)


def _attention(qcb, kcb, vcb, kc, vc, bt, rm):
    bsz = qcb.shape[0]
    nhp = N_HEADS // 2
    n_t = bt.shape[3]
    return pl.pallas_call(
        _attn_body,
        grid=(nhp, NGRP, bsz),
        in_specs=[pl.BlockSpec((None, None, GRID_ROWS * QCOLS, 128), lambda h, g, b: (b, g, 0, h)),
                  pl.BlockSpec((None, None, GRID_ROWS * KCOLS, 128), lambda h, g, b: (b, g, 0, h)),
                  pl.BlockSpec((None, None, GRID_ROWS * KCOLS, 128), lambda h, g, b: (b, g, 0, h)),
                  pl.BlockSpec((None, CTX, 128), lambda h, g, b: (b, 0, h)),
                  pl.BlockSpec((None, CTX, 128), lambda h, g, b: (b, 0, h)),
                  pl.BlockSpec((None, 2, None, n_t, QCOLS, 128), lambda h, g, b: (h, 0, g, 0, 0, 0)),
                  _const_spec((13, QCOLS, 128))],
        out_specs=pl.BlockSpec((None, None, GRID_ROWS * QCOLS, 128), lambda h, g, b: (b, g, 0, h)),
        out_shape=jax.ShapeDtypeStruct((bsz, NGRP, GRID_ROWS * QCOLS, D), BF16),
        scratch_shapes=[pltpu.VMEM((2, 2, RB * QCOLS, WROWS * KCOLS + CTX), F32),
                        pltpu.VMEM((2, N_BT, QCOLS, 128), F32),
                        pltpu.VMEM((2, RB * WROWS * KCOLS // 128, QCOLS, 128), F32)],
        compiler_params=_cparams(3),
        name="nbr_attention",
    )(qcb, kcb, vcb, kc, vc, bt, rm)


def _attnout_mlp_body(x_ref, o_ref_in, mod_ref, g2_ref, wo_ref, w1_ref, w2_ref, out_ref):
    b = pl.program_id(0)
    gate1, shift2, scale2, gate2 = _mod_rows(mod_ref, b, (2, 3, 4, 5))
    chunks = [o_ref_in[g, rho * QCOLS:(rho + 1) * QCOLS, :]
              for rho in range(TM // GRID_W) for g in range(NGRP)]
    o_nat = jnp.concatenate(chunks, axis=0)
    mo = jnp.dot(o_nat, wo_ref[...], preferred_element_type=F32)
    x1 = x_ref[...] + gate1 * mo
    out_ref[...] = x1 + gate2 * _mlp(x1, g2_ref[...], shift2, scale2, w1_ref, w2_ref)


def _attnout_mlp(x, ocb, mod, g2, wo_bf, w1_bf, w2_bf):
    bsz = x.shape[0]
    nq = TM // GRID_W * QCOLS
    return pl.pallas_call(
        _attnout_mlp_body,
        grid=(bsz, SEQ // TM),
        in_specs=[pl.BlockSpec((None, TM, D), lambda b, t: (b, t, 0)),
                  pl.BlockSpec((None, NGRP, nq, D), lambda b, t: (b, 0, t, 0)),
                  _const_spec((8, N_MOD * D)), _const_spec((1, D)),
                  _const_spec((D, D)), _const_spec((D, D_FF)), _const_spec((D_FF, D))],
        out_specs=pl.BlockSpec((None, TM, D), lambda b, t: (b, t, 0)),
        out_shape=jax.ShapeDtypeStruct((bsz, SEQ, D), F32),
        compiler_params=_cparams(2),
        name="attn_out_mlp",
    )(x, ocb, mod, g2, wo_bf, w1_bf, w2_bf)


def kernel(x, c, ctx, c_ctx, ada_w, ada_b, norm_mix_g, norm_mlp_g, mlp_w1, mlp_w2, ab_w_in, ab_conv_w,
           ab_w_out, na_w_qkv, na_q_g, na_k_g, na_rpb, na_w_out):
    bsz = x.shape[0]
    ctx_row = bsz
    cond8 = jnp.zeros((8, D), F32).at[:bsz].set(c).at[ctx_row].set(c_ctx)
    mods, w_in = _modulation(cond8, ada_w, ada_b, casts=((ab_w_in, 0),))

    g_mix = norm_mix_g.reshape(DEPTH, 1, D)
    g_mlp = norm_mlp_g.reshape(DEPTH, 1, D)
    conv_w8 = jnp.zeros((8, CONV_CH), F32).at[:3].set(ab_conv_w[0])

    ay, f, w1_0, w2_0, w_out0 = _inproj(x, mods[0], g_mix[0], w_in, conv_w8, None, TM_IN,
                                        casts=((mlp_w1, 0), (mlp_w2, 0), (ab_w_out, 0)))
    fy = _dft_2d(f, _bf16_const(_channel_mix_const(SEQ)))
    x, w1_1, w2_1, wqkv, wo = _mixout_mlp(x, ay, fy, mods[0], g_mlp[0], w_out0, w1_0, w2_0, None, TM,
                                          casts=((mlp_w1, 1), (mlp_w2, 1), (na_w_qkv, 0), (na_w_out, 0)))

    ay_c, f_c = _inproj(ctx, mods[0], g_mix[0], w_in, conv_w8, ctx_row, CTX)
    fy_c = _dft_2d_small(f_c, _bf16_const(_channel_mix_const(CTX)))
    flat = lambda a: a.reshape(1, bsz * CTX, a.shape[-1])
    ctx, = _mixout_mlp(flat(ctx), flat(ay_c), flat(fy_c), mods[0], g_mlp[0], w_out0, w1_0, w2_0,
                       ctx_row, bsz * CTX)

    qcb, kcb, vcb = _qkv_grid(x, mods[1], g_mix[1], wqkv, na_q_g[0], na_k_g[0])
    kc, vc = _qkv_ctx(ctx, mods[1], g_mix[1], wqkv, na_q_g[0], na_k_g[0], ctx_row)
    kc, vc = kc.reshape(bsz, CTX, D), vc.reshape(bsz, CTX, D)
    bt, rm = _bias_tables(na_rpb[0])
    ocb = _attention(qcb, kcb, vcb, kc, vc, bt, rm)
    x = _attnout_mlp(x, ocb, mods[1], g_mlp[1], wo, w1_1, w2_1)
    return x
```

```python
import functools

import numpy as np
import jax
import jax.numpy as jnp
from jax import lax
from jax.experimental import pallas as pl
from jax.experimental.pallas import tpu as pltpu

F32 = jnp.float32
BF16 = jnp.bfloat16

D = 1024
DEPTH = 2
SEQ = 8192
CTX = 256
GRID_W = 64
GRID_ROWS = SEQ // GRID_W
HEAD_DIM = 64
N_HEADS = 16
CONV_CH = 512
FOUR_CH = 512
FOUR_GROUP = 64
WIN_ROWS = 8
WIN_COLS = 16
D_FF = 4 * D
N_MOD = 6
EPS = 1e-6

LANES = 128
MXU_TILE = 256

TM = 512
TM_IN = 1024
HALO = 8
FF_CHUNK = 1024
VMEM_LIMIT = 56 * 1024 * 1024

DFT_A = 32
DFT_B = SEQ // DFT_A
SUB = 8

QCOLS = 16
NGRP = GRID_W // QCOLS
KCOLS = 32
KC0 = tuple(int(np.clip(QCOLS * g - 8, 0, GRID_W - KCOLS)) for g in range(NGRP))
RB = 8
NRB = GRID_ROWS // RB
WROWS = 16
KROWS_PER_TILE = LANES // KCOLS
N_RM = WIN_ROWS + KROWS_PER_TILE + 1
NEG = -1e30
LOG2E = float(np.log2(np.e))
N_BT = 2 * WIN_ROWS - 1 + KROWS_PER_TILE - 1


def _cparams(n_axes):
    return pltpu.CompilerParams(dimension_semantics=("arbitrary",) * n_axes,
                                vmem_limit_bytes=VMEM_LIMIT)


def _const_spec(shape):
    nd = len(shape)
    return pl.BlockSpec(shape, lambda *_: (0,) * nd)


def _cast_jobs(jobs, n_steps, step_of):
    in_specs, out_specs, out_shapes = [], [], []
    for arr, layer in jobs:
        _, rows, cols = arr.shape
        chunk = rows // n_steps
        assert chunk * n_steps == rows and chunk % 16 == 0
        in_specs.append(pl.BlockSpec((None, chunk, cols), lambda *ids, layer=layer: (layer, step_of(*ids), 0)))
        out_specs.append(pl.BlockSpec((chunk, cols), lambda *ids: (step_of(*ids), 0)))
        out_shapes.append(jax.ShapeDtypeStruct((rows, cols), BF16))
    return in_specs, out_specs, out_shapes


def _with_casts(body, n_in, n_out, n_jobs):
    def wrapped(*refs):
        ins, refs = refs[:n_in], refs[n_in:]
        cast_ins, refs = refs[:n_jobs], refs[n_jobs:]
        outs, refs = refs[:n_out], refs[n_out:]
        cast_outs, scratch = refs[:n_jobs], refs[n_jobs:]
        for src, dst in zip(cast_ins, cast_outs):
            dst[...] = src[...].astype(BF16)
        body(*ins, *outs, *scratch)
    return wrapped


def _bf16_const(a):
    return jnp.asarray(a, F32).astype(BF16)


def _rms_mod(x, g, shift, scale):
    ms = jnp.mean(x * x, axis=-1, keepdims=True)
    y = x * lax.rsqrt(ms + EPS)
    return (y * g) * (1.0 + scale) + shift


def _mod_rows(mod_ref, row, ks):
    return [mod_ref[pl.ds(row, 1), k * D:(k + 1) * D] for k in ks]


def _mlp(x1, g, shift, scale, w1_ref, w2_ref):
    h = _rms_mod(x1, g, shift, scale).astype(BF16)
    acc = jnp.zeros(x1.shape, F32)
    for c in range(D_FF // FF_CHUNK):
        a = jnp.dot(h, w1_ref[:, c * FF_CHUNK:(c + 1) * FF_CHUNK], preferred_element_type=F32)
        a = jnp.maximum(a, 0.0)
        a = (a * a).astype(BF16)
        acc = acc + jnp.dot(a, w2_ref[c * FF_CHUNK:(c + 1) * FF_CHUNK, :], preferred_element_type=F32)
    return acc


def _mod_body(cond_ref, w_ref, b_ref, o_ref):
    c = cond_ref[...]
    s = c * jax.nn.sigmoid(c)
    o_ref[...] = jnp.dot(s.astype(BF16), w_ref[...].astype(BF16), preferred_element_type=F32) + b_ref[...]


MOD_TILE = 768


def _modulation(cond8, ada_w, ada_b, casts=()):
    nt = N_MOD * D // MOD_TILE
    c_in, c_out, c_shape = _cast_jobs(casts, DEPTH * nt, lambda l, n: l * nt + n)
    return pl.pallas_call(
        _with_casts(_mod_body, 3, 1, len(casts)),
        grid=(DEPTH, nt),
        in_specs=[_const_spec((8, D)),
                  pl.BlockSpec((None, D, MOD_TILE), lambda l, n: (l, 0, n)),
                  pl.BlockSpec((None, 1, MOD_TILE), lambda l, n: (l, 0, n))] + c_in,
        out_specs=[pl.BlockSpec((None, 8, MOD_TILE), lambda l, n: (l, 0, n))] + c_out,
        out_shape=[jax.ShapeDtypeStruct((DEPTH, 8, N_MOD * D), F32)] + c_shape,
        compiler_params=_cparams(2),
        name="adaln_mod",
    )(cond8, ada_w, ada_b.reshape(DEPTH, 1, N_MOD * D), *[a for a, _ in casts])


def _inproj_body(mod_row, seq_len, tm, xp_ref, x_ref, xn_ref, mod_ref, g_ref, w_ref, cw_ref,
                 ay_ref, f_ref):
    b = pl.program_id(0)
    t = pl.program_id(1)
    row = b if mod_row is None else mod_row
    xx = jnp.concatenate([xp_ref[...], x_ref[...], xn_ref[...]], axis=0)
    shift, scale = _mod_rows(mod_ref, row, (0, 1))
    h = _rms_mod(xx, g_ref[...], shift, scale).astype(BF16)
    u = jnp.dot(h, w_ref[...], preferred_element_type=F32)
    z = u[:, CONV_CH:2 * CONV_CH] * u[:, 0:CONV_CH]
    n = t * tm - HALO + lax.broadcasted_iota(jnp.int32, (tm + 2 * HALO, 1), 0)
    z = jnp.where((n >= 0) & (n < seq_len), z, 0.0)
    cw = cw_ref[...]
    zc = (z[HALO - 1:HALO - 1 + tm] * cw[0:1] + z[HALO:HALO + tm] * cw[1:2]
          + z[HALO + 1:HALO + 1 + tm] * cw[2:3])
    ay_ref[...] = (u[HALO:HALO + tm, 2 * CONV_CH:3 * CONV_CH] * zc).astype(BF16)
    f_ref[...] = u[HALO:HALO + tm, 3 * CONV_CH:]


def _inproj(x, mod, g, w_in_bf, conv_w8, mod_row, tm, casts=()):
    bsz, seq_len, _ = x.shape
    nt = seq_len // tm
    nb8 = seq_len // HALO
    r8 = tm // HALO
    c_in, c_out, c_shape = _cast_jobs(casts, bsz * nt, lambda b, t: b * nt + t)
    body = _with_casts(functools.partial(_inproj_body, mod_row, seq_len, tm), 7, 2, len(casts))
    return pl.pallas_call(
        body,
        grid=(bsz, nt),
        in_specs=[pl.BlockSpec((None, HALO, D), lambda b, t: (b, jnp.maximum(t * r8 - 1, 0), 0)),
                  pl.BlockSpec((None, tm, D), lambda b, t: (b, t, 0)),
                  pl.BlockSpec((None, HALO, D), lambda b, t: (b, jnp.minimum((t + 1) * r8, nb8 - 1), 0)),
                  _const_spec((8, N_MOD * D)),
                  _const_spec((1, D)),
                  _const_spec((D, 3 * CONV_CH + FOUR_CH)),
                  _const_spec((8, CONV_CH))] + c_in,
        out_specs=[pl.BlockSpec((None, tm, CONV_CH), lambda b, t: (b, t, 0)),
                   pl.BlockSpec((None, tm, FOUR_CH), lambda b, t: (b, t, 0))] + c_out,
        out_shape=[jax.ShapeDtypeStruct((bsz, seq_len, CONV_CH), BF16),
                   jax.ShapeDtypeStruct((bsz, seq_len, FOUR_CH), F32)] + c_shape,
        compiler_params=_cparams(2),
        name="mixer_in",
    )(x, x, x, mod, g, w_in_bf, conv_w8, *[a for a, _ in casts])


def _dft_consts():
    a = np.arange(DFT_A)
    s = np.arange(SUB)
    ang = 2.0 * np.pi * np.outer(a, a) / DFT_A
    eye = np.eye(SUB)
    re = np.einsum("va,ts->vtas", np.cos(ang), eye).reshape(DFT_A * SUB, DFT_A * SUB)
    im = np.einsum("va,ts->vtas", -np.sin(ang), eye).reshape(DFT_A * SUB, DFT_A * SUB)
    l1 = np.concatenate([re, im], axis=0)
    m = np.arange(DFT_B // SUB)
    bb = (SUB * m[:, None, None] + s[None, None, :])
    tang = 2.0 * np.pi * a[None, :, None] * bb / SEQ
    twr = np.cos(tang).reshape(DFT_B // SUB, DFT_A * SUB, 1)
    twi = (-np.sin(tang)).reshape(DFT_B // SUB, DFT_A * SUB, 1)
    u = np.arange(DFT_B)
    ang2 = 2.0 * np.pi * np.outer(u, u) / DFT_B
    c2, s2 = np.cos(ang2), np.sin(ang2)
    l2 = np.block([[c2, s2], [s2, -c2]])
    return l1, twr, twi, l2


DFT_MPAIR = 4
DFT_VB = 4
DFT_NA = DFT_B // SUB // DFT_MPAIR
DFT_NB = DFT_A // DFT_VB


def _channel_dft(pq, n, mix_ref):
    w = MXU_TILE
    outs = []
    for cb in range(FOUR_CH // w):
        cols = slice(cb * w, (cb + 1) * w)
        lhs = jnp.concatenate([pq[:n, cols], pq[n:, cols]], axis=1).astype(BF16)
        rhs = jnp.concatenate([mix_ref[cols, cols],
                               mix_ref[FOUR_CH + cb * w:FOUR_CH + (cb + 1) * w, cols]], axis=0)
        outs.append(jnp.dot(lhs, rhs, preferred_element_type=F32))
    return jnp.concatenate(outs, axis=1).astype(BF16)


def _dft_body(x_ref, l1_ref, twr_ref, twi_ref, l2_ref, mix_ref, fy_ref, br_scr, bi_scr):
    t = pl.program_id(1)
    rows = DFT_A * SUB
    reps = FOUR_CH // LANES

    @pl.when(t < DFT_NA)
    def _stage_a():
        x = x_ref[...]
        brs, bis = [], []
        for j in range(DFT_MPAIR):
            xj = x[:, j * SUB:(j + 1) * SUB, :].reshape(rows, FOUR_CH).astype(BF16)
            a = jnp.dot(l1_ref[...], xj, preferred_element_type=F32)
            ar, ai = a[:rows], a[rows:]
            twr = jnp.concatenate([twr_ref[j]] * reps, axis=1)
            twi = jnp.concatenate([twi_ref[j]] * reps, axis=1)
            brs.append((ar * twr - ai * twi).reshape(DFT_A, SUB, FOUR_CH))
            bis.append((ar * twi + ai * twr).reshape(DFT_A, SUB, FOUR_CH))
        shape = (DFT_A, 1, DFT_MPAIR * SUB, FOUR_CH)
        br_scr[:, pl.ds(t, 1)] = jnp.concatenate(brs, axis=1).astype(BF16).reshape(shape)
        bi_scr[:, pl.ds(t, 1)] = jnp.concatenate(bis, axis=1).astype(BF16).reshape(shape)

    @pl.when(t >= DFT_NA)
    def _stage_b():
        v0 = (t - DFT_NA) * DFT_VB
        for j in range(DFT_VB):
            br = br_scr[v0 + j].reshape(DFT_B, FOUR_CH)
            bi = bi_scr[v0 + j].reshape(DFT_B, FOUR_CH)
            pq = jnp.dot(l2_ref[...], jnp.concatenate([br, bi], axis=0),
                         preferred_element_type=F32)
            fy_ref[j] = _channel_dft(pq, DFT_B, mix_ref)


def _dft_2d(f, mix_bf):
    bsz = f.shape[0]
    l1, twr, twi, l2 = _dft_consts()
    nm = DFT_B // SUB
    rows = DFT_A * SUB
    twr_b = jnp.asarray(np.broadcast_to(twr, (nm, rows, LANES)), F32)
    twi_b = jnp.asarray(np.broadcast_to(twi, (nm, rows, LANES)), F32)
    f5 = f.reshape(bsz, DFT_A, DFT_NA, DFT_MPAIR * SUB, FOUR_CH)
    a_step = lambda t: jnp.minimum(t, DFT_NA - 1)
    b_step = lambda t: jnp.maximum(t - DFT_NA, 0)
    tw_spec = pl.BlockSpec((DFT_MPAIR, rows, LANES), lambda b, t: (a_step(t), 0, 0))
    scr = pltpu.VMEM((DFT_A, DFT_NA, DFT_MPAIR * SUB, FOUR_CH), BF16)
    fy = pl.pallas_call(
        _dft_body,
        grid=(bsz, DFT_NA + DFT_NB),
        in_specs=[pl.BlockSpec((None, DFT_A, None, DFT_MPAIR * SUB, FOUR_CH),
                               lambda b, t: (b, 0, a_step(t), 0, 0)),
                  _const_spec((2 * rows, rows)), tw_spec, tw_spec,
                  _const_spec((2 * DFT_B, 2 * DFT_B)), _const_spec((2 * FOUR_CH, FOUR_CH))],
        out_specs=pl.BlockSpec((None, DFT_VB, DFT_B, FOUR_CH), lambda b, t: (b, b_step(t), 0, 0)),
        out_shape=jax.ShapeDtypeStruct((bsz, DFT_A, DFT_B, FOUR_CH), BF16),
        scratch_shapes=[scr, scr],
        compiler_params=_cparams(2),
        name="dft_2d",
    )(f5, _bf16_const(l1), twr_b, twi_b, _bf16_const(l2), mix_bf)
    return jnp.transpose(fy, (0, 2, 1, 3)).reshape(bsz, SEQ, FOUR_CH)


def _dft_small_body(f_ref, lc_ref, mix_ref, fy_ref):
    n = f_ref.shape[0]
    pq = jnp.dot(lc_ref[...], f_ref[...].astype(BF16), preferred_element_type=F32)
    fy_ref[...] = _channel_dft(pq, n, mix_ref)


def _dft_2d_small(f, mix_bf):
    bsz, n, _ = f.shape
    k = np.arange(n)
    ang = 2.0 * np.pi * np.outer(k, k) / n
    lc = np.concatenate([np.cos(ang), np.sin(ang)], axis=0)
    spec = pl.BlockSpec((None, n, FOUR_CH), lambda b: (b, 0, 0))
    return pl.pallas_call(
        _dft_small_body,
        grid=(bsz,),
        in_specs=[spec, _const_spec((2 * n, n)), _const_spec((2 * FOUR_CH, FOUR_CH))],
        out_specs=spec,
        out_shape=jax.ShapeDtypeStruct((bsz, n, FOUR_CH), BF16),
        compiler_params=_cparams(1),
        name="dft_small",
    )(f, _bf16_const(lc), mix_bf)


def _channel_mix_const(seq_len):
    k = np.arange(FOUR_GROUP)
    ang = 2.0 * np.pi * np.outer(k, k) / FOUR_GROUP
    ng = FOUR_CH // FOUR_GROUP
    bdc = np.kron(np.eye(ng), np.cos(ang))
    bds = np.kron(np.eye(ng), np.sin(ang))
    return np.concatenate([bdc, -bds], axis=0) / np.sqrt(seq_len * FOUR_GROUP)


def _mixout_mlp_body(mod_row, x_ref, ay_ref, fy_ref, mod_ref, g2_ref, wout_ref, w1_ref, w2_ref, o_ref):
    b = pl.program_id(0)
    row = b if mod_row is None else mod_row
    gate1, shift2, scale2, gate2 = _mod_rows(mod_ref, row, (2, 3, 4, 5))
    cat = jnp.concatenate([ay_ref[...], fy_ref[...]], axis=1)
    mo = jnp.dot(cat, wout_ref[...], preferred_element_type=F32)
    x1 = x_ref[...] + gate1 * mo
    o_ref[...] = x1 + gate2 * _mlp(x1, g2_ref[...], shift2, scale2, w1_ref, w2_ref)


def _mixout_mlp(x, ay, fy, mod, g2, wout_bf, w1_bf, w2_bf, mod_row, tm, casts=()):
    bsz, seq_len, _ = x.shape
    nt = seq_len // tm
    row_spec = lambda w: pl.BlockSpec((None, tm, w), lambda b, t: (b, t, 0))
    c_in, c_out, c_shape = _cast_jobs(casts, bsz * nt, lambda b, t: b * nt + t)
    body = _with_casts(functools.partial(_mixout_mlp_body, mod_row), 8, 1, len(casts))
    return pl.pallas_call(
        body,
        grid=(bsz, nt),
        in_specs=[row_spec(D), row_spec(CONV_CH), row_spec(FOUR_CH),
                  _const_spec((8, N_MOD * D)), _const_spec((1, D)),
                  _const_spec((D, D)), _const_spec((D, D_FF)), _const_spec((D_FF, D))] + c_in,
        out_specs=[row_spec(D)] + c_out,
        out_shape=[jax.ShapeDtypeStruct((bsz, seq_len, D), F32)] + c_shape,
        compiler_params=_cparams(2),
        name="mixer_out_mlp",
    )(x, ay, fy, mod, g2, wout_bf, w1_bf, w2_bf, *[a for a, _ in casts])


def _head_rms(t, bd_ref, gt):
    tt = (t * t).astype(BF16)
    w = bd_ref.shape[0]
    ms = jnp.concatenate(
        [jnp.dot(tt[:, j * w:(j + 1) * w], bd_ref[...], preferred_element_type=F32)
         for j in range(D // w)], axis=1)
    return t * lax.rsqrt(ms + EPS) * gt


def _qkv_common(mod_row, x_ref, mod_ref, g_ref, w_ref, bd_ref, qg_ref, kg_ref):
    b = pl.program_id(0)
    row = b if mod_row is None else mod_row
    shift, scale = _mod_rows(mod_ref, row, (0, 1))
    h = _rms_mod(x_ref[...], g_ref[...], shift, scale).astype(BF16)
    qkv = jnp.dot(h, w_ref[...], preferred_element_type=F32)
    q = _head_rms(qkv[:, 0:D], bd_ref, qg_ref[...]) * (HEAD_DIM ** -0.5 * LOG2E)
    k = _head_rms(qkv[:, D:2 * D], bd_ref, kg_ref[...])
    v = qkv[:, 2 * D:]
    return q, k, v


def _qkv_grid_body(x_ref, mod_ref, g_ref, w_ref, bd_ref, qg_ref, kg_ref, q_ref, k_ref, v_ref):
    q, k, v = _qkv_common(None, x_ref, mod_ref, g_ref, w_ref, bd_ref, qg_ref, kg_ref)
    for rho in range(TM // GRID_W):
        for g in range(NGRP):
            q0 = rho * GRID_W + QCOLS * g
            q_ref[g, rho * QCOLS:(rho + 1) * QCOLS, :] = q[q0:q0 + QCOLS].astype(BF16)
            k0 = rho * GRID_W + KC0[g]
            k_ref[g, rho * KCOLS:(rho + 1) * KCOLS, :] = k[k0:k0 + KCOLS].astype(BF16)
            v_ref[g, rho * KCOLS:(rho + 1) * KCOLS, :] = v[k0:k0 + KCOLS].astype(BF16)


def _qkv_ctx_body(mod_row, x_ref, mod_ref, g_ref, w_ref, bd_ref, qg_ref, kg_ref, k_ref, v_ref):
    _, k, v = _qkv_common(mod_row, x_ref, mod_ref, g_ref, w_ref, bd_ref, qg_ref, kg_ref)
    k_ref[...] = k.astype(BF16)
    v_ref[...] = v.astype(BF16)


def _qkv_consts(q_g, k_g):
    bd = np.kron(np.eye(MXU_TILE // HEAD_DIM), np.ones((HEAD_DIM, HEAD_DIM))) / HEAD_DIM
    qg = jnp.tile(q_g, N_HEADS).reshape(1, D)
    kg = jnp.tile(k_g, N_HEADS).reshape(1, D)
    return _bf16_const(bd), qg, kg


def _qkv_in_specs(tm):
    return [pl.BlockSpec((None, tm, D), lambda b, t: (b, t, 0)),
            _const_spec((8, N_MOD * D)), _const_spec((1, D)), _const_spec((D, 3 * D)),
            _const_spec((MXU_TILE, MXU_TILE)), _const_spec((1, D)), _const_spec((1, D))]


def _qkv_grid(x, mod, g, wqkv_bf, q_g, k_g):
    bsz = x.shape[0]
    bd, qg, kg = _qkv_consts(q_g, k_g)
    nq = TM // GRID_W * QCOLS
    nk = TM // GRID_W * KCOLS
    return pl.pallas_call(
        _qkv_grid_body,
        grid=(bsz, SEQ // TM),
        in_specs=_qkv_in_specs(TM),
        out_specs=[pl.BlockSpec((None, NGRP, nq, D), lambda b, t: (b, 0, t, 0)),
                   pl.BlockSpec((None, NGRP, nk, D), lambda b, t: (b, 0, t, 0)),
                   pl.BlockSpec((None, NGRP, nk, D), lambda b, t: (b, 0, t, 0))],
        out_shape=[jax.ShapeDtypeStruct((bsz, NGRP, GRID_ROWS * QCOLS, D), BF16),
                   jax.ShapeDtypeStruct((bsz, NGRP, GRID_ROWS * KCOLS, D), BF16),
                   jax.ShapeDtypeStruct((bsz, NGRP, GRID_ROWS * KCOLS, D), BF16)],
        compiler_params=_cparams(2),
        name="qkv_grid",
    )(x, mod, g, wqkv_bf, bd, qg, kg)


def _qkv_ctx(ctx, mod, g, wqkv_bf, q_g, k_g, mod_row):
    bsz, n, _ = ctx.shape
    bd, qg, kg = _qkv_consts(q_g, k_g)
    spec = pl.BlockSpec((None, n, D), lambda b, t: (b, t, 0))
    return pl.pallas_call(
        functools.partial(_qkv_ctx_body, mod_row),
        grid=(bsz, 1),
        in_specs=_qkv_in_specs(n),
        out_specs=[spec, spec],
        out_shape=[jax.ShapeDtypeStruct((bsz, n, D), BF16)] * 2,
        compiler_params=_cparams(2),
        name="qkv_ctx",
    )(ctx, mod, g, wqkv_bf, bd, qg, kg)


def _bias_tables(rpb):
    n_ro = 2 * WIN_ROWS - 1
    n_co = 2 * WIN_COLS - 1
    g = np.arange(NGRP)[:, None, None]
    cq = np.arange(QCOLS)[None, :, None]
    kcw = (np.arange(LANES) % KCOLS)[None, None, :]
    c = QCOLS * g + cq
    kc = np.asarray(KC0)[:, None, None] + kcw
    cs = np.clip(c - WIN_COLS // 2, 0, GRID_W - WIN_COLS)
    col_ok = (kc >= cs) & (kc < cs + WIN_COLS)
    co = np.where(col_ok, kc - c + (WIN_COLS - 1), -1)
    onehot = (np.arange(n_co)[:, None, None, None] == co[None]).astype(np.float32)
    n_pad = n_ro + 2 + KROWS_PER_TILE
    rpb_p = jnp.pad(rpb * LOG2E, ((0, 0), (3, KROWS_PER_TILE - 1), (0, 0)))
    sel = jnp.einsum("hrc,cn->hrn", rpb_p, jnp.asarray(onehot.reshape(n_co, -1)),
                     precision=lax.Precision.HIGHEST)
    sel = sel.reshape(N_HEADS, n_pad, NGRP, QCOLS, LANES).transpose(0, 2, 1, 3, 4)
    rt = jnp.where(col_ok[None, :, None], sel, NEG)
    rt = rt.reshape(N_HEADS // 2, 2, NGRP, n_pad, QCOLS, LANES)
    delta = np.arange(N_RM)[:, None, None] - WIN_ROWS
    jj2 = (np.arange(LANES) // KCOLS)[None, None, :]
    rm = np.where((jj2 >= delta) & (jj2 < delta + WIN_ROWS), 0.0, NEG)
    rm = np.broadcast_to(rm, (N_RM, QCOLS, LANES)).astype(np.float32)
    return rt.astype(F32), jnp.asarray(rm)


def _tile_indices(rb, i, jq):
    wr = int(np.clip(RB * rb - WIN_ROWS // 2, 0, GRID_ROWS - WROWS))
    r = RB * rb + i
    delta = int(np.clip(r - WIN_ROWS // 2, 0, GRID_ROWS - WIN_ROWS)) - wr - KROWS_PER_TILE * jq
    if not -WIN_ROWS < delta < KROWS_PER_TILE:
        return None
    bti = wr + KROWS_PER_TILE * jq - r + (WIN_ROWS - 1) + 3
    assert 0 <= bti < N_BT and 0 <= delta + WIN_ROWS < N_RM
    return bti, delta + WIN_ROWS


def _attn_body(q_ref, k_ref, v_ref, kc_ref, vc_ref, rt_ref, rm_ref, o_ref, s_scr, bt_scr, bm_scr):
    kc = kc_ref[...]
    vc = vc_ref[...]
    nq = RB * QCOLS
    nk = WROWS * KCOLS
    lane = lax.broadcasted_iota(jnp.int32, (nq, LANES), 1)
    nt = nk // LANES
    nc = CTX // LANES
    dn = (((1,), (1,)), ((), ()))

    @pl.when(pl.program_id(2) == 0)
    def _build_tables():
        lgrp = lax.broadcasted_iota(jnp.int32, (QCOLS, LANES), 1) // KCOLS
        for e in range(2):
            for t in range(N_BT):
                blk = rt_ref[e, t + KROWS_PER_TILE - 1]
                for jj in range(KROWS_PER_TILE - 2, -1, -1):
                    blk = jnp.where(lgrp == jj, rt_ref[e, t + jj], blk)
                bt_scr[e, t] = blk
        for e in range(2):
            for i in range(RB):
                for jq in range(nt):
                    idx = _tile_indices(1, i, jq)
                    if idx is not None:
                        bm_scr[e, i * nt + jq] = bt_scr[e, idx[0]] + rm_ref[idx[1]]

    def window_row(rb):
        return jnp.clip(RB * rb - WIN_ROWS // 2, 0, GRID_ROWS - WROWS)

    def scores(rb, slot):
        rb = jnp.asarray(rb, jnp.int32)
        koff = pl.multiple_of(window_row(rb) * KCOLS, LANES)
        kw = k_ref[pl.ds(koff, nk), :]
        q = q_ref[pl.ds(pl.multiple_of(rb * nq, nq), nq), :]
        for e in range(2):
            in_head = (lane >= HEAD_DIM * e) & (lane < HEAD_DIM * (e + 1))
            qm = jnp.where(in_head, q, jnp.zeros_like(q))
            s_scr[slot, e, :, 0:nk] = lax.dot_general(qm, kw, dn, preferred_element_type=F32)
            s_scr[slot, e, :, nk:] = lax.dot_general(qm, kc, dn, preferred_element_type=F32)

    def finish(rb, slot, edge_rb):
        rb = jnp.asarray(rb, jnp.int32)
        wr = window_row(rb)
        vw = v_ref[pl.ds(pl.multiple_of(wr * KCOLS, LANES), nk), :]
        zero_tile = jnp.zeros((QCOLS, LANES), F32)
        outs = []
        for e in range(2):
            p_rows = []
            l_rows = []
            for i in range(RB):
                rows = slice(i * QCOLS, (i + 1) * QCOLS)
                blks = {}
                for jq in range(nt):
                    idx = _tile_indices(1 if edge_rb is None else edge_rb, i, jq)
                    if idx is None:
                        continue
                    add = bm_scr[e, i * nt + jq] if edge_rb is None else bt_scr[e, idx[0]] + rm_ref[idx[1]]
                    blks[jq] = s_scr[slot, e, rows, jq * LANES:(jq + 1) * LANES] + add
                for jc in range(nc):
                    blks[nt + jc] = s_scr[slot, e, rows, nk + jc * LANES:nk + (jc + 1) * LANES]
                vals = list(blks.values())
                m = jnp.max(functools.reduce(jnp.maximum, vals), axis=-1, keepdims=True)
                ps = {j: jnp.exp2(sb - m) for j, sb in blks.items()}
                l_rows.append(jnp.sum(functools.reduce(jnp.add, list(ps.values())), axis=-1, keepdims=True))
                p_rows.append(jnp.concatenate([ps.get(j, zero_tile) for j in range(nt + nc)], axis=1))
            p = jnp.concatenate(p_rows, axis=0).astype(BF16)
            lsum = jnp.concatenate(l_rows, axis=0)
            o = (jnp.dot(p[:, :nk], vw, preferred_element_type=F32)
                 + jnp.dot(p[:, nk:], vc, preferred_element_type=F32))
            outs.append(o / lsum)
        o = jnp.where(lane < HEAD_DIM, outs[0], outs[1])
        o_ref[pl.ds(pl.multiple_of(rb * nq, nq), nq), :] = o.astype(BF16)

    scores(0, 0)
    scores(1, 1)
    finish(0, 0, 0)
    scores(2, 0)
    finish(1, 1, None)

    def pair(tt, carry):
        t0 = 2 * tt
        scores(t0 + 1, 1)
        finish(t0, 0, None)
        scores(t0 + 2, 0)
        finish(t0 + 1, 1, None)
        return carry

    lax.fori_loop(1, NRB // 2 - 1, pair, 0, unroll=True)
    scores(NRB - 1, 1)
    finish(NRB - 2, 0, None)
    finish(NRB - 1, 1, NRB - 1)


def _attention(qcb, kcb, vcb, kc, vc, bt, rm):
    bsz = qcb.shape[0]
    nhp = N_HEADS // 2
    n_t = bt.shape[3]
    return pl.pallas_call(
        _attn_body,
        grid=(nhp, NGRP, bsz),
        in_specs=[pl.BlockSpec((None, None, GRID_ROWS * QCOLS, LANES), lambda h, g, b: (b, g, 0, h)),
                  pl.BlockSpec((None, None, GRID_ROWS * KCOLS, LANES), lambda h, g, b: (b, g, 0, h)),
                  pl.BlockSpec((None, None, GRID_ROWS * KCOLS, LANES), lambda h, g, b: (b, g, 0, h)),
                  pl.BlockSpec((None, CTX, LANES), lambda h, g, b: (b, 0, h)),
                  pl.BlockSpec((None, CTX, LANES), lambda h, g, b: (b, 0, h)),
                  pl.BlockSpec((None, 2, None, n_t, QCOLS, LANES), lambda h, g, b: (h, 0, g, 0, 0, 0)),
                  _const_spec((N_RM, QCOLS, LANES))],
        out_specs=pl.BlockSpec((None, None, GRID_ROWS * QCOLS, LANES), lambda h, g, b: (b, g, 0, h)),
        out_shape=jax.ShapeDtypeStruct((bsz, NGRP, GRID_ROWS * QCOLS, D), BF16),
        scratch_shapes=[pltpu.VMEM((2, 2, RB * QCOLS, WROWS * KCOLS + CTX), F32),
                        pltpu.VMEM((2, N_BT, QCOLS, LANES), F32),
                        pltpu.VMEM((2, RB * WROWS * KCOLS // LANES, QCOLS, LANES), F32)],
        compiler_params=_cparams(3),
        name="nbr_attention",
    )(qcb, kcb, vcb, kc, vc, bt, rm)


def _attnout_mlp_body(x_ref, o_ref_in, mod_ref, g2_ref, wo_ref, w1_ref, w2_ref, out_ref):
    b = pl.program_id(0)
    gate1, shift2, scale2, gate2 = _mod_rows(mod_ref, b, (2, 3, 4, 5))
    chunks = [o_ref_in[g, rho * QCOLS:(rho + 1) * QCOLS, :]
              for rho in range(TM // GRID_W) for g in range(NGRP)]
    o_nat = jnp.concatenate(chunks, axis=0)
    mo = jnp.dot(o_nat, wo_ref[...], preferred_element_type=F32)
    x1 = x_ref[...] + gate1 * mo
    out_ref[...] = x1 + gate2 * _mlp(x1, g2_ref[...], shift2, scale2, w1_ref, w2_ref)


def _attnout_mlp(x, ocb, mod, g2, wo_bf, w1_bf, w2_bf):
    bsz = x.shape[0]
    nq = TM // GRID_W * QCOLS
    return pl.pallas_call(
        _attnout_mlp_body,
        grid=(bsz, SEQ // TM),
        in_specs=[pl.BlockSpec((None, TM, D), lambda b, t: (b, t, 0)),
                  pl.BlockSpec((None, NGRP, nq, D), lambda b, t: (b, 0, t, 0)),
                  _const_spec((8, N_MOD * D)), _const_spec((1, D)),
                  _const_spec((D, D)), _const_spec((D, D_FF)), _const_spec((D_FF, D))],
        out_specs=pl.BlockSpec((None, TM, D), lambda b, t: (b, t, 0)),
        out_shape=jax.ShapeDtypeStruct((bsz, SEQ, D), F32),
        compiler_params=_cparams(2),
        name="attn_out_mlp",
    )(x, ocb, mod, g2, wo_bf, w1_bf, w2_bf)


def kernel(x, c, ctx, c_ctx, ada_w, ada_b, norm_mix_g, norm_mlp_g, mlp_w1, mlp_w2, ab_w_in, ab_conv_w,
           ab_w_out, na_w_qkv, na_q_g, na_k_g, na_rpb, na_w_out):
    bsz = x.shape[0]
    ctx_row = bsz
    cond8 = jnp.zeros((8, D), F32).at[:bsz].set(c).at[ctx_row].set(c_ctx)
    mods, w_in = _modulation(cond8, ada_w, ada_b, casts=((ab_w_in, 0),))

    g_mix = norm_mix_g.reshape(DEPTH, 1, D)
    g_mlp = norm_mlp_g.reshape(DEPTH, 1, D)
    conv_w8 = jnp.zeros((8, CONV_CH), F32).at[:3].set(ab_conv_w[0])

    ay, f, w1_0, w2_0, w_out0 = _inproj(x, mods[0], g_mix[0], w_in, conv_w8, None, TM_IN,
                                        casts=((mlp_w1, 0), (mlp_w2, 0), (ab_w_out, 0)))
    fy = _dft_2d(f, _bf16_const(_channel_mix_const(SEQ)))
    x, w1_1, w2_1, wqkv, wo = _mixout_mlp(x, ay, fy, mods[0], g_mlp[0], w_out0, w1_0, w2_0, None, TM,
                                          casts=((mlp_w1, 1), (mlp_w2, 1), (na_w_qkv, 0), (na_w_out, 0)))

    ay_c, f_c = _inproj(ctx, mods[0], g_mix[0], w_in, conv_w8, ctx_row, CTX)
    fy_c = _dft_2d_small(f_c, _bf16_const(_channel_mix_const(CTX)))
    flat = lambda a: a.reshape(1, bsz * CTX, a.shape[-1])
    ctx, = _mixout_mlp(flat(ctx), flat(ay_c), flat(fy_c), mods[0], g_mlp[0], w_out0, w1_0, w2_0,
                       ctx_row, bsz * CTX)

    qcb, kcb, vcb = _qkv_grid(x, mods[1], g_mix[1], wqkv, na_q_g[0], na_k_g[0])
    kc, vc = _qkv_ctx(ctx, mods[1], g_mix[1], wqkv, na_q_g[0], na_k_g[0], ctx_row)
    kc, vc = kc.reshape(bsz, CTX, D), vc.reshape(bsz, CTX, D)
    bt, rm = _bias_tables(na_rpb[0])
    ocb = _attention(qcb, kcb, vcb, kc, vc, bt, rm)
    x = _attnout_mlp(x, ocb, mods[1], g_mlp[1], wo, w1_1, w2_1)
    return x
```

```python
import functools

import numpy as np
import jax
import jax.numpy as jnp
from jax import lax
from jax.experimental import pallas as pl
from jax.experimental.pallas import tpu as pltpu

F32 = jnp.float32
BF16 = jnp.bfloat16

D = 1024
DEPTH = 2
SEQ = 8192
CTX = 256
GRID_W = 64
GRID_ROWS = SEQ // GRID_W
HEAD_DIM = 64
N_HEADS = 16
CONV_CH = 512
FOUR_CH = 512
FOUR_GROUP = 64
WIN_ROWS = 8
WIN_COLS = 16
D_FF = 4 * D
N_MOD = 6
EPS = 1e-6

LANES = 128
MXU_TILE = 256

TM = 512
TM_IN = 1024
TM_MLP = 1024
HALO = 8
FF_CHUNK = 1024
VMEM_LIMIT = 56 * 1024 * 1024

DFT_A = 32
DFT_B = SEQ // DFT_A
SUB = 8

QCOLS = 16
NGRP = GRID_W // QCOLS
KCOLS = 32
KC0 = tuple(int(np.clip(QCOLS * g - 8, 0, GRID_W - KCOLS)) for g in range(NGRP))
RB = 8
NRB = GRID_ROWS // RB
WROWS = 16
KROWS_PER_TILE = LANES // KCOLS
N_RM = WIN_ROWS + KROWS_PER_TILE + 1
NEG = -1e30
LOG2E = float(np.log2(np.e))
N_BT = 2 * WIN_ROWS - 1 + KROWS_PER_TILE - 1


def _cparams(n_axes):
    return pltpu.CompilerParams(dimension_semantics=("arbitrary",) * n_axes,
                                vmem_limit_bytes=VMEM_LIMIT)


def _const_spec(shape):
    nd = len(shape)
    return pl.BlockSpec(shape, lambda *_: (0,) * nd)


def _resident_spec(shape):
    nd = len(shape)
    return pl.BlockSpec(shape, lambda *_: (0,) * nd, pipeline_mode=pl.Buffered(1))


def _cast_jobs(jobs, n_steps, step_of):
    in_specs, out_specs, out_shapes = [], [], []
    for arr, layer in jobs:
        _, rows, cols = arr.shape
        chunk = rows // n_steps
        assert chunk * n_steps == rows and chunk % 16 == 0
        in_specs.append(pl.BlockSpec((None, chunk, cols), lambda *ids, layer=layer: (layer, step_of(*ids), 0)))
        out_specs.append(pl.BlockSpec((chunk, cols), lambda *ids: (step_of(*ids), 0)))
        out_shapes.append(jax.ShapeDtypeStruct((rows, cols), BF16))
    return in_specs, out_specs, out_shapes


def _with_casts(body, n_in, n_out, n_jobs):
    def wrapped(*refs):
        ins, refs = refs[:n_in], refs[n_in:]
        cast_ins, refs = refs[:n_jobs], refs[n_jobs:]
        outs, refs = refs[:n_out], refs[n_out:]
        cast_outs, scratch = refs[:n_jobs], refs[n_jobs:]
        for src, dst in zip(cast_ins, cast_outs):
            dst[...] = src[...].astype(BF16)
        body(*ins, *outs, *scratch)
    return wrapped


def _bf16_const(a):
    return jnp.asarray(a, F32).astype(BF16)


def _rms_mod(x, g, shift, scale):
    ms = jnp.mean(x * x, axis=-1, keepdims=True)
    y = x * lax.rsqrt(ms + EPS)
    return (y * g) * (1.0 + scale) + shift


def _mod_rows(mod_ref, row, ks):
    return [mod_ref[pl.ds(row, 1), k * D:(k + 1) * D] for k in ks]


def _mlp(x1, g, shift, scale, w1_ref, w2_ref):
    h = _rms_mod(x1, g, shift, scale).astype(BF16)
    acc = jnp.zeros(x1.shape, F32)
    for c in range(D_FF // FF_CHUNK):
        a = jnp.dot(h, w1_ref[:, c * FF_CHUNK:(c + 1) * FF_CHUNK], preferred_element_type=F32)
        a = jnp.maximum(a, 0.0)
        a = (a * a).astype(BF16)
        acc = acc + jnp.dot(a, w2_ref[c * FF_CHUNK:(c + 1) * FF_CHUNK, :], preferred_element_type=F32)
    return acc


def _mod_body(cond_ref, w_ref, b_ref, o_ref):
    c = cond_ref[...]
    s = c * jax.nn.sigmoid(c)
    o_ref[...] = jnp.dot(s.astype(BF16), w_ref[...].astype(BF16), preferred_element_type=F32) + b_ref[...]


MOD_TILE = 768


def _modulation(cond8, ada_w, ada_b, casts=()):
    nt = N_MOD * D // MOD_TILE
    c_in, c_out, c_shape = _cast_jobs(casts, DEPTH * nt, lambda l, n: l * nt + n)
    return pl.pallas_call(
        _with_casts(_mod_body, 3, 1, len(casts)),
        grid=(DEPTH, nt),
        in_specs=[_const_spec((8, D)),
                  pl.BlockSpec((None, D, MOD_TILE), lambda l, n: (l, 0, n)),
                  pl.BlockSpec((None, 1, MOD_TILE), lambda l, n: (l, 0, n))] + c_in,
        out_specs=[pl.BlockSpec((None, 8, MOD_TILE), lambda l, n: (l, 0, n))] + c_out,
        out_shape=[jax.ShapeDtypeStruct((DEPTH, 8, N_MOD * D), F32)] + c_shape,
        compiler_params=_cparams(2),
        name="adaln_mod",
    )(cond8, ada_w, ada_b.reshape(DEPTH, 1, N_MOD * D), *[a for a, _ in casts])


def _inproj_body(mod_row, seq_len, tm, xp_ref, x_ref, xn_ref, mod_ref, g_ref, w_ref, cw_ref,
                 ay_ref, f_ref):
    b = pl.program_id(0)
    t = pl.program_id(1)
    row = b if mod_row is None else mod_row
    xx = jnp.concatenate([xp_ref[...], x_ref[...], xn_ref[...]], axis=0)
    shift, scale = _mod_rows(mod_ref, row, (0, 1))
    h = _rms_mod(xx, g_ref[...], shift, scale).astype(BF16)
    u = jnp.dot(h, w_ref[...], preferred_element_type=F32)
    z = u[:, CONV_CH:2 * CONV_CH] * u[:, 0:CONV_CH]
    n = t * tm - HALO + lax.broadcasted_iota(jnp.int32, (tm + 2 * HALO, 1), 0)
    z = jnp.where((n >= 0) & (n < seq_len), z, 0.0)
    cw = cw_ref[...]
    zc = (z[HALO - 1:HALO - 1 + tm] * cw[0:1] + z[HALO:HALO + tm] * cw[1:2]
          + z[HALO + 1:HALO + 1 + tm] * cw[2:3])
    ay_ref[...] = (u[HALO:HALO + tm, 2 * CONV_CH:3 * CONV_CH] * zc).astype(BF16)
    f_ref[...] = u[HALO:HALO + tm, 3 * CONV_CH:]


def _inproj(x, mod, g, w_in_bf, conv_w8, mod_row, tm, casts=()):
    bsz, seq_len, _ = x.shape
    nt = seq_len // tm
    nb8 = seq_len // HALO
    r8 = tm // HALO
    c_in, c_out, c_shape = _cast_jobs(casts, bsz * nt, lambda b, t: b * nt + t)
    body = _with_casts(functools.partial(_inproj_body, mod_row, seq_len, tm), 7, 2, len(casts))
    return pl.pallas_call(
        body,
        grid=(bsz, nt),
        in_specs=[pl.BlockSpec((None, HALO, D), lambda b, t: (b, jnp.maximum(t * r8 - 1, 0), 0)),
                  pl.BlockSpec((None, tm, D), lambda b, t: (b, t, 0)),
                  pl.BlockSpec((None, HALO, D), lambda b, t: (b, jnp.minimum((t + 1) * r8, nb8 - 1), 0)),
                  _const_spec((8, N_MOD * D)),
                  _const_spec((1, D)),
                  _const_spec((D, 3 * CONV_CH + FOUR_CH)),
                  _const_spec((8, CONV_CH))] + c_in,
        out_specs=[pl.BlockSpec((None, tm, CONV_CH), lambda b, t: (b, t, 0)),
                   pl.BlockSpec((None, tm, FOUR_CH), lambda b, t: (b, t, 0))] + c_out,
        out_shape=[jax.ShapeDtypeStruct((bsz, seq_len, CONV_CH), BF16),
                   jax.ShapeDtypeStruct((bsz, seq_len, FOUR_CH), F32)] + c_shape,
        compiler_params=_cparams(2),
        name="mixer_in",
    )(x, x, x, mod, g, w_in_bf, conv_w8, *[a for a, _ in casts])


def _dft_consts():
    a = np.arange(DFT_A)
    s = np.arange(SUB)
    ang = 2.0 * np.pi * np.outer(a, a) / DFT_A
    eye = np.eye(SUB)
    re = np.einsum("va,ts->vtas", np.cos(ang), eye).reshape(DFT_A * SUB, DFT_A * SUB)
    im = np.einsum("va,ts->vtas", -np.sin(ang), eye).reshape(DFT_A * SUB, DFT_A * SUB)
    l1 = np.concatenate([re, im], axis=0)
    m = np.arange(DFT_B // SUB)
    bb = (SUB * m[:, None, None] + s[None, None, :])
    tang = 2.0 * np.pi * a[None, :, None] * bb / SEQ
    twr = np.cos(tang).reshape(DFT_B // SUB, DFT_A * SUB, 1)
    twi = (-np.sin(tang)).reshape(DFT_B // SUB, DFT_A * SUB, 1)
    u = np.arange(DFT_B)
    ang2 = 2.0 * np.pi * np.outer(u, u) / DFT_B
    c2, s2 = np.cos(ang2), np.sin(ang2)
    l2 = np.block([[c2, s2], [s2, -c2]])
    return l1, twr, twi, l2


DFT_MPAIR = 4
DFT_VB = 4
DFT_NA = DFT_B // SUB // DFT_MPAIR
DFT_NB = DFT_A // DFT_VB


def _channel_dft(pq, n, mix_ref):
    w = MXU_TILE
    outs = []
    for cb in range(FOUR_CH // w):
        cols = slice(cb * w, (cb + 1) * w)
        lhs = jnp.concatenate([pq[:n, cols], pq[n:, cols]], axis=1).astype(BF16)
        rhs = jnp.concatenate([mix_ref[cols, cols],
                               mix_ref[FOUR_CH + cb * w:FOUR_CH + (cb + 1) * w, cols]], axis=0)
        outs.append(jnp.dot(lhs, rhs, preferred_element_type=F32))
    return jnp.concatenate(outs, axis=1).astype(BF16)


def _dft_body(x_ref, l1_ref, twr_ref, twi_ref, l2_ref, mix_ref, fy_ref, br_scr, bi_scr):
    t = pl.program_id(1)
    rows = DFT_A * SUB
    reps = FOUR_CH // LANES

    @pl.when(t < DFT_NA)
    def _stage_a():
        x = x_ref[...]
        brs, bis = [], []
        for j in range(DFT_MPAIR):
            xj = x[:, j * SUB:(j + 1) * SUB, :].reshape(rows, FOUR_CH).astype(BF16)
            a = jnp.dot(l1_ref[...], xj, preferred_element_type=F32)
            ar, ai = a[:rows], a[rows:]
            twr = jnp.concatenate([twr_ref[j]] * reps, axis=1)
            twi = jnp.concatenate([twi_ref[j]] * reps, axis=1)
            brs.append((ar * twr - ai * twi).reshape(DFT_A, SUB, FOUR_CH))
            bis.append((ar * twi + ai * twr).reshape(DFT_A, SUB, FOUR_CH))
        shape = (DFT_A, 1, DFT_MPAIR * SUB, FOUR_CH)
        br_scr[:, pl.ds(t, 1)] = jnp.concatenate(brs, axis=1).astype(BF16).reshape(shape)
        bi_scr[:, pl.ds(t, 1)] = jnp.concatenate(bis, axis=1).astype(BF16).reshape(shape)

    @pl.when(t >= DFT_NA)
    def _stage_b():
        v0 = (t - DFT_NA) * DFT_VB
        for j in range(DFT_VB):
            br = br_scr[v0 + j].reshape(DFT_B, FOUR_CH)
            bi = bi_scr[v0 + j].reshape(DFT_B, FOUR_CH)
            pq = jnp.dot(l2_ref[...], jnp.concatenate([br, bi], axis=0),
                         preferred_element_type=F32)
            fy_ref[j] = _channel_dft(pq, DFT_B, mix_ref)


def _dft_2d(f, mix_bf, casts=()):
    bsz = f.shape[0]
    n_steps = DFT_NA + DFT_NB
    c_in, c_out, c_shape = _cast_jobs(casts, bsz * n_steps, lambda b, t: b * n_steps + t)
    l1, twr, twi, l2 = _dft_consts()
    nm = DFT_B // SUB
    rows = DFT_A * SUB
    twr_b = jnp.asarray(np.broadcast_to(twr, (nm, rows, LANES)), F32)
    twi_b = jnp.asarray(np.broadcast_to(twi, (nm, rows, LANES)), F32)
    f5 = f.reshape(bsz, DFT_A, DFT_NA, DFT_MPAIR * SUB, FOUR_CH)
    a_step = lambda t: jnp.minimum(t, DFT_NA - 1)
    b_step = lambda t: jnp.maximum(t - DFT_NA, 0)
    tw_spec = pl.BlockSpec((DFT_MPAIR, rows, LANES), lambda b, t: (a_step(t), 0, 0))
    scr = pltpu.VMEM((DFT_A, DFT_NA, DFT_MPAIR * SUB, FOUR_CH), BF16)
    fy, *cast_out = pl.pallas_call(
        _with_casts(_dft_body, 6, 1, len(casts)),
        grid=(bsz, n_steps),
        in_specs=[pl.BlockSpec((None, DFT_A, None, DFT_MPAIR * SUB, FOUR_CH),
                               lambda b, t: (b, 0, a_step(t), 0, 0)),
                  _const_spec((2 * rows, rows)), tw_spec, tw_spec,
                  _const_spec((2 * DFT_B, 2 * DFT_B)), _const_spec((2 * FOUR_CH, FOUR_CH))] + c_in,
        out_specs=[pl.BlockSpec((None, DFT_VB, DFT_B, FOUR_CH), lambda b, t: (b, b_step(t), 0, 0))] + c_out,
        out_shape=[jax.ShapeDtypeStruct((bsz, DFT_A, DFT_B, FOUR_CH), BF16)] + c_shape,
        scratch_shapes=[scr, scr],
        compiler_params=_cparams(2),
        name="dft_2d",
    )(f5, _bf16_const(l1), twr_b, twi_b, _bf16_const(l2), mix_bf, *[a for a, _ in casts])
    return [jnp.transpose(fy, (0, 2, 1, 3)).reshape(bsz, SEQ, FOUR_CH)] + cast_out


def _dft_small_body(f_ref, lc_ref, mix_ref, fy_ref):
    n = f_ref.shape[0]
    pq = jnp.dot(lc_ref[...], f_ref[...].astype(BF16), preferred_element_type=F32)
    fy_ref[...] = _channel_dft(pq, n, mix_ref)


def _dft_2d_small(f, mix_bf):
    bsz, n, _ = f.shape
    k = np.arange(n)
    ang = 2.0 * np.pi * np.outer(k, k) / n
    lc = np.concatenate([np.cos(ang), np.sin(ang)], axis=0)
    spec = pl.BlockSpec((None, n, FOUR_CH), lambda b: (b, 0, 0))
    return pl.pallas_call(
        _dft_small_body,
        grid=(bsz,),
        in_specs=[spec, _const_spec((2 * n, n)), _const_spec((2 * FOUR_CH, FOUR_CH))],
        out_specs=spec,
        out_shape=jax.ShapeDtypeStruct((bsz, n, FOUR_CH), BF16),
        compiler_params=_cparams(1),
        name="dft_small",
    )(f, _bf16_const(lc), mix_bf)


def _channel_mix_const(seq_len):
    k = np.arange(FOUR_GROUP)
    ang = 2.0 * np.pi * np.outer(k, k) / FOUR_GROUP
    ng = FOUR_CH // FOUR_GROUP
    bdc = np.kron(np.eye(ng), np.cos(ang))
    bds = np.kron(np.eye(ng), np.sin(ang))
    return np.concatenate([bdc, -bds], axis=0) / np.sqrt(seq_len * FOUR_GROUP)


def _mixout_mlp_body(mod_row, x_ref, ay_ref, fy_ref, mod_ref, g2_ref, wout_ref, w1_ref, w2_ref, o_ref):
    b = pl.program_id(0)
    row = b if mod_row is None else mod_row
    gate1, shift2, scale2, gate2 = _mod_rows(mod_ref, row, (2, 3, 4, 5))
    cat = jnp.concatenate([ay_ref[...], fy_ref[...]], axis=1)
    mo = jnp.dot(cat, wout_ref[...], preferred_element_type=F32)
    x1 = x_ref[...] + gate1 * mo
    o_ref[...] = x1 + gate2 * _mlp(x1, g2_ref[...], shift2, scale2, w1_ref, w2_ref)


def _mixout_mlp(x, ay, fy, mod, g2, wout_bf, w1_bf, w2_bf, mod_row, tm, casts=()):
    bsz, seq_len, _ = x.shape
    nt = seq_len // tm
    row_spec = lambda w: pl.BlockSpec((None, tm, w), lambda b, t: (b, t, 0))
    c_in, c_out, c_shape = _cast_jobs(casts, bsz * nt, lambda b, t: b * nt + t)
    body = _with_casts(functools.partial(_mixout_mlp_body, mod_row), 8, 1, len(casts))
    return pl.pallas_call(
        body,
        grid=(bsz, nt),
        in_specs=[row_spec(D), row_spec(CONV_CH), row_spec(FOUR_CH),
                  _const_spec((8, N_MOD * D)), _const_spec((1, D)),
                  _resident_spec((D, D)), _resident_spec((D, D_FF)), _resident_spec((D_FF, D))] + c_in,
        out_specs=[row_spec(D)] + c_out,
        out_shape=[jax.ShapeDtypeStruct((bsz, seq_len, D), F32)] + c_shape,
        compiler_params=_cparams(2),
        name="mixer_out_mlp",
    )(x, ay, fy, mod, g2, wout_bf, w1_bf, w2_bf, *[a for a, _ in casts])


def _head_rms(t, bd_ref, gt):
    tt = (t * t).astype(BF16)
    w = bd_ref.shape[0]
    ms = jnp.concatenate(
        [jnp.dot(tt[:, j * w:(j + 1) * w], bd_ref[...], preferred_element_type=F32)
         for j in range(D // w)], axis=1)
    return t * lax.rsqrt(ms + EPS) * gt


def _qkv_common(mod_row, x_ref, mod_ref, g_ref, w_ref, bd_ref, qg_ref, kg_ref):
    b = pl.program_id(0)
    row = b if mod_row is None else mod_row
    shift, scale = _mod_rows(mod_ref, row, (0, 1))
    h = _rms_mod(x_ref[...], g_ref[...], shift, scale).astype(BF16)
    qkv = jnp.dot(h, w_ref[...], preferred_element_type=F32)
    q = _head_rms(qkv[:, 0:D], bd_ref, qg_ref[...]) * (HEAD_DIM ** -0.5 * LOG2E)
    k = _head_rms(qkv[:, D:2 * D], bd_ref, kg_ref[...])
    v = qkv[:, 2 * D:]
    return q, k, v


def _qkv_grid_body(x_ref, mod_ref, g_ref, w_ref, bd_ref, qg_ref, kg_ref, q_ref, k_ref, v_ref):
    q, k, v = _qkv_common(None, x_ref, mod_ref, g_ref, w_ref, bd_ref, qg_ref, kg_ref)
    for rho in range(TM // GRID_W):
        for g in range(NGRP):
            q0 = rho * GRID_W + QCOLS * g
            q_ref[g, rho * QCOLS:(rho + 1) * QCOLS, :] = q[q0:q0 + QCOLS].astype(BF16)
            k0 = rho * GRID_W + KC0[g]
            k_ref[g, rho * KCOLS:(rho + 1) * KCOLS, :] = k[k0:k0 + KCOLS].astype(BF16)
            v_ref[g, rho * KCOLS:(rho + 1) * KCOLS, :] = v[k0:k0 + KCOLS].astype(BF16)


def _qkv_ctx_body(mod_row, x_ref, mod_ref, g_ref, w_ref, bd_ref, qg_ref, kg_ref, k_ref, v_ref):
    _, k, v = _qkv_common(mod_row, x_ref, mod_ref, g_ref, w_ref, bd_ref, qg_ref, kg_ref)
    k_ref[...] = k.astype(BF16)
    v_ref[...] = v.astype(BF16)


def _qkv_consts(q_g, k_g):
    bd = np.kron(np.eye(MXU_TILE // HEAD_DIM), np.ones((HEAD_DIM, HEAD_DIM))) / HEAD_DIM
    qg = jnp.tile(q_g, N_HEADS).reshape(1, D)
    kg = jnp.tile(k_g, N_HEADS).reshape(1, D)
    return _bf16_const(bd), qg, kg


def _qkv_in_specs(tm):
    return [pl.BlockSpec((None, tm, D), lambda b, t: (b, t, 0)),
            _const_spec((8, N_MOD * D)), _const_spec((1, D)), _const_spec((D, 3 * D)),
            _const_spec((MXU_TILE, MXU_TILE)), _const_spec((1, D)), _const_spec((1, D))]


def _qkv_grid(x, mod, g, wqkv_bf, q_g, k_g):
    bsz = x.shape[0]
    bd, qg, kg = _qkv_consts(q_g, k_g)
    nq = TM // GRID_W * QCOLS
    nk = TM // GRID_W * KCOLS
    return pl.pallas_call(
        _qkv_grid_body,
        grid=(bsz, SEQ // TM),
        in_specs=_qkv_in_specs(TM),
        out_specs=[pl.BlockSpec((None, NGRP, nq, D), lambda b, t: (b, 0, t, 0)),
                   pl.BlockSpec((None, NGRP, nk, D), lambda b, t: (b, 0, t, 0)),
                   pl.BlockSpec((None, NGRP, nk, D), lambda b, t: (b, 0, t, 0))],
        out_shape=[jax.ShapeDtypeStruct((bsz, NGRP, GRID_ROWS * QCOLS, D), BF16),
                   jax.ShapeDtypeStruct((bsz, NGRP, GRID_ROWS * KCOLS, D), BF16),
                   jax.ShapeDtypeStruct((bsz, NGRP, GRID_ROWS * KCOLS, D), BF16)],
        compiler_params=_cparams(2),
        name="qkv_grid",
    )(x, mod, g, wqkv_bf, bd, qg, kg)


def _qkv_ctx(ctx, mod, g, wqkv_bf, q_g, k_g, mod_row):
    bsz, n, _ = ctx.shape
    bd, qg, kg = _qkv_consts(q_g, k_g)
    spec = pl.BlockSpec((None, n, D), lambda b, t: (b, t, 0))
    return pl.pallas_call(
        functools.partial(_qkv_ctx_body, mod_row),
        grid=(bsz, 1),
        in_specs=_qkv_in_specs(n),
        out_specs=[spec, spec],
        out_shape=[jax.ShapeDtypeStruct((bsz, n, D), BF16)] * 2,
        compiler_params=_cparams(2),
        name="qkv_ctx",
    )(ctx, mod, g, wqkv_bf, bd, qg, kg)


def _bias_tables(rpb):
    n_ro = 2 * WIN_ROWS - 1
    n_co = 2 * WIN_COLS - 1
    g = np.arange(NGRP)[:, None, None]
    cq = np.arange(QCOLS)[None, :, None]
    kcw = (np.arange(LANES) % KCOLS)[None, None, :]
    c = QCOLS * g + cq
    kc = np.asarray(KC0)[:, None, None] + kcw
    cs = np.clip(c - WIN_COLS // 2, 0, GRID_W - WIN_COLS)
    col_ok = (kc >= cs) & (kc < cs + WIN_COLS)
    co = np.where(col_ok, kc - c + (WIN_COLS - 1), -1)
    onehot = (np.arange(n_co)[:, None, None, None] == co[None]).astype(np.float32)
    n_pad = n_ro + 2 + KROWS_PER_TILE
    rpb_p = jnp.pad(rpb * LOG2E, ((0, 0), (3, KROWS_PER_TILE - 1), (0, 0)))
    sel = jnp.einsum("hrc,cn->hrn", rpb_p, jnp.asarray(onehot.reshape(n_co, -1)),
                     precision=lax.Precision.HIGHEST)
    sel = sel.reshape(N_HEADS, n_pad, NGRP, QCOLS, LANES).transpose(0, 2, 1, 3, 4)
    rt = jnp.where(col_ok[None, :, None], sel, NEG)
    rt = rt.reshape(N_HEADS // 2, 2, NGRP, n_pad, QCOLS, LANES)
    delta = np.arange(N_RM)[:, None, None] - WIN_ROWS
    jj2 = (np.arange(LANES) // KCOLS)[None, None, :]
    rm = np.where((jj2 >= delta) & (jj2 < delta + WIN_ROWS), 0.0, NEG)
    rm = np.broadcast_to(rm, (N_RM, QCOLS, LANES)).astype(np.float32)
    return rt.astype(F32), jnp.asarray(rm)


def _tile_indices(rb, i, jq):
    wr = int(np.clip(RB * rb - WIN_ROWS // 2, 0, GRID_ROWS - WROWS))
    r = RB * rb + i
    delta = int(np.clip(r - WIN_ROWS // 2, 0, GRID_ROWS - WIN_ROWS)) - wr - KROWS_PER_TILE * jq
    if not -WIN_ROWS < delta < KROWS_PER_TILE:
        return None
    bti = wr + KROWS_PER_TILE * jq - r + (WIN_ROWS - 1) + 3
    assert 0 <= bti < N_BT and 0 <= delta + WIN_ROWS < N_RM
    return bti, delta + WIN_ROWS


def _attn_body(q_ref, k_ref, v_ref, kc_ref, vc_ref, rt_ref, rm_ref, o_ref, s_scr, bt_scr, bm_scr):
    kc = kc_ref[...]
    vc = vc_ref[...]
    nq = RB * QCOLS
    nk = WROWS * KCOLS
    lane = lax.broadcasted_iota(jnp.int32, (nq, LANES), 1)
    nt = nk // LANES
    nc = CTX // LANES
    dn = (((1,), (1,)), ((), ()))

    @pl.when(pl.program_id(2) == 0)
    def _build_tables():
        lgrp = lax.broadcasted_iota(jnp.int32, (QCOLS, LANES), 1) // KCOLS
        for e in range(2):
            for t in range(N_BT):
                blk = rt_ref[e, t + KROWS_PER_TILE - 1]
                for jj in range(KROWS_PER_TILE - 2, -1, -1):
                    blk = jnp.where(lgrp == jj, rt_ref[e, t + jj], blk)
                bt_scr[e, t] = blk
        for e in range(2):
            for i in range(RB):
                for jq in range(nt):
                    idx = _tile_indices(1, i, jq)
                    if idx is not None:
                        bm_scr[e, i * nt + jq] = bt_scr[e, idx[0]] + rm_ref[idx[1]]

    def window_row(rb):
        return jnp.clip(RB * rb - WIN_ROWS // 2, 0, GRID_ROWS - WROWS)

    def scores(rb, slot):
        rb = jnp.asarray(rb, jnp.int32)
        koff = pl.multiple_of(window_row(rb) * KCOLS, LANES)
        kw = k_ref[pl.ds(koff, nk), :]
        q = q_ref[pl.ds(pl.multiple_of(rb * nq, nq), nq), :]
        for e in range(2):
            in_head = (lane >= HEAD_DIM * e) & (lane < HEAD_DIM * (e + 1))
            qm = jnp.where(in_head, q, jnp.zeros_like(q))
            s_scr[slot, e, :, 0:nk] = lax.dot_general(qm, kw, dn, preferred_element_type=F32)
            s_scr[slot, e, :, nk:] = lax.dot_general(qm, kc, dn, preferred_element_type=F32)

    def finish(rb, slot, edge_rb):
        rb = jnp.asarray(rb, jnp.int32)
        wr = window_row(rb)
        vw = v_ref[pl.ds(pl.multiple_of(wr * KCOLS, LANES), nk), :]
        zero_tile = jnp.zeros((QCOLS, LANES), F32)
        outs = []
        for e in range(2):
            p_rows = []
            l_rows = []
            for i in range(RB):
                rows = slice(i * QCOLS, (i + 1) * QCOLS)
                blks = {}
                for jq in range(nt):
                    idx = _tile_indices(1 if edge_rb is None else edge_rb, i, jq)
                    if idx is None:
                        continue
                    add = bm_scr[e, i * nt + jq] if edge_rb is None else bt_scr[e, idx[0]] + rm_ref[idx[1]]
                    blks[jq] = s_scr[slot, e, rows, jq * LANES:(jq + 1) * LANES] + add
                for jc in range(nc):
                    blks[nt + jc] = s_scr[slot, e, rows, nk + jc * LANES:nk + (jc + 1) * LANES]
                vals = list(blks.values())
                m = jnp.max(functools.reduce(jnp.maximum, vals), axis=-1, keepdims=True)
                ps = {j: jnp.exp2(sb - m) for j, sb in blks.items()}
                l_rows.append(jnp.sum(functools.reduce(jnp.add, list(ps.values())), axis=-1, keepdims=True))
                p_rows.append(jnp.concatenate([ps.get(j, zero_tile) for j in range(nt + nc)], axis=1))
            p = jnp.concatenate(p_rows, axis=0).astype(BF16)
            lsum = jnp.concatenate(l_rows, axis=0)
            o = (jnp.dot(p[:, :nk], vw, preferred_element_type=F32)
                 + jnp.dot(p[:, nk:], vc, preferred_element_type=F32))
            outs.append(o / lsum)
        o = jnp.where(lane < HEAD_DIM, outs[0], outs[1])
        o_ref[pl.ds(pl.multiple_of(rb * nq, nq), nq), :] = o.astype(BF16)

    scores(0, 0)
    scores(1, 1)
    finish(0, 0, 0)
    scores(2, 0)
    finish(1, 1, None)

    def pair(tt, carry):
        t0 = 2 * tt
        scores(t0 + 1, 1)
        finish(t0, 0, None)
        scores(t0 + 2, 0)
        finish(t0 + 1, 1, None)
        return carry

    lax.fori_loop(1, NRB // 2 - 1, pair, 0, unroll=True)
    scores(NRB - 1, 1)
    finish(NRB - 2, 0, None)
    finish(NRB - 1, 1, NRB - 1)


def _attention(qcb, kcb, vcb, kc, vc, bt, rm):
    bsz = qcb.shape[0]
    nhp = N_HEADS // 2
    n_t = bt.shape[3]
    return pl.pallas_call(
        _attn_body,
        grid=(nhp, NGRP, bsz),
        in_specs=[pl.BlockSpec((None, None, GRID_ROWS * QCOLS, LANES), lambda h, g, b: (b, g, 0, h)),
                  pl.BlockSpec((None, None, GRID_ROWS * KCOLS, LANES), lambda h, g, b: (b, g, 0, h)),
                  pl.BlockSpec((None, None, GRID_ROWS * KCOLS, LANES), lambda h, g, b: (b, g, 0, h)),
                  pl.BlockSpec((None, CTX, LANES), lambda h, g, b: (b, 0, h)),
                  pl.BlockSpec((None, CTX, LANES), lambda h, g, b: (b, 0, h)),
                  pl.BlockSpec((None, 2, None, n_t, QCOLS, LANES), lambda h, g, b: (h, 0, g, 0, 0, 0)),
                  _const_spec((N_RM, QCOLS, LANES))],
        out_specs=pl.BlockSpec((None, None, GRID_ROWS * QCOLS, LANES), lambda h, g, b: (b, g, 0, h)),
        out_shape=jax.ShapeDtypeStruct((bsz, NGRP, GRID_ROWS * QCOLS, D), BF16),
        scratch_shapes=[pltpu.VMEM((2, 2, RB * QCOLS, WROWS * KCOLS + CTX), F32),
                        pltpu.VMEM((2, N_BT, QCOLS, LANES), F32),
                        pltpu.VMEM((2, RB * WROWS * KCOLS // LANES, QCOLS, LANES), F32)],
        compiler_params=_cparams(3),
        name="nbr_attention",
    )(qcb, kcb, vcb, kc, vc, bt, rm)


def _attnout_mlp_body(x_ref, o_ref_in, mod_ref, g2_ref, wo_ref, w1_ref, w2_ref, out_ref):
    b = pl.program_id(0)
    gate1, shift2, scale2, gate2 = _mod_rows(mod_ref, b, (2, 3, 4, 5))
    chunks = [o_ref_in[g, rho * QCOLS:(rho + 1) * QCOLS, :]
              for rho in range(TM_MLP // GRID_W) for g in range(NGRP)]
    o_nat = jnp.concatenate(chunks, axis=0)
    mo = jnp.dot(o_nat, wo_ref[...], preferred_element_type=F32)
    x1 = x_ref[...] + gate1 * mo
    out_ref[...] = x1 + gate2 * _mlp(x1, g2_ref[...], shift2, scale2, w1_ref, w2_ref)


def _attnout_mlp(x, ocb, mod, g2, wo_bf, w1_bf, w2_bf):
    bsz = x.shape[0]
    nq = TM_MLP // GRID_W * QCOLS
    return pl.pallas_call(
        _attnout_mlp_body,
        grid=(bsz, SEQ // TM_MLP),
        in_specs=[pl.BlockSpec((None, TM_MLP, D), lambda b, t: (b, t, 0)),
                  pl.BlockSpec((None, NGRP, nq, D), lambda b, t: (b, 0, t, 0)),
                  _const_spec((8, N_MOD * D)), _const_spec((1, D)),
                  _resident_spec((D, D)), _resident_spec((D, D_FF)), _resident_spec((D_FF, D))],
        out_specs=pl.BlockSpec((None, TM_MLP, D), lambda b, t: (b, t, 0)),
        out_shape=jax.ShapeDtypeStruct((bsz, SEQ, D), F32),
        compiler_params=_cparams(2),
        name="attn_out_mlp",
    )(x, ocb, mod, g2, wo_bf, w1_bf, w2_bf)


def kernel(x, c, ctx, c_ctx, ada_w, ada_b, norm_mix_g, norm_mlp_g, mlp_w1, mlp_w2, ab_w_in, ab_conv_w,
           ab_w_out, na_w_qkv, na_q_g, na_k_g, na_rpb, na_w_out):
    bsz = x.shape[0]
    ctx_row = bsz
    cond8 = jnp.zeros((8, D), F32).at[:bsz].set(c).at[ctx_row].set(c_ctx)
    mods, w_in = _modulation(cond8, ada_w, ada_b, casts=((ab_w_in, 0),))

    g_mix = norm_mix_g.reshape(DEPTH, 1, D)
    g_mlp = norm_mlp_g.reshape(DEPTH, 1, D)
    conv_w8 = jnp.zeros((8, CONV_CH), F32).at[:3].set(ab_conv_w[0])

    ay, f, w1_0, w2_0, w_out0 = _inproj(x, mods[0], g_mix[0], w_in, conv_w8, None, TM_IN,
                                        casts=((mlp_w1, 0), (mlp_w2, 0), (ab_w_out, 0)))
    fy, w1_1, w2_1, wqkv, wo = _dft_2d(f, _bf16_const(_channel_mix_const(SEQ)),
                                       casts=((mlp_w1, 1), (mlp_w2, 1), (na_w_qkv, 0), (na_w_out, 0)))
    x, = _mixout_mlp(x, ay, fy, mods[0], g_mlp[0], w_out0, w1_0, w2_0, None, TM_MLP)

    ay_c, f_c = _inproj(ctx, mods[0], g_mix[0], w_in, conv_w8, ctx_row, CTX)
    fy_c = _dft_2d_small(f_c, _bf16_const(_channel_mix_const(CTX)))
    flat = lambda a: a.reshape(1, bsz * CTX, a.shape[-1])
    ctx, = _mixout_mlp(flat(ctx), flat(ay_c), flat(fy_c), mods[0], g_mlp[0], w_out0, w1_0, w2_0,
                       ctx_row, bsz * CTX)

    qcb, kcb, vcb = _qkv_grid(x, mods[1], g_mix[1], wqkv, na_q_g[0], na_k_g[0])
    kc, vc = _qkv_ctx(ctx, mods[1], g_mix[1], wqkv, na_q_g[0], na_k_g[0], ctx_row)
    kc, vc = kc.reshape(bsz, CTX, D), vc.reshape(bsz, CTX, D)
    bt, rm = _bias_tables(na_rpb[0])
    ocb = _attention(qcb, kcb, vcb, kc, vc, bt, rm)
    x = _attnout_mlp(x, ocb, mods[1], g_mlp[1], wo, w1_1, w2_1)
    return x
```

```python
import functools

import numpy as np
import jax
import jax.numpy as jnp
from jax import lax
from jax.experimental import pallas as pl
from jax.experimental.pallas import tpu as pltpu

F32 = jnp.float32
BF16 = jnp.bfloat16

D = 1024
DEPTH = 2
SEQ = 8192
CTX = 256
GRID_W = 64
GRID_ROWS = SEQ // GRID_W
HEAD_DIM = 64
N_HEADS = 16
CONV_CH = 512
FOUR_CH = 512
FOUR_GROUP = 64
WIN_ROWS = 8
WIN_COLS = 16
D_FF = 4 * D
N_MOD = 6
EPS = 1e-6

LANES = 128
MXU_TILE = 256

TM = 512
TM_IN = 1024
TM_MLP = 1024
HALO = 8
FF_CHUNK = 1024
VMEM_LIMIT = 56 * 1024 * 1024

DFT_A = 32
DFT_B = SEQ // DFT_A
SUB = 8

QCOLS = 16
NGRP = GRID_W // QCOLS
KCOLS = 32
KC0 = tuple(int(np.clip(QCOLS * g - 8, 0, GRID_W - KCOLS)) for g in range(NGRP))
RB = 8
NRB = GRID_ROWS // RB
WROWS = 16
KROWS_PER_TILE = LANES // KCOLS
N_RM = WIN_ROWS + KROWS_PER_TILE + 1
NEG = -1e30
LOG2E = float(np.log2(np.e))
N_BT = 2 * WIN_ROWS - 1 + KROWS_PER_TILE - 1


def _cparams(n_axes):
    return pltpu.CompilerParams(dimension_semantics=("arbitrary",) * n_axes,
                                vmem_limit_bytes=VMEM_LIMIT)


def _const_spec(shape):
    nd = len(shape)
    return pl.BlockSpec(shape, lambda *_: (0,) * nd)


def _resident_spec(shape):
    nd = len(shape)
    return pl.BlockSpec(shape, lambda *_: (0,) * nd, pipeline_mode=pl.Buffered(1))


def _cast_jobs(jobs, n_steps, step_of):
    in_specs, out_specs, out_shapes = [], [], []
    for arr, layer in jobs:
        _, rows, cols = arr.shape
        chunk = rows // n_steps
        assert chunk * n_steps == rows and chunk % 16 == 0
        in_specs.append(pl.BlockSpec((None, chunk, cols), lambda *ids, layer=layer: (layer, step_of(*ids), 0)))
        out_specs.append(pl.BlockSpec((chunk, cols), lambda *ids: (step_of(*ids), 0)))
        out_shapes.append(jax.ShapeDtypeStruct((rows, cols), BF16))
    return in_specs, out_specs, out_shapes


def _with_casts(body, n_in, n_out, n_jobs):
    def wrapped(*refs):
        ins, refs = refs[:n_in], refs[n_in:]
        cast_ins, refs = refs[:n_jobs], refs[n_jobs:]
        outs, refs = refs[:n_out], refs[n_out:]
        cast_outs, scratch = refs[:n_jobs], refs[n_jobs:]
        for src, dst in zip(cast_ins, cast_outs):
            dst[...] = src[...].astype(BF16)
        body(*ins, *outs, *scratch)
    return wrapped


def _bf16_const(a):
    return jnp.asarray(a, F32).astype(BF16)


def _rms_mod(x, g, shift, scale):
    ms = jnp.mean(x * x, axis=-1, keepdims=True)
    y = x * lax.rsqrt(ms + EPS)
    return (y * g) * (1.0 + scale) + shift


def _mod_rows(mod_ref, row, ks):
    return [mod_ref[pl.ds(row, 1), k * D:(k + 1) * D] for k in ks]


def _mlp(x1, g, shift, scale, w1_ref, w2_ref):
    h = _rms_mod(x1, g, shift, scale).astype(BF16)
    acc = jnp.zeros(x1.shape, F32)
    for c in range(D_FF // FF_CHUNK):
        a = jnp.dot(h, w1_ref[:, c * FF_CHUNK:(c + 1) * FF_CHUNK], preferred_element_type=F32)
        a = jnp.maximum(a, 0.0)
        a = (a * a).astype(BF16)
        acc = acc + jnp.dot(a, w2_ref[c * FF_CHUNK:(c + 1) * FF_CHUNK, :], preferred_element_type=F32)
    return acc


def _mod_body(cond_ref, w_ref, b_ref, o_ref):
    c = cond_ref[...]
    s = c * jax.nn.sigmoid(c)
    o_ref[...] = jnp.dot(s.astype(BF16), w_ref[...].astype(BF16), preferred_element_type=F32) + b_ref[...]


MOD_TILE = 768


def _modulation(cond8, ada_w, ada_b, casts=()):
    nt = N_MOD * D // MOD_TILE
    c_in, c_out, c_shape = _cast_jobs(casts, DEPTH * nt, lambda l, n: l * nt + n)
    return pl.pallas_call(
        _with_casts(_mod_body, 3, 1, len(casts)),
        grid=(DEPTH, nt),
        in_specs=[_const_spec((8, D)),
                  pl.BlockSpec((None, D, MOD_TILE), lambda l, n: (l, 0, n)),
                  pl.BlockSpec((None, 1, MOD_TILE), lambda l, n: (l, 0, n))] + c_in,
        out_specs=[pl.BlockSpec((None, 8, MOD_TILE), lambda l, n: (l, 0, n))] + c_out,
        out_shape=[jax.ShapeDtypeStruct((DEPTH, 8, N_MOD * D), F32)] + c_shape,
        compiler_params=_cparams(2),
        name="adaln_mod",
    )(cond8, ada_w, ada_b.reshape(DEPTH, 1, N_MOD * D), *[a for a, _ in casts])


def _inproj_body(mod_row, seq_len, tm, xp_ref, x_ref, xn_ref, mod_ref, g_ref, w_ref, cw_ref,
                 ay_ref, f_ref):
    b = pl.program_id(0)
    t = pl.program_id(1)
    row = b if mod_row is None else mod_row
    xx = jnp.concatenate([xp_ref[...], x_ref[...], xn_ref[...]], axis=0)
    shift, scale = _mod_rows(mod_ref, row, (0, 1))
    h = _rms_mod(xx, g_ref[...], shift, scale).astype(BF16)
    u = jnp.dot(h, w_ref[...], preferred_element_type=F32)
    z = u[:, CONV_CH:2 * CONV_CH] * u[:, 0:CONV_CH]
    n = t * tm - HALO + lax.broadcasted_iota(jnp.int32, (tm + 2 * HALO, 1), 0)
    z = jnp.where((n >= 0) & (n < seq_len), z, 0.0)
    cw = cw_ref[...]
    zc = (z[HALO - 1:HALO - 1 + tm] * cw[0:1] + z[HALO:HALO + tm] * cw[1:2]
          + z[HALO + 1:HALO + 1 + tm] * cw[2:3])
    ay_ref[...] = (u[HALO:HALO + tm, 2 * CONV_CH:3 * CONV_CH] * zc).astype(BF16)
    f_ref[...] = u[HALO:HALO + tm, 3 * CONV_CH:]


def _inproj(x, mod, g, w_in_bf, conv_w8, mod_row, tm, casts=()):
    bsz, seq_len, _ = x.shape
    nt = seq_len // tm
    nb8 = seq_len // HALO
    r8 = tm // HALO
    c_in, c_out, c_shape = _cast_jobs(casts, bsz * nt, lambda b, t: b * nt + t)
    body = _with_casts(functools.partial(_inproj_body, mod_row, seq_len, tm), 7, 2, len(casts))
    return pl.pallas_call(
        body,
        grid=(bsz, nt),
        in_specs=[pl.BlockSpec((None, HALO, D), lambda b, t: (b, jnp.maximum(t * r8 - 1, 0), 0)),
                  pl.BlockSpec((None, tm, D), lambda b, t: (b, t, 0)),
                  pl.BlockSpec((None, HALO, D), lambda b, t: (b, jnp.minimum((t + 1) * r8, nb8 - 1), 0)),
                  _const_spec((8, N_MOD * D)),
                  _const_spec((1, D)),
                  _const_spec((D, 3 * CONV_CH + FOUR_CH)),
                  _const_spec((8, CONV_CH))] + c_in,
        out_specs=[pl.BlockSpec((None, tm, CONV_CH), lambda b, t: (b, t, 0)),
                   pl.BlockSpec((None, tm, FOUR_CH), lambda b, t: (b, t, 0))] + c_out,
        out_shape=[jax.ShapeDtypeStruct((bsz, seq_len, CONV_CH), BF16),
                   jax.ShapeDtypeStruct((bsz, seq_len, FOUR_CH), F32)] + c_shape,
        compiler_params=_cparams(2),
        name="mixer_in",
    )(x, x, x, mod, g, w_in_bf, conv_w8, *[a for a, _ in casts])


def _dft_consts():
    a = np.arange(DFT_A)
    s = np.arange(SUB)
    ang = 2.0 * np.pi * np.outer(a, a) / DFT_A
    eye = np.eye(SUB)
    re = np.einsum("va,ts->vtas", np.cos(ang), eye).reshape(DFT_A * SUB, DFT_A * SUB)
    im = np.einsum("va,ts->vtas", -np.sin(ang), eye).reshape(DFT_A * SUB, DFT_A * SUB)
    l1 = np.concatenate([re, im], axis=0)
    m = np.arange(DFT_B // SUB)
    bb = (SUB * m[:, None, None] + s[None, None, :])
    tang = 2.0 * np.pi * a[None, :, None] * bb / SEQ
    twr = np.cos(tang).reshape(DFT_B // SUB, DFT_A * SUB, 1)
    twi = (-np.sin(tang)).reshape(DFT_B // SUB, DFT_A * SUB, 1)
    u = np.arange(DFT_B)
    ang2 = 2.0 * np.pi * np.outer(u, u) / DFT_B
    c2, s2 = np.cos(ang2), np.sin(ang2)
    l2 = np.block([[c2, s2], [s2, -c2]])
    return l1, twr, twi, l2


DFT_MPAIR = 4
DFT_VB = 4
DFT_NA = DFT_B // SUB // DFT_MPAIR
DFT_NB = DFT_A // DFT_VB


def _channel_dft(pq, n, mix_ref):
    w = MXU_TILE
    outs = []
    for cb in range(FOUR_CH // w):
        cols = slice(cb * w, (cb + 1) * w)
        lhs = jnp.concatenate([pq[:n, cols], pq[n:, cols]], axis=1).astype(BF16)
        rhs = jnp.concatenate([mix_ref[cols, cols],
                               mix_ref[FOUR_CH + cb * w:FOUR_CH + (cb + 1) * w, cols]], axis=0)
        outs.append(jnp.dot(lhs, rhs, preferred_element_type=F32))
    return jnp.concatenate(outs, axis=1).astype(BF16)


def _dft_body(x_ref, l1_ref, twr_ref, twi_ref, l2_ref, mix_ref, fy_ref, br_scr, bi_scr):
    t = pl.program_id(1)
    rows = DFT_A * SUB
    reps = FOUR_CH // LANES

    @pl.when(t < DFT_NA)
    def _stage_a():
        x = x_ref[...]
        brs, bis = [], []
        for j in range(DFT_MPAIR):
            xj = x[:, j * SUB:(j + 1) * SUB, :].reshape(rows, FOUR_CH).astype(BF16)
            a = jnp.dot(l1_ref[...], xj, preferred_element_type=F32)
            ar, ai = a[:rows], a[rows:]
            twr = jnp.concatenate([twr_ref[j]] * reps, axis=1)
            twi = jnp.concatenate([twi_ref[j]] * reps, axis=1)
            brs.append((ar * twr - ai * twi).reshape(DFT_A, SUB, FOUR_CH))
            bis.append((ar * twi + ai * twr).reshape(DFT_A, SUB, FOUR_CH))
        shape = (DFT_A, 1, DFT_MPAIR * SUB, FOUR_CH)
        br_scr[:, pl.ds(t, 1)] = jnp.concatenate(brs, axis=1).astype(BF16).reshape(shape)
        bi_scr[:, pl.ds(t, 1)] = jnp.concatenate(bis, axis=1).astype(BF16).reshape(shape)

    @pl.when(t >= DFT_NA)
    def _stage_b():
        v0 = (t - DFT_NA) * DFT_VB
        for j in range(DFT_VB):
            br = br_scr[v0 + j].reshape(DFT_B, FOUR_CH)
            bi = bi_scr[v0 + j].reshape(DFT_B, FOUR_CH)
            pq = jnp.dot(l2_ref[...], jnp.concatenate([br, bi], axis=0),
                         preferred_element_type=F32)
            fy_ref[j] = _channel_dft(pq, DFT_B, mix_ref)


def _dft_2d(f, mix_bf, casts=()):
    bsz = f.shape[0]
    n_steps = DFT_NA + DFT_NB
    c_in, c_out, c_shape = _cast_jobs(casts, bsz * n_steps, lambda b, t: b * n_steps + t)
    l1, twr, twi, l2 = _dft_consts()
    nm = DFT_B // SUB
    rows = DFT_A * SUB
    twr_b = jnp.asarray(np.broadcast_to(twr, (nm, rows, LANES)), F32)
    twi_b = jnp.asarray(np.broadcast_to(twi, (nm, rows, LANES)), F32)
    f5 = f.reshape(bsz, DFT_A, DFT_NA, DFT_MPAIR * SUB, FOUR_CH)
    a_step = lambda t: jnp.minimum(t, DFT_NA - 1)
    b_step = lambda t: jnp.maximum(t - DFT_NA, 0)
    tw_spec = pl.BlockSpec((DFT_MPAIR, rows, LANES), lambda b, t: (a_step(t), 0, 0))
    scr = pltpu.VMEM((DFT_A, DFT_NA, DFT_MPAIR * SUB, FOUR_CH), BF16)
    fy, *cast_out = pl.pallas_call(
        _with_casts(_dft_body, 6, 1, len(casts)),
        grid=(bsz, n_steps),
        in_specs=[pl.BlockSpec((None, DFT_A, None, DFT_MPAIR * SUB, FOUR_CH),
                               lambda b, t: (b, 0, a_step(t), 0, 0)),
                  _const_spec((2 * rows, rows)), tw_spec, tw_spec,
                  _const_spec((2 * DFT_B, 2 * DFT_B)), _const_spec((2 * FOUR_CH, FOUR_CH))] + c_in,
        out_specs=[pl.BlockSpec((None, DFT_VB, DFT_B, FOUR_CH), lambda b, t: (b, b_step(t), 0, 0))] + c_out,
        out_shape=[jax.ShapeDtypeStruct((bsz, DFT_A, DFT_B, FOUR_CH), BF16)] + c_shape,
        scratch_shapes=[scr, scr],
        compiler_params=_cparams(2),
        name="dft_2d",
    )(f5, _bf16_const(l1), twr_b, twi_b, _bf16_const(l2), mix_bf, *[a for a, _ in casts])
    return [jnp.transpose(fy, (0, 2, 1, 3)).reshape(bsz, SEQ, FOUR_CH)] + cast_out


def _dft_small_body(f_ref, lc_ref, mix_ref, fy_ref):
    n = f_ref.shape[0]
    pq = jnp.dot(lc_ref[...], f_ref[...].astype(BF16), preferred_element_type=F32)
    fy_ref[...] = _channel_dft(pq, n, mix_ref)


def _dft_2d_small(f, mix_bf):
    bsz, n, _ = f.shape
    k = np.arange(n)
    ang = 2.0 * np.pi * np.outer(k, k) / n
    lc = np.concatenate([np.cos(ang), np.sin(ang)], axis=0)
    spec = pl.BlockSpec((None, n, FOUR_CH), lambda b: (b, 0, 0))
    return pl.pallas_call(
        _dft_small_body,
        grid=(bsz,),
        in_specs=[spec, _const_spec((2 * n, n)), _const_spec((2 * FOUR_CH, FOUR_CH))],
        out_specs=spec,
        out_shape=jax.ShapeDtypeStruct((bsz, n, FOUR_CH), BF16),
        compiler_params=_cparams(1),
        name="dft_small",
    )(f, _bf16_const(lc), mix_bf)


def _channel_mix_const(seq_len):
    k = np.arange(FOUR_GROUP)
    ang = 2.0 * np.pi * np.outer(k, k) / FOUR_GROUP
    ng = FOUR_CH // FOUR_GROUP
    bdc = np.kron(np.eye(ng), np.cos(ang))
    bds = np.kron(np.eye(ng), np.sin(ang))
    return np.concatenate([bdc, -bds], axis=0) / np.sqrt(seq_len * FOUR_GROUP)


def _mixout_mlp_body(mod_row, x_ref, ay_ref, fy_ref, mod_ref, g2_ref, wout_ref, w1_ref, w2_ref, o_ref):
    b = pl.program_id(0)
    row = b if mod_row is None else mod_row
    gate1, shift2, scale2, gate2 = _mod_rows(mod_ref, row, (2, 3, 4, 5))
    cat = jnp.concatenate([ay_ref[...], fy_ref[...]], axis=1)
    mo = jnp.dot(cat, wout_ref[...], preferred_element_type=F32)
    x1 = x_ref[...] + gate1 * mo
    o_ref[...] = x1 + gate2 * _mlp(x1, g2_ref[...], shift2, scale2, w1_ref, w2_ref)


def _mixout_mlp(x, ay, fy, mod, g2, wout_bf, w1_bf, w2_bf, mod_row, tm, casts=()):
    bsz, seq_len, _ = x.shape
    nt = seq_len // tm
    row_spec = lambda w: pl.BlockSpec((None, tm, w), lambda b, t: (b, t, 0))
    c_in, c_out, c_shape = _cast_jobs(casts, bsz * nt, lambda b, t: b * nt + t)
    body = _with_casts(functools.partial(_mixout_mlp_body, mod_row), 8, 1, len(casts))
    return pl.pallas_call(
        body,
        grid=(bsz, nt),
        in_specs=[row_spec(D), row_spec(CONV_CH), row_spec(FOUR_CH),
                  _const_spec((8, N_MOD * D)), _const_spec((1, D)),
                  _resident_spec((D, D)), _resident_spec((D, D_FF)), _resident_spec((D_FF, D))] + c_in,
        out_specs=[row_spec(D)] + c_out,
        out_shape=[jax.ShapeDtypeStruct((bsz, seq_len, D), F32)] + c_shape,
        compiler_params=_cparams(2),
        name="mixer_out_mlp",
    )(x, ay, fy, mod, g2, wout_bf, w1_bf, w2_bf, *[a for a, _ in casts])


def _head_rms(t, bd_ref, gt):
    tt = (t * t).astype(BF16)
    w = bd_ref.shape[0]
    ms = jnp.concatenate(
        [jnp.dot(tt[:, j * w:(j + 1) * w], bd_ref[...], preferred_element_type=F32)
         for j in range(D // w)], axis=1)
    return t * lax.rsqrt(ms + EPS) * gt


def _qkv_common(mod_row, x_ref, mod_ref, g_ref, w_ref, bd_ref, qg_ref, kg_ref):
    b = pl.program_id(0)
    row = b if mod_row is None else mod_row
    shift, scale = _mod_rows(mod_ref, row, (0, 1))
    h = _rms_mod(x_ref[...], g_ref[...], shift, scale).astype(BF16)
    qkv = jnp.dot(h, w_ref[...], preferred_element_type=F32)
    q = _head_rms(qkv[:, 0:D], bd_ref, qg_ref[...]) * (HEAD_DIM ** -0.5 * LOG2E)
    k = _head_rms(qkv[:, D:2 * D], bd_ref, kg_ref[...])
    v = qkv[:, 2 * D:]
    return q, k, v


def _qkv_grid_body(x_ref, mod_ref, g_ref, w_ref, bd_ref, qg_ref, kg_ref, q_ref, k_ref, v_ref):
    q, k, v = _qkv_common(None, x_ref, mod_ref, g_ref, w_ref, bd_ref, qg_ref, kg_ref)
    for rho in range(TM // GRID_W):
        for g in range(NGRP):
            q0 = rho * GRID_W + QCOLS * g
            q_ref[g, rho * QCOLS:(rho + 1) * QCOLS, :] = q[q0:q0 + QCOLS].astype(BF16)
            k0 = rho * GRID_W + KC0[g]
            k_ref[g, rho * KCOLS:(rho + 1) * KCOLS, :] = k[k0:k0 + KCOLS].astype(BF16)
            v_ref[g, rho * KCOLS:(rho + 1) * KCOLS, :] = v[k0:k0 + KCOLS].astype(BF16)


def _qkv_ctx_body(mod_row, x_ref, mod_ref, g_ref, w_ref, bd_ref, qg_ref, kg_ref, k_ref, v_ref):
    _, k, v = _qkv_common(mod_row, x_ref, mod_ref, g_ref, w_ref, bd_ref, qg_ref, kg_ref)
    k_ref[...] = k.astype(BF16)
    v_ref[...] = v.astype(BF16)


def _qkv_consts(q_g, k_g):
    bd = np.kron(np.eye(MXU_TILE // HEAD_DIM), np.ones((HEAD_DIM, HEAD_DIM))) / HEAD_DIM
    qg = jnp.tile(q_g, N_HEADS).reshape(1, D)
    kg = jnp.tile(k_g, N_HEADS).reshape(1, D)
    return _bf16_const(bd), qg, kg


def _qkv_in_specs(tm):
    return [pl.BlockSpec((None, tm, D), lambda b, t: (b, t, 0)),
            _const_spec((8, N_MOD * D)), _const_spec((1, D)), _const_spec((D, 3 * D)),
            _const_spec((MXU_TILE, MXU_TILE)), _const_spec((1, D)), _const_spec((1, D))]


def _qkv_grid(x, mod, g, wqkv_bf, q_g, k_g):
    bsz = x.shape[0]
    bd, qg, kg = _qkv_consts(q_g, k_g)
    nq = TM // GRID_W * QCOLS
    nk = TM // GRID_W * KCOLS
    return pl.pallas_call(
        _qkv_grid_body,
        grid=(bsz, SEQ // TM),
        in_specs=_qkv_in_specs(TM),
        out_specs=[pl.BlockSpec((None, NGRP, nq, D), lambda b, t: (b, 0, t, 0)),
                   pl.BlockSpec((None, NGRP, nk, D), lambda b, t: (b, 0, t, 0)),
                   pl.BlockSpec((None, NGRP, nk, D), lambda b, t: (b, 0, t, 0))],
        out_shape=[jax.ShapeDtypeStruct((bsz, NGRP, GRID_ROWS * QCOLS, D), BF16),
                   jax.ShapeDtypeStruct((bsz, NGRP, GRID_ROWS * KCOLS, D), BF16),
                   jax.ShapeDtypeStruct((bsz, NGRP, GRID_ROWS * KCOLS, D), BF16)],
        compiler_params=_cparams(2),
        name="qkv_grid",
    )(x, mod, g, wqkv_bf, bd, qg, kg)


def _qkv_ctx(ctx, mod, g, wqkv_bf, q_g, k_g, mod_row):
    bsz, n, _ = ctx.shape
    bd, qg, kg = _qkv_consts(q_g, k_g)
    spec = pl.BlockSpec((None, n, D), lambda b, t: (b, t, 0))
    return pl.pallas_call(
        functools.partial(_qkv_ctx_body, mod_row),
        grid=(bsz, 1),
        in_specs=_qkv_in_specs(n),
        out_specs=[spec, spec],
        out_shape=[jax.ShapeDtypeStruct((bsz, n, D), BF16)] * 2,
        compiler_params=_cparams(2),
        name="qkv_ctx",
    )(ctx, mod, g, wqkv_bf, bd, qg, kg)


def _bias_rows_body(rpb_ref, oh_ref, neg_ref, out_ref):
    rows = rpb_ref[...]
    for cq in range(QCOLS):
        sel = jnp.dot(rows, oh_ref[cq], preferred_element_type=F32, precision=lax.Precision.HIGHEST)
        out_ref[:, cq, :] = sel + neg_ref[cq:cq + 1, :]


def _bias_tables(rpb):
    n_ro = 2 * WIN_ROWS - 1
    n_co = 2 * WIN_COLS - 1
    g = np.arange(NGRP)[:, None, None]
    cq = np.arange(QCOLS)[None, :, None]
    kcw = (np.arange(LANES) % KCOLS)[None, None, :]
    c = QCOLS * g + cq
    kc = np.asarray(KC0)[:, None, None] + kcw
    cs = np.clip(c - WIN_COLS // 2, 0, GRID_W - WIN_COLS)
    col_ok = (kc >= cs) & (kc < cs + WIN_COLS)
    co = np.where(col_ok, kc - c + (WIN_COLS - 1), -1)
    onehot = (np.arange(LANES)[None, None, :, None] == co[:, :, None, :]).astype(np.float32)
    colneg = np.where(col_ok, 0.0, NEG).astype(np.float32)
    n_pad = 24
    rpb_p = jnp.pad(rpb * LOG2E, ((0, 0), (3, n_pad - n_ro - 3), (0, LANES - n_co)))
    rt = pl.pallas_call(
        _bias_rows_body,
        grid=(N_HEADS, NGRP),
        in_specs=[pl.BlockSpec((None, n_pad, LANES), lambda h, g: (h, 0, 0)),
                  pl.BlockSpec((None, QCOLS, LANES, LANES), lambda h, g: (g, 0, 0, 0)),
                  pl.BlockSpec((None, QCOLS, LANES), lambda h, g: (g, 0, 0))],
        out_specs=pl.BlockSpec((None, None, n_pad, QCOLS, LANES), lambda h, g: (h, g, 0, 0, 0)),
        out_shape=jax.ShapeDtypeStruct((N_HEADS, NGRP, n_pad, QCOLS, LANES), F32),
        compiler_params=_cparams(2),
        name="bias_rows",
    )(rpb_p, jnp.asarray(onehot), jnp.asarray(colneg))
    rt = rt.reshape(N_HEADS // 2, 2, NGRP, n_pad, QCOLS, LANES)
    delta = np.arange(N_RM)[:, None, None] - WIN_ROWS
    jj2 = (np.arange(LANES) // KCOLS)[None, None, :]
    rm = np.where((jj2 >= delta) & (jj2 < delta + WIN_ROWS), 0.0, NEG)
    rm = np.broadcast_to(rm, (N_RM, QCOLS, LANES)).astype(np.float32)
    return rt.astype(F32), jnp.asarray(rm)


def _tile_indices(rb, i, jq):
    wr = int(np.clip(RB * rb - WIN_ROWS // 2, 0, GRID_ROWS - WROWS))
    r = RB * rb + i
    delta = int(np.clip(r - WIN_ROWS // 2, 0, GRID_ROWS - WIN_ROWS)) - wr - KROWS_PER_TILE * jq
    if not -WIN_ROWS < delta < KROWS_PER_TILE:
        return None
    bti = wr + KROWS_PER_TILE * jq - r + (WIN_ROWS - 1) + 3
    assert 0 <= bti < N_BT and 0 <= delta + WIN_ROWS < N_RM
    return bti, delta + WIN_ROWS


def _attn_body(q_ref, k_ref, v_ref, kc_ref, vc_ref, rt_ref, rm_ref, o_ref, s_scr, bt_scr, bm_scr):
    kc = kc_ref[...]
    vc = vc_ref[...]
    nq = RB * QCOLS
    nk = WROWS * KCOLS
    lane = lax.broadcasted_iota(jnp.int32, (nq, LANES), 1)
    nt = nk // LANES
    nc = CTX // LANES
    dn = (((1,), (1,)), ((), ()))

    @pl.when(pl.program_id(2) == 0)
    def _build_tables():
        lgrp = lax.broadcasted_iota(jnp.int32, (QCOLS, LANES), 1) // KCOLS
        for e in range(2):
            for t in range(N_BT):
                blk = rt_ref[e, t + KROWS_PER_TILE - 1]
                for jj in range(KROWS_PER_TILE - 2, -1, -1):
                    blk = jnp.where(lgrp == jj, rt_ref[e, t + jj], blk)
                bt_scr[e, t] = blk
        for e in range(2):
            for i in range(RB):
                for jq in range(nt):
                    idx = _tile_indices(1, i, jq)
                    if idx is not None:
                        bm_scr[e, i * nt + jq] = bt_scr[e, idx[0]] + rm_ref[idx[1]]

    def window_row(rb):
        return jnp.clip(RB * rb - WIN_ROWS // 2, 0, GRID_ROWS - WROWS)

    def scores(rb, slot):
        rb = jnp.asarray(rb, jnp.int32)
        koff = pl.multiple_of(window_row(rb) * KCOLS, LANES)
        kw = k_ref[pl.ds(koff, nk), :]
        q = q_ref[pl.ds(pl.multiple_of(rb * nq, nq), nq), :]
        for e in range(2):
            in_head = (lane >= HEAD_DIM * e) & (lane < HEAD_DIM * (e + 1))
            qm = jnp.where(in_head, q, jnp.zeros_like(q))
            s_scr[slot, e, :, 0:nk] = lax.dot_general(qm, kw, dn, preferred_element_type=F32)
            s_scr[slot, e, :, nk:] = lax.dot_general(qm, kc, dn, preferred_element_type=F32)

    def finish(rb, slot, edge_rb):
        rb = jnp.asarray(rb, jnp.int32)
        wr = window_row(rb)
        vw = v_ref[pl.ds(pl.multiple_of(wr * KCOLS, LANES), nk), :]
        zero_tile = jnp.zeros((QCOLS, LANES), F32)
        outs = []
        for e in range(2):
            p_rows = []
            l_rows = []
            for i in range(RB):
                rows = slice(i * QCOLS, (i + 1) * QCOLS)
                blks = {}
                for jq in range(nt):
                    idx = _tile_indices(1 if edge_rb is None else edge_rb, i, jq)
                    if idx is None:
                        continue
                    add = bm_scr[e, i * nt + jq] if edge_rb is None else bt_scr[e, idx[0]] + rm_ref[idx[1]]
                    blks[jq] = s_scr[slot, e, rows, jq * LANES:(jq + 1) * LANES] + add
                for jc in range(nc):
                    blks[nt + jc] = s_scr[slot, e, rows, nk + jc * LANES:nk + (jc + 1) * LANES]
                vals = list(blks.values())
                m = jnp.max(functools.reduce(jnp.maximum, vals), axis=-1, keepdims=True)
                ps = {j: jnp.exp2(sb - m) for j, sb in blks.items()}
                l_rows.append(jnp.sum(functools.reduce(jnp.add, list(ps.values())), axis=-1, keepdims=True))
                p_rows.append(jnp.concatenate([ps.get(j, zero_tile) for j in range(nt + nc)], axis=1))
            p = jnp.concatenate(p_rows, axis=0).astype(BF16)
            lsum = jnp.concatenate(l_rows, axis=0)
            o = (jnp.dot(p[:, :nk], vw, preferred_element_type=F32)
                 + jnp.dot(p[:, nk:], vc, preferred_element_type=F32))
            outs.append(o / lsum)
        o = jnp.where(lane < HEAD_DIM, outs[0], outs[1])
        o_ref[pl.ds(pl.multiple_of(rb * nq, nq), nq), :] = o.astype(BF16)

    scores(0, 0)
    scores(1, 1)
    finish(0, 0, 0)
    scores(2, 0)
    finish(1, 1, None)

    def pair(tt, carry):
        t0 = 2 * tt
        scores(t0 + 1, 1)
        finish(t0, 0, None)
        scores(t0 + 2, 0)
        finish(t0 + 1, 1, None)
        return carry

    lax.fori_loop(1, NRB // 2 - 1, pair, 0, unroll=True)
    scores(NRB - 1, 1)
    finish(NRB - 2, 0, None)
    finish(NRB - 1, 1, NRB - 1)


def _attention(qcb, kcb, vcb, kc, vc, bt, rm, casts=()):
    bsz = qcb.shape[0]
    nhp = N_HEADS // 2
    n_t = bt.shape[3]
    c_in, c_out, c_shape = _cast_jobs(casts, nhp * NGRP * bsz, lambda h, g, b: (h * NGRP + g) * bsz + b)
    return pl.pallas_call(
        _with_casts(_attn_body, 7, 1, len(casts)),
        grid=(nhp, NGRP, bsz),
        in_specs=[pl.BlockSpec((None, None, GRID_ROWS * QCOLS, LANES), lambda h, g, b: (b, g, 0, h)),
                  pl.BlockSpec((None, None, GRID_ROWS * KCOLS, LANES), lambda h, g, b: (b, g, 0, h)),
                  pl.BlockSpec((None, None, GRID_ROWS * KCOLS, LANES), lambda h, g, b: (b, g, 0, h)),
                  pl.BlockSpec((None, CTX, LANES), lambda h, g, b: (b, 0, h)),
                  pl.BlockSpec((None, CTX, LANES), lambda h, g, b: (b, 0, h)),
                  pl.BlockSpec((None, 2, None, n_t, QCOLS, LANES), lambda h, g, b: (h, 0, g, 0, 0, 0)),
                  _const_spec((N_RM, QCOLS, LANES))] + c_in,
        out_specs=[pl.BlockSpec((None, None, GRID_ROWS * QCOLS, LANES), lambda h, g, b: (b, g, 0, h))] + c_out,
        out_shape=[jax.ShapeDtypeStruct((bsz, NGRP, GRID_ROWS * QCOLS, D), BF16)] + c_shape,
        scratch_shapes=[pltpu.VMEM((2, 2, RB * QCOLS, WROWS * KCOLS + CTX), F32),
                        pltpu.VMEM((2, N_BT, QCOLS, LANES), F32),
                        pltpu.VMEM((2, RB * WROWS * KCOLS // LANES, QCOLS, LANES), F32)],
        compiler_params=_cparams(3),
        name="nbr_attention",
    )(qcb, kcb, vcb, kc, vc, bt, rm, *[a for a, _ in casts])


def _attnout_mlp_body(x_ref, o_ref_in, mod_ref, g2_ref, wo_ref, w1_ref, w2_ref, out_ref):
    b = pl.program_id(0)
    gate1, shift2, scale2, gate2 = _mod_rows(mod_ref, b, (2, 3, 4, 5))
    chunks = [o_ref_in[g, rho * QCOLS:(rho + 1) * QCOLS, :]
              for rho in range(TM_MLP // GRID_W) for g in range(NGRP)]
    o_nat = jnp.concatenate(chunks, axis=0)
    mo = jnp.dot(o_nat, wo_ref[...], preferred_element_type=F32)
    x1 = x_ref[...] + gate1 * mo
    out_ref[...] = x1 + gate2 * _mlp(x1, g2_ref[...], shift2, scale2, w1_ref, w2_ref)


def _attnout_mlp(x, ocb, mod, g2, wo_bf, w1_bf, w2_bf):
    bsz = x.shape[0]
    nq = TM_MLP // GRID_W * QCOLS
    return pl.pallas_call(
        _attnout_mlp_body,
        grid=(bsz, SEQ // TM_MLP),
        in_specs=[pl.BlockSpec((None, TM_MLP, D), lambda b, t: (b, t, 0)),
                  pl.BlockSpec((None, NGRP, nq, D), lambda b, t: (b, 0, t, 0)),
                  _const_spec((8, N_MOD * D)), _const_spec((1, D)),
                  _resident_spec((D, D)), _resident_spec((D, D_FF)), _resident_spec((D_FF, D))],
        out_specs=pl.BlockSpec((None, TM_MLP, D), lambda b, t: (b, t, 0)),
        out_shape=jax.ShapeDtypeStruct((bsz, SEQ, D), F32),
        compiler_params=_cparams(2),
        name="attn_out_mlp",
    )(x, ocb, mod, g2, wo_bf, w1_bf, w2_bf)


def kernel(x, c, ctx, c_ctx, ada_w, ada_b, norm_mix_g, norm_mlp_g, mlp_w1, mlp_w2, ab_w_in, ab_conv_w,
           ab_w_out, na_w_qkv, na_q_g, na_k_g, na_rpb, na_w_out):
    bsz = x.shape[0]
    ctx_row = bsz
    cond8 = jnp.zeros((8, D), F32).at[:bsz].set(c).at[ctx_row].set(c_ctx)
    mods, w_in = _modulation(cond8, ada_w, ada_b, casts=((ab_w_in, 0),))

    g_mix = norm_mix_g.reshape(DEPTH, 1, D)
    g_mlp = norm_mlp_g.reshape(DEPTH, 1, D)
    conv_w8 = jnp.zeros((8, CONV_CH), F32).at[:3].set(ab_conv_w[0])

    ay, f, w1_0, w2_0, w_out0, wqkv, w1_1, w2_1, wo = _inproj(
        x, mods[0], g_mix[0], w_in, conv_w8, None, TM_IN,
        casts=((mlp_w1, 0), (mlp_w2, 0), (ab_w_out, 0), (na_w_qkv, 0), (mlp_w1, 1), (mlp_w2, 1), (na_w_out, 0)))
    fy, = _dft_2d(f, _bf16_const(_channel_mix_const(SEQ)))
    x, = _mixout_mlp(x, ay, fy, mods[0], g_mlp[0], w_out0, w1_0, w2_0, None, TM_MLP)

    ay_c, f_c = _inproj(ctx, mods[0], g_mix[0], w_in, conv_w8, ctx_row, CTX)
    fy_c = _dft_2d_small(f_c, _bf16_const(_channel_mix_const(CTX)))
    flat = lambda a: a.reshape(1, bsz * CTX, a.shape[-1])
    ctx, = _mixout_mlp(flat(ctx), flat(ay_c), flat(fy_c), mods[0], g_mlp[0], w_out0, w1_0, w2_0,
                       ctx_row, bsz * CTX)

    qcb, kcb, vcb = _qkv_grid(x, mods[1], g_mix[1], wqkv, na_q_g[0], na_k_g[0])
    kc, vc = _qkv_ctx(ctx, mods[1], g_mix[1], wqkv, na_q_g[0], na_k_g[0], ctx_row)
    kc, vc = kc.reshape(bsz, CTX, D), vc.reshape(bsz, CTX, D)
    bt, rm = _bias_tables(na_rpb[0])
    ocb, = _attention(qcb, kcb, vcb, kc, vc, bt, rm)
    x = _attnout_mlp(x, ocb, mods[1], g_mlp[1], wo, w1_1, w2_1)
    return x
```

```python
import functools

import numpy as np
import jax
import jax.numpy as jnp
from jax import lax
from jax.experimental import pallas as pl
from jax.experimental.pallas import tpu as pltpu

F32 = jnp.float32
BF16 = jnp.bfloat16

D = 1024
DEPTH = 2
SEQ = 8192
CTX = 256
GRID_W = 64
GRID_ROWS = SEQ // GRID_W
HEAD_DIM = 64
N_HEADS = 16
CONV_CH = 512
FOUR_CH = 512
FOUR_GROUP = 64
WIN_ROWS = 8
WIN_COLS = 16
D_FF = 4 * D
N_MOD = 6
EPS = 1e-6

LANES = 128
MXU_TILE = 256

TM = 512
TM_IN = 1024
TM_MLP = 1024
HALO = 8
FF_CHUNK = 1024
VMEM_LIMIT = 56 * 1024 * 1024

DFT_A = 32
DFT_B = SEQ // DFT_A
SUB = 8

QCOLS = 16
NGRP = GRID_W // QCOLS
KCOLS = 32
KC0 = tuple(int(np.clip(QCOLS * g - 8, 0, GRID_W - KCOLS)) for g in range(NGRP))
RB = 8
NRB = GRID_ROWS // RB
WROWS = 16
KROWS_PER_TILE = LANES // KCOLS
N_RM = WIN_ROWS + KROWS_PER_TILE + 1
NEG = -1e30
LOG2E = float(np.log2(np.e))
N_BT = 2 * WIN_ROWS - 1 + KROWS_PER_TILE - 1


def _cparams(n_axes):
    return pltpu.CompilerParams(dimension_semantics=("arbitrary",) * n_axes,
                                vmem_limit_bytes=VMEM_LIMIT)


def _const_spec(shape):
    nd = len(shape)
    return pl.BlockSpec(shape, lambda *_: (0,) * nd)


def _resident_spec(shape):
    nd = len(shape)
    return pl.BlockSpec(shape, lambda *_: (0,) * nd, pipeline_mode=pl.Buffered(1))


def _cast_jobs(jobs, n_steps, step_of):
    in_specs, out_specs, out_shapes = [], [], []
    for arr, layer in jobs:
        _, rows, cols = arr.shape
        chunk = rows // n_steps
        assert chunk * n_steps == rows and chunk % 16 == 0
        in_specs.append(pl.BlockSpec((None, chunk, cols), lambda *ids, layer=layer: (layer, step_of(*ids), 0)))
        out_specs.append(pl.BlockSpec((chunk, cols), lambda *ids: (step_of(*ids), 0)))
        out_shapes.append(jax.ShapeDtypeStruct((rows, cols), BF16))
    return in_specs, out_specs, out_shapes


def _with_casts(body, n_in, n_out, n_jobs):
    def wrapped(*refs):
        ins, refs = refs[:n_in], refs[n_in:]
        cast_ins, refs = refs[:n_jobs], refs[n_jobs:]
        outs, refs = refs[:n_out], refs[n_out:]
        cast_outs, scratch = refs[:n_jobs], refs[n_jobs:]
        for src, dst in zip(cast_ins, cast_outs):
            dst[...] = src[...].astype(BF16)
        body(*ins, *outs, *scratch)
    return wrapped


def _bf16_const(a):
    return jnp.asarray(a, F32).astype(BF16)


def _rms_mod(x, g, shift, scale):
    ms = jnp.mean(x * x, axis=-1, keepdims=True)
    y = x * lax.rsqrt(ms + EPS)
    return (y * g) * (1.0 + scale) + shift


def _mod_rows(mod_ref, row, ks):
    return [mod_ref[pl.ds(row, 1), k * D:(k + 1) * D] for k in ks]


def _mlp(x1, g, shift, scale, w1_ref, w2_ref):
    h = _rms_mod(x1, g, shift, scale).astype(BF16)
    acc = jnp.zeros(x1.shape, F32)
    for c in range(D_FF // FF_CHUNK):
        a = jnp.dot(h, w1_ref[:, c * FF_CHUNK:(c + 1) * FF_CHUNK], preferred_element_type=F32)
        a = jnp.maximum(a, 0.0)
        a = (a * a).astype(BF16)
        acc = acc + jnp.dot(a, w2_ref[c * FF_CHUNK:(c + 1) * FF_CHUNK, :], preferred_element_type=F32)
    return acc


def _mod_body(cond_ref, w_ref, b_ref, o_ref):
    c = cond_ref[...]
    s = c * jax.nn.sigmoid(c)
    o_ref[...] = jnp.dot(s.astype(BF16), w_ref[...].astype(BF16), preferred_element_type=F32) + b_ref[...]


MOD_TILE = 768


def _modulation(cond8, ada_w, ada_b, casts=()):
    nt = N_MOD * D // MOD_TILE
    c_in, c_out, c_shape = _cast_jobs(casts, DEPTH * nt, lambda l, n: l * nt + n)
    return pl.pallas_call(
        _with_casts(_mod_body, 3, 1, len(casts)),
        grid=(DEPTH, nt),
        in_specs=[_const_spec((8, D)),
                  pl.BlockSpec((None, D, MOD_TILE), lambda l, n: (l, 0, n)),
                  pl.BlockSpec((None, 1, MOD_TILE), lambda l, n: (l, 0, n))] + c_in,
        out_specs=[pl.BlockSpec((None, 8, MOD_TILE), lambda l, n: (l, 0, n))] + c_out,
        out_shape=[jax.ShapeDtypeStruct((DEPTH, 8, N_MOD * D), F32)] + c_shape,
        compiler_params=_cparams(2),
        name="adaln_mod",
    )(cond8, ada_w, ada_b.reshape(DEPTH, 1, N_MOD * D), *[a for a, _ in casts])


def _inproj_body(mod_row, seq_len, tm, xp_ref, x_ref, xn_ref, mod_ref, g_ref, w_ref, cw_ref,
                 ay_ref, f_ref):
    b = pl.program_id(0)
    t = pl.program_id(1)
    row = b if mod_row is None else mod_row
    xx = jnp.concatenate([xp_ref[...], x_ref[...], xn_ref[...]], axis=0)
    shift, scale = _mod_rows(mod_ref, row, (0, 1))
    h = _rms_mod(xx, g_ref[...], shift, scale).astype(BF16)
    u = jnp.dot(h, w_ref[...], preferred_element_type=F32)
    z = u[:, CONV_CH:2 * CONV_CH] * u[:, 0:CONV_CH]
    n = t * tm - HALO + lax.broadcasted_iota(jnp.int32, (tm + 2 * HALO, 1), 0)
    z = jnp.where((n >= 0) & (n < seq_len), z, 0.0)
    cw = cw_ref[...]
    zc = (z[HALO - 1:HALO - 1 + tm] * cw[0:1] + z[HALO:HALO + tm] * cw[1:2]
          + z[HALO + 1:HALO + 1 + tm] * cw[2:3])
    ay_ref[...] = (u[HALO:HALO + tm, 2 * CONV_CH:3 * CONV_CH] * zc).astype(BF16)
    f_ref[...] = u[HALO:HALO + tm, 3 * CONV_CH:]


def _inproj(x, mod, g, w_in_bf, conv_w8, mod_row, tm, casts=()):
    bsz, seq_len, _ = x.shape
    nt = seq_len // tm
    nb8 = seq_len // HALO
    r8 = tm // HALO
    c_in, c_out, c_shape = _cast_jobs(casts, bsz * nt, lambda b, t: b * nt + t)
    body = _with_casts(functools.partial(_inproj_body, mod_row, seq_len, tm), 7, 2, len(casts))
    return pl.pallas_call(
        body,
        grid=(bsz, nt),
        in_specs=[pl.BlockSpec((None, HALO, D), lambda b, t: (b, jnp.maximum(t * r8 - 1, 0), 0)),
                  pl.BlockSpec((None, tm, D), lambda b, t: (b, t, 0)),
                  pl.BlockSpec((None, HALO, D), lambda b, t: (b, jnp.minimum((t + 1) * r8, nb8 - 1), 0)),
                  _const_spec((8, N_MOD * D)),
                  _const_spec((1, D)),
                  _const_spec((D, 3 * CONV_CH + FOUR_CH)),
                  _const_spec((8, CONV_CH))] + c_in,
        out_specs=[pl.BlockSpec((None, tm, CONV_CH), lambda b, t: (b, t, 0)),
                   pl.BlockSpec((None, tm, FOUR_CH), lambda b, t: (b, t, 0))] + c_out,
        out_shape=[jax.ShapeDtypeStruct((bsz, seq_len, CONV_CH), BF16),
                   jax.ShapeDtypeStruct((bsz, seq_len, FOUR_CH), F32)] + c_shape,
        compiler_params=_cparams(2),
        name="mixer_in",
    )(x, x, x, mod, g, w_in_bf, conv_w8, *[a for a, _ in casts])


def _dft_consts():
    a = np.arange(DFT_A)
    s = np.arange(SUB)
    ang = 2.0 * np.pi * np.outer(a, a) / DFT_A
    eye = np.eye(SUB)
    re = np.einsum("va,ts->vtas", np.cos(ang), eye).reshape(DFT_A * SUB, DFT_A * SUB)
    im = np.einsum("va,ts->vtas", -np.sin(ang), eye).reshape(DFT_A * SUB, DFT_A * SUB)
    l1 = np.concatenate([re, im], axis=0)
    m = np.arange(DFT_B // SUB)
    bb = (SUB * m[:, None, None] + s[None, None, :])
    tang = 2.0 * np.pi * a[None, :, None] * bb / SEQ
    twr = np.cos(tang).reshape(DFT_B // SUB, DFT_A * SUB, 1)
    twi = (-np.sin(tang)).reshape(DFT_B // SUB, DFT_A * SUB, 1)
    u = np.arange(DFT_B)
    ang2 = 2.0 * np.pi * np.outer(u, u) / DFT_B
    c2, s2 = np.cos(ang2), np.sin(ang2)
    l2 = np.block([[c2, s2], [s2, -c2]])
    return l1, twr, twi, l2


DFT_MPAIR = 4
DFT_VB = 4
DFT_NA = DFT_B // SUB // DFT_MPAIR
DFT_NB = DFT_A // DFT_VB


def _channel_dft(pq, n, mix_ref):
    w = MXU_TILE
    outs = []
    for cb in range(FOUR_CH // w):
        cols = slice(cb * w, (cb + 1) * w)
        lhs = jnp.concatenate([pq[:n, cols], pq[n:, cols]], axis=1).astype(BF16)
        rhs = jnp.concatenate([mix_ref[cols, cols],
                               mix_ref[FOUR_CH + cb * w:FOUR_CH + (cb + 1) * w, cols]], axis=0)
        outs.append(jnp.dot(lhs, rhs, preferred_element_type=F32))
    return jnp.concatenate(outs, axis=1).astype(BF16)


def _dft_body(x_ref, l1_ref, twr_ref, twi_ref, l2_ref, mix_ref, fy_ref, br_scr, bi_scr):
    t = pl.program_id(1)
    rows = DFT_A * SUB
    reps = FOUR_CH // LANES

    @pl.when(t < DFT_NA)
    def _stage_a():
        x = x_ref[...]
        brs, bis = [], []
        for j in range(DFT_MPAIR):
            xj = x[:, j * SUB:(j + 1) * SUB, :].reshape(rows, FOUR_CH).astype(BF16)
            a = jnp.dot(l1_ref[...], xj, preferred_element_type=F32)
            ar, ai = a[:rows], a[rows:]
            twr = jnp.concatenate([twr_ref[j]] * reps, axis=1)
            twi = jnp.concatenate([twi_ref[j]] * reps, axis=1)
            brs.append((ar * twr - ai * twi).reshape(DFT_A, SUB, FOUR_CH))
            bis.append((ar * twi + ai * twr).reshape(DFT_A, SUB, FOUR_CH))
        shape = (DFT_A, 1, DFT_MPAIR * SUB, FOUR_CH)
        br_scr[:, pl.ds(t, 1)] = jnp.concatenate(brs, axis=1).astype(BF16).reshape(shape)
        bi_scr[:, pl.ds(t, 1)] = jnp.concatenate(bis, axis=1).astype(BF16).reshape(shape)

    @pl.when(t >= DFT_NA)
    def _stage_b():
        v0 = (t - DFT_NA) * DFT_VB
        for j in range(DFT_VB):
            br = br_scr[v0 + j].reshape(DFT_B, FOUR_CH)
            bi = bi_scr[v0 + j].reshape(DFT_B, FOUR_CH)
            pq = jnp.dot(l2_ref[...], jnp.concatenate([br, bi], axis=0),
                         preferred_element_type=F32)
            fy_ref[j] = _channel_dft(pq, DFT_B, mix_ref)


def _dft_2d(f, mix_bf, casts=()):
    bsz = f.shape[0]
    n_steps = DFT_NA + DFT_NB
    c_in, c_out, c_shape = _cast_jobs(casts, bsz * n_steps, lambda b, t: b * n_steps + t)
    l1, twr, twi, l2 = _dft_consts()
    nm = DFT_B // SUB
    rows = DFT_A * SUB
    twr_b = jnp.asarray(np.broadcast_to(twr, (nm, rows, LANES)), F32)
    twi_b = jnp.asarray(np.broadcast_to(twi, (nm, rows, LANES)), F32)
    f5 = f.reshape(bsz, DFT_A, DFT_NA, DFT_MPAIR * SUB, FOUR_CH)
    a_step = lambda t: jnp.minimum(t, DFT_NA - 1)
    b_step = lambda t: jnp.maximum(t - DFT_NA, 0)
    tw_spec = pl.BlockSpec((DFT_MPAIR, rows, LANES), lambda b, t: (a_step(t), 0, 0))
    scr = pltpu.VMEM((DFT_A, DFT_NA, DFT_MPAIR * SUB, FOUR_CH), BF16)
    fy, *cast_out = pl.pallas_call(
        _with_casts(_dft_body, 6, 1, len(casts)),
        grid=(bsz, n_steps),
        in_specs=[pl.BlockSpec((None, DFT_A, None, DFT_MPAIR * SUB, FOUR_CH),
                               lambda b, t: (b, 0, a_step(t), 0, 0)),
                  _const_spec((2 * rows, rows)), tw_spec, tw_spec,
                  _const_spec((2 * DFT_B, 2 * DFT_B)), _const_spec((2 * FOUR_CH, FOUR_CH))] + c_in,
        out_specs=[pl.BlockSpec((None, DFT_VB, DFT_B, FOUR_CH), lambda b, t: (b, b_step(t), 0, 0))] + c_out,
        out_shape=[jax.ShapeDtypeStruct((bsz, DFT_A, DFT_B, FOUR_CH), BF16)] + c_shape,
        scratch_shapes=[scr, scr],
        compiler_params=_cparams(2),
        name="dft_2d",
    )(f5, _bf16_const(l1), twr_b, twi_b, _bf16_const(l2), mix_bf, *[a for a, _ in casts])
    return [jnp.transpose(fy, (0, 2, 1, 3)).reshape(bsz, SEQ, FOUR_CH)] + cast_out


def _dft_small_body(f_ref, lc_ref, mix_ref, fy_ref):
    n = f_ref.shape[0]
    pq = jnp.dot(lc_ref[...], f_ref[...].astype(BF16), preferred_element_type=F32)
    fy_ref[...] = _channel_dft(pq, n, mix_ref)


def _dft_2d_small(f, mix_bf):
    bsz, n, _ = f.shape
    k = np.arange(n)
    ang = 2.0 * np.pi * np.outer(k, k) / n
    lc = np.concatenate([np.cos(ang), np.sin(ang)], axis=0)
    spec = pl.BlockSpec((None, n, FOUR_CH), lambda b: (b, 0, 0))
    return pl.pallas_call(
        _dft_small_body,
        grid=(bsz,),
        in_specs=[spec, _const_spec((2 * n, n)), _const_spec((2 * FOUR_CH, FOUR_CH))],
        out_specs=spec,
        out_shape=jax.ShapeDtypeStruct((bsz, n, FOUR_CH), BF16),
        compiler_params=_cparams(1),
        name="dft_small",
    )(f, _bf16_const(lc), mix_bf)


def _channel_mix_const(seq_len):
    k = np.arange(FOUR_GROUP)
    ang = 2.0 * np.pi * np.outer(k, k) / FOUR_GROUP
    ng = FOUR_CH // FOUR_GROUP
    bdc = np.kron(np.eye(ng), np.cos(ang))
    bds = np.kron(np.eye(ng), np.sin(ang))
    return np.concatenate([bdc, -bds], axis=0) / np.sqrt(seq_len * FOUR_GROUP)


def _mixout_mlp_body(mod_row, x_ref, ay_ref, fy_ref, mod_ref, g2_ref, wout_ref, w1_ref, w2_ref, o_ref):
    b = pl.program_id(0)
    row = b if mod_row is None else mod_row
    gate1, shift2, scale2, gate2 = _mod_rows(mod_ref, row, (2, 3, 4, 5))
    cat = jnp.concatenate([ay_ref[...], fy_ref[...]], axis=1)
    mo = jnp.dot(cat, wout_ref[...], preferred_element_type=F32)
    x1 = x_ref[...] + gate1 * mo
    o_ref[...] = x1 + gate2 * _mlp(x1, g2_ref[...], shift2, scale2, w1_ref, w2_ref)


def _mixout_mlp(x, ay, fy, mod, g2, wout_bf, w1_bf, w2_bf, mod_row, tm, casts=()):
    bsz, seq_len, _ = x.shape
    nt = seq_len // tm
    row_spec = lambda w: pl.BlockSpec((None, tm, w), lambda b, t: (b, t, 0))
    c_in, c_out, c_shape = _cast_jobs(casts, bsz * nt, lambda b, t: b * nt + t)
    body = _with_casts(functools.partial(_mixout_mlp_body, mod_row), 8, 1, len(casts))
    return pl.pallas_call(
        body,
        grid=(bsz, nt),
        in_specs=[row_spec(D), row_spec(CONV_CH), row_spec(FOUR_CH),
                  _const_spec((8, N_MOD * D)), _const_spec((1, D)),
                  _resident_spec((D, D)), _resident_spec((D, D_FF)), _resident_spec((D_FF, D))] + c_in,
        out_specs=[row_spec(D)] + c_out,
        out_shape=[jax.ShapeDtypeStruct((bsz, seq_len, D), F32)] + c_shape,
        compiler_params=_cparams(2),
        name="mixer_out_mlp",
    )(x, ay, fy, mod, g2, wout_bf, w1_bf, w2_bf, *[a for a, _ in casts])


def _head_rms(t, bd_ref, gt):
    tt = (t * t).astype(BF16)
    w = bd_ref.shape[0]
    ms = jnp.concatenate(
        [jnp.dot(tt[:, j * w:(j + 1) * w], bd_ref[...], preferred_element_type=F32)
         for j in range(D // w)], axis=1)
    return t * lax.rsqrt(ms + EPS) * gt


def _qkv_common(mod_row, x_ref, mod_ref, g_ref, w_ref, bd_ref, qg_ref, kg_ref):
    b = pl.program_id(0)
    row = b if mod_row is None else mod_row
    shift, scale = _mod_rows(mod_ref, row, (0, 1))
    h = _rms_mod(x_ref[...], g_ref[...], shift, scale).astype(BF16)
    qkv = jnp.dot(h, w_ref[...], preferred_element_type=F32)
    q = _head_rms(qkv[:, 0:D], bd_ref, qg_ref[...]) * (HEAD_DIM ** -0.5 * LOG2E)
    k = _head_rms(qkv[:, D:2 * D], bd_ref, kg_ref[...])
    v = qkv[:, 2 * D:]
    return q, k, v


def _qkv_grid_body(x_ref, mod_ref, g_ref, w_ref, bd_ref, qg_ref, kg_ref, q_ref, k_ref, v_ref):
    q, k, v = _qkv_common(None, x_ref, mod_ref, g_ref, w_ref, bd_ref, qg_ref, kg_ref)
    for rho in range(TM // GRID_W):
        for g in range(NGRP):
            q0 = rho * GRID_W + QCOLS * g
            q_ref[g, rho * QCOLS:(rho + 1) * QCOLS, :] = q[q0:q0 + QCOLS].astype(BF16)
            k0 = rho * GRID_W + KC0[g]
            k_ref[g, rho * KCOLS:(rho + 1) * KCOLS, :] = k[k0:k0 + KCOLS].astype(BF16)
            v_ref[g, rho * KCOLS:(rho + 1) * KCOLS, :] = v[k0:k0 + KCOLS].astype(BF16)


def _qkv_ctx_body(mod_row, x_ref, mod_ref, g_ref, w_ref, bd_ref, qg_ref, kg_ref, k_ref, v_ref):
    _, k, v = _qkv_common(mod_row, x_ref, mod_ref, g_ref, w_ref, bd_ref, qg_ref, kg_ref)
    k_ref[...] = k.astype(BF16)
    v_ref[...] = v.astype(BF16)


def _qkv_consts(q_g, k_g):
    bd = np.kron(np.eye(MXU_TILE // HEAD_DIM), np.ones((HEAD_DIM, HEAD_DIM))) / HEAD_DIM
    qg = jnp.tile(q_g, N_HEADS).reshape(1, D)
    kg = jnp.tile(k_g, N_HEADS).reshape(1, D)
    return _bf16_const(bd), qg, kg


def _qkv_in_specs(tm):
    return [pl.BlockSpec((None, tm, D), lambda b, t: (b, t, 0)),
            _const_spec((8, N_MOD * D)), _const_spec((1, D)), _const_spec((D, 3 * D)),
            _const_spec((MXU_TILE, MXU_TILE)), _const_spec((1, D)), _const_spec((1, D))]


def _qkv_grid(x, mod, g, wqkv_bf, q_g, k_g):
    bsz = x.shape[0]
    bd, qg, kg = _qkv_consts(q_g, k_g)
    nq = TM // GRID_W * QCOLS
    nk = TM // GRID_W * KCOLS
    return pl.pallas_call(
        _qkv_grid_body,
        grid=(bsz, SEQ // TM),
        in_specs=_qkv_in_specs(TM),
        out_specs=[pl.BlockSpec((None, NGRP, nq, D), lambda b, t: (b, 0, t, 0)),
                   pl.BlockSpec((None, NGRP, nk, D), lambda b, t: (b, 0, t, 0)),
                   pl.BlockSpec((None, NGRP, nk, D), lambda b, t: (b, 0, t, 0))],
        out_shape=[jax.ShapeDtypeStruct((bsz, NGRP, GRID_ROWS * QCOLS, D), BF16),
                   jax.ShapeDtypeStruct((bsz, NGRP, GRID_ROWS * KCOLS, D), BF16),
                   jax.ShapeDtypeStruct((bsz, NGRP, GRID_ROWS * KCOLS, D), BF16)],
        compiler_params=_cparams(2),
        name="qkv_grid",
    )(x, mod, g, wqkv_bf, bd, qg, kg)


def _qkv_ctx(ctx, mod, g, wqkv_bf, q_g, k_g, mod_row):
    bsz, n, _ = ctx.shape
    bd, qg, kg = _qkv_consts(q_g, k_g)
    spec = pl.BlockSpec((None, n, D), lambda b, t: (b, t, 0))
    return pl.pallas_call(
        functools.partial(_qkv_ctx_body, mod_row),
        grid=(bsz, 1),
        in_specs=_qkv_in_specs(n),
        out_specs=[spec, spec],
        out_shape=[jax.ShapeDtypeStruct((bsz, n, D), BF16)] * 2,
        compiler_params=_cparams(2),
        name="qkv_ctx",
    )(ctx, mod, g, wqkv_bf, bd, qg, kg)


def _bias_tables(rpb):
    n_ro = 2 * WIN_ROWS - 1
    n_co = 2 * WIN_COLS - 1
    g = np.arange(NGRP)[:, None, None]
    cq = np.arange(QCOLS)[None, :, None]
    kcw = (np.arange(LANES) % KCOLS)[None, None, :]
    c = QCOLS * g + cq
    kc = np.asarray(KC0)[:, None, None] + kcw
    cs = np.clip(c - WIN_COLS // 2, 0, GRID_W - WIN_COLS)
    col_ok = (kc >= cs) & (kc < cs + WIN_COLS)
    co = np.where(col_ok, kc - c + (WIN_COLS - 1), -1)
    onehot = (np.arange(n_co)[:, None, None, None] == co[None]).astype(np.float32)
    n_pad = n_ro + 2 + KROWS_PER_TILE
    rpb_p = jnp.pad(rpb * LOG2E, ((0, 0), (3, KROWS_PER_TILE - 1), (0, 0)))
    sel = jnp.einsum("hrc,cn->hrn", rpb_p, jnp.asarray(onehot.reshape(n_co, -1)),
                     precision=lax.Precision.HIGHEST)
    sel = sel.reshape(N_HEADS, n_pad, NGRP, QCOLS, LANES).transpose(0, 2, 1, 3, 4)
    rt = jnp.where(col_ok[None, :, None], sel, NEG)
    rt = rt.reshape(N_HEADS // 2, 2, NGRP, n_pad, QCOLS, LANES)
    delta = np.arange(N_RM)[:, None, None] - WIN_ROWS
    jj2 = (np.arange(LANES) // KCOLS)[None, None, :]
    rm = np.where((jj2 >= delta) & (jj2 < delta + WIN_ROWS), 0.0, NEG)
    rm = np.broadcast_to(rm, (N_RM, QCOLS, LANES)).astype(np.float32)
    return rt.astype(F32), jnp.asarray(rm)


def _tile_indices(rb, i, jq):
    wr = int(np.clip(RB * rb - WIN_ROWS // 2, 0, GRID_ROWS - WROWS))
    r = RB * rb + i
    delta = int(np.clip(r - WIN_ROWS // 2, 0, GRID_ROWS - WIN_ROWS)) - wr - KROWS_PER_TILE * jq
    if not -WIN_ROWS < delta < KROWS_PER_TILE:
        return None
    bti = wr + KROWS_PER_TILE * jq - r + (WIN_ROWS - 1) + 3
    assert 0 <= bti < N_BT and 0 <= delta + WIN_ROWS < N_RM
    return bti, delta + WIN_ROWS


def _attn_body(q_ref, k_ref, v_ref, kc_ref, vc_ref, rt_ref, rm_ref, o_ref, s_scr, bt_scr, bm_scr):
    kc = kc_ref[...]
    vc = vc_ref[...]
    nq = RB * QCOLS
    nk = WROWS * KCOLS
    lane = lax.broadcasted_iota(jnp.int32, (nq, LANES), 1)
    nt = nk // LANES
    nc = CTX // LANES
    dn = (((1,), (1,)), ((), ()))

    @pl.when(pl.program_id(2) == 0)
    def _build_tables():
        lgrp = lax.broadcasted_iota(jnp.int32, (QCOLS, LANES), 1) // KCOLS
        for e in range(2):
            for t in range(N_BT):
                blk = rt_ref[e, t + KROWS_PER_TILE - 1]
                for jj in range(KROWS_PER_TILE - 2, -1, -1):
                    blk = jnp.where(lgrp == jj, rt_ref[e, t + jj], blk)
                bt_scr[e, t] = blk
        for e in range(2):
            for i in range(RB):
                for jq in range(nt):
                    idx = _tile_indices(1, i, jq)
                    if idx is not None:
                        bm_scr[e, i * nt + jq] = bt_scr[e, idx[0]] + rm_ref[idx[1]]

    def window_row(rb):
        return jnp.clip(RB * rb - WIN_ROWS // 2, 0, GRID_ROWS - WROWS)

    def scores(rb, slot):
        rb = jnp.asarray(rb, jnp.int32)
        koff = pl.multiple_of(window_row(rb) * KCOLS, LANES)
        kw = k_ref[pl.ds(koff, nk), :]
        q = q_ref[pl.ds(pl.multiple_of(rb * nq, nq), nq), :]
        for e in range(2):
            in_head = (lane >= HEAD_DIM * e) & (lane < HEAD_DIM * (e + 1))
            qm = jnp.where(in_head, q, jnp.zeros_like(q))
            s_scr[slot, e, :, 0:nk] = lax.dot_general(qm, kw, dn, preferred_element_type=F32)
            s_scr[slot, e, :, nk:] = lax.dot_general(qm, kc, dn, preferred_element_type=F32)

    def finish(rb, slot, edge_rb):
        rb = jnp.asarray(rb, jnp.int32)
        wr = window_row(rb)
        vw = v_ref[pl.ds(pl.multiple_of(wr * KCOLS, LANES), nk), :]
        zero_tile = jnp.zeros((QCOLS, LANES), F32)
        outs = []
        for e in range(2):
            p_rows = []
            l_rows = []
            for i in range(RB):
                rows = slice(i * QCOLS, (i + 1) * QCOLS)
                blks = {}
                for jq in range(nt):
                    idx = _tile_indices(1 if edge_rb is None else edge_rb, i, jq)
                    if idx is None:
                        continue
                    add = bm_scr[e, i * nt + jq] if edge_rb is None else bt_scr[e, idx[0]] + rm_ref[idx[1]]
                    blks[jq] = s_scr[slot, e, rows, jq * LANES:(jq + 1) * LANES] + add
                for jc in range(nc):
                    blks[nt + jc] = s_scr[slot, e, rows, nk + jc * LANES:nk + (jc + 1) * LANES]
                vals = list(blks.values())
                m = jnp.max(functools.reduce(jnp.maximum, vals), axis=-1, keepdims=True)
                ps = {j: jnp.exp2(sb - m) for j, sb in blks.items()}
                l_rows.append(jnp.sum(functools.reduce(jnp.add, list(ps.values())), axis=-1, keepdims=True))
                p_rows.append(jnp.concatenate([ps.get(j, zero_tile) for j in range(nt + nc)], axis=1))
            p = jnp.concatenate(p_rows, axis=0).astype(BF16)
            lsum = jnp.concatenate(l_rows, axis=0)
            o = (jnp.dot(p[:, :nk], vw, preferred_element_type=F32)
                 + jnp.dot(p[:, nk:], vc, preferred_element_type=F32))
            outs.append(o / lsum)
        o = jnp.where(lane < HEAD_DIM, outs[0], outs[1])
        o_ref[pl.ds(pl.multiple_of(rb * nq, nq), nq), :] = o.astype(BF16)

    scores(0, 0)
    scores(1, 1)
    finish(0, 0, 0)
    scores(2, 0)
    finish(1, 1, None)

    def pair(tt, carry):
        t0 = 2 * tt
        scores(t0 + 1, 1)
        finish(t0, 0, None)
        scores(t0 + 2, 0)
        finish(t0 + 1, 1, None)
        return carry

    lax.fori_loop(1, NRB // 2 - 1, pair, 0, unroll=True)
    scores(NRB - 1, 1)
    finish(NRB - 2, 0, None)
    finish(NRB - 1, 1, NRB - 1)


def _attention(qcb, kcb, vcb, kc, vc, bt, rm, casts=()):
    bsz = qcb.shape[0]
    nhp = N_HEADS // 2
    n_t = bt.shape[3]
    c_in, c_out, c_shape = _cast_jobs(casts, nhp * NGRP * bsz, lambda h, g, b: (h * NGRP + g) * bsz + b)
    return pl.pallas_call(
        _with_casts(_attn_body, 7, 1, len(casts)),
        grid=(nhp, NGRP, bsz),
        in_specs=[pl.BlockSpec((None, None, GRID_ROWS * QCOLS, LANES), lambda h, g, b: (b, g, 0, h)),
                  pl.BlockSpec((None, None, GRID_ROWS * KCOLS, LANES), lambda h, g, b: (b, g, 0, h)),
                  pl.BlockSpec((None, None, GRID_ROWS * KCOLS, LANES), lambda h, g, b: (b, g, 0, h)),
                  pl.BlockSpec((None, CTX, LANES), lambda h, g, b: (b, 0, h)),
                  pl.BlockSpec((None, CTX, LANES), lambda h, g, b: (b, 0, h)),
                  pl.BlockSpec((None, 2, None, n_t, QCOLS, LANES), lambda h, g, b: (h, 0, g, 0, 0, 0)),
                  _const_spec((N_RM, QCOLS, LANES))] + c_in,
        out_specs=[pl.BlockSpec((None, None, GRID_ROWS * QCOLS, LANES), lambda h, g, b: (b, g, 0, h))] + c_out,
        out_shape=[jax.ShapeDtypeStruct((bsz, NGRP, GRID_ROWS * QCOLS, D), BF16)] + c_shape,
        scratch_shapes=[pltpu.VMEM((2, 2, RB * QCOLS, WROWS * KCOLS + CTX), F32),
                        pltpu.VMEM((2, N_BT, QCOLS, LANES), F32),
                        pltpu.VMEM((2, RB * WROWS * KCOLS // LANES, QCOLS, LANES), F32)],
        compiler_params=_cparams(3),
        name="nbr_attention",
    )(qcb, kcb, vcb, kc, vc, bt, rm, *[a for a, _ in casts])


def _attnout_mlp_body(x_ref, o_ref_in, mod_ref, g2_ref, wo_ref, w1_ref, w2_ref, out_ref):
    b = pl.program_id(0)
    gate1, shift2, scale2, gate2 = _mod_rows(mod_ref, b, (2, 3, 4, 5))
    chunks = [o_ref_in[g, rho * QCOLS:(rho + 1) * QCOLS, :]
              for rho in range(TM_MLP // GRID_W) for g in range(NGRP)]
    o_nat = jnp.concatenate(chunks, axis=0)
    mo = jnp.dot(o_nat, wo_ref[...], preferred_element_type=F32)
    x1 = x_ref[...] + gate1 * mo
    out_ref[...] = x1 + gate2 * _mlp(x1, g2_ref[...], shift2, scale2, w1_ref, w2_ref)


def _attnout_mlp(x, ocb, mod, g2, wo_bf, w1_bf, w2_bf):
    bsz = x.shape[0]
    nq = TM_MLP // GRID_W * QCOLS
    return pl.pallas_call(
        _attnout_mlp_body,
        grid=(bsz, SEQ // TM_MLP),
        in_specs=[pl.BlockSpec((None, TM_MLP, D), lambda b, t: (b, t, 0)),
                  pl.BlockSpec((None, NGRP, nq, D), lambda b, t: (b, 0, t, 0)),
                  _const_spec((8, N_MOD * D)), _const_spec((1, D)),
                  _resident_spec((D, D)), _resident_spec((D, D_FF)), _resident_spec((D_FF, D))],
        out_specs=pl.BlockSpec((None, TM_MLP, D), lambda b, t: (b, t, 0)),
        out_shape=jax.ShapeDtypeStruct((bsz, SEQ, D), F32),
        compiler_params=_cparams(2),
        name="attn_out_mlp",
    )(x, ocb, mod, g2, wo_bf, w1_bf, w2_bf)


def kernel(x, c, ctx, c_ctx, ada_w, ada_b, norm_mix_g, norm_mlp_g, mlp_w1, mlp_w2, ab_w_in, ab_conv_w,
           ab_w_out, na_w_qkv, na_q_g, na_k_g, na_rpb, na_w_out):
    bsz = x.shape[0]
    ctx_row = bsz
    cond8 = jnp.zeros((8, D), F32).at[:bsz].set(c).at[ctx_row].set(c_ctx)
    mods, w_in = _modulation(cond8, ada_w, ada_b, casts=((ab_w_in, 0),))

    g_mix = norm_mix_g.reshape(DEPTH, 1, D)
    g_mlp = norm_mlp_g.reshape(DEPTH, 1, D)
    conv_w8 = jnp.zeros((8, CONV_CH), F32).at[:3].set(ab_conv_w[0])

    ay, f, w1_0, w2_0, w_out0, wqkv, w1_1, w2_1, wo = _inproj(
        x, mods[0], g_mix[0], w_in, conv_w8, None, TM_IN,
        casts=((mlp_w1, 0), (mlp_w2, 0), (ab_w_out, 0), (na_w_qkv, 0), (mlp_w1, 1), (mlp_w2, 1), (na_w_out, 0)))
    fy, = _dft_2d(f, _bf16_const(_channel_mix_const(SEQ)))
    x, = _mixout_mlp(x, ay, fy, mods[0], g_mlp[0], w_out0, w1_0, w2_0, None, TM_MLP)

    ay_c, f_c = _inproj(ctx, mods[0], g_mix[0], w_in, conv_w8, ctx_row, CTX)
    fy_c = _dft_2d_small(f_c, _bf16_const(_channel_mix_const(CTX)))
    flat = lambda a: a.reshape(1, bsz * CTX, a.shape[-1])
    ctx, = _mixout_mlp(flat(ctx), flat(ay_c), flat(fy_c), mods[0], g_mlp[0], w_out0, w1_0, w2_0,
                       ctx_row, bsz * CTX)

    qcb, kcb, vcb = _qkv_grid(x, mods[1], g_mix[1], wqkv, na_q_g[0], na_k_g[0])
    kc, vc = _qkv_ctx(ctx, mods[1], g_mix[1], wqkv, na_q_g[0], na_k_g[0], ctx_row)
    kc, vc = kc.reshape(bsz, CTX, D), vc.reshape(bsz, CTX, D)
    bt, rm = _bias_tables(na_rpb[0])
    ocb, = _attention(qcb, kcb, vcb, kc, vc, bt, rm)
    x = _attnout_mlp(x, ocb, mods[1], g_mlp[1], wo, w1_1, w2_1)
    return x
```

```python
import functools

import numpy as np
import jax
import jax.numpy as jnp
from jax import lax
from jax.experimental import pallas as pl
from jax.experimental.pallas import tpu as pltpu

F32 = jnp.float32
BF16 = jnp.bfloat16

D = 1024
DEPTH = 2
SEQ = 8192
CTX = 256
GRID_W = 64
GRID_ROWS = SEQ // GRID_W
HEAD_DIM = 64
N_HEADS = 16
CONV_CH = 512
FOUR_CH = 512
FOUR_GROUP = 64
WIN_ROWS = 8
WIN_COLS = 16
D_FF = 4 * D
N_MOD = 6
EPS = 1e-6

LANES = 128
MXU_TILE = 256

TM = 1024
TM_IN = 1024
TM_MLP = 1024
HALO = 8
FF_CHUNK = 1024
VMEM_LIMIT = 56 * 1024 * 1024

DFT_A = 32
DFT_B = SEQ // DFT_A
SUB = 8

QCOLS = 16
NGRP = GRID_W // QCOLS
KCOLS = 32
KC0 = tuple(int(np.clip(QCOLS * g - 8, 0, GRID_W - KCOLS)) for g in range(NGRP))
RB = 8
NRB = GRID_ROWS // RB
WROWS = 16
KROWS_PER_TILE = LANES // KCOLS
N_RM = WIN_ROWS + KROWS_PER_TILE + 1
NEG = -1e30
LOG2E = float(np.log2(np.e))
N_BT = 2 * WIN_ROWS - 1 + KROWS_PER_TILE - 1


def _cparams(n_axes):
    return pltpu.CompilerParams(dimension_semantics=("arbitrary",) * n_axes,
                                vmem_limit_bytes=VMEM_LIMIT)


def _const_spec(shape):
    nd = len(shape)
    return pl.BlockSpec(shape, lambda *_: (0,) * nd)


def _resident_spec(shape):
    nd = len(shape)
    return pl.BlockSpec(shape, lambda *_: (0,) * nd, pipeline_mode=pl.Buffered(1))


def _cast_jobs(jobs, n_steps, step_of):
    in_specs, out_specs, out_shapes = [], [], []
    for arr, layer in jobs:
        _, rows, cols = arr.shape
        chunk = rows // n_steps
        assert chunk * n_steps == rows and chunk % 16 == 0
        in_specs.append(pl.BlockSpec((None, chunk, cols), lambda *ids, layer=layer: (layer, step_of(*ids), 0)))
        out_specs.append(pl.BlockSpec((chunk, cols), lambda *ids: (step_of(*ids), 0)))
        out_shapes.append(jax.ShapeDtypeStruct((rows, cols), BF16))
    return in_specs, out_specs, out_shapes


def _with_casts(body, n_in, n_out, n_jobs):
    def wrapped(*refs):
        ins, refs = refs[:n_in], refs[n_in:]
        cast_ins, refs = refs[:n_jobs], refs[n_jobs:]
        outs, refs = refs[:n_out], refs[n_out:]
        cast_outs, scratch = refs[:n_jobs], refs[n_jobs:]
        for src, dst in zip(cast_ins, cast_outs):
            dst[...] = src[...].astype(BF16)
        body(*ins, *outs, *scratch)
    return wrapped


def _bf16_const(a):
    return jnp.asarray(a, F32).astype(BF16)


def _rms_mod(x, g, shift, scale):
    ms = jnp.mean(x * x, axis=-1, keepdims=True)
    y = x * lax.rsqrt(ms + EPS)
    return (y * g) * (1.0 + scale) + shift


def _mod_rows(mod_ref, row, ks):
    return [mod_ref[pl.ds(row, 1), k * D:(k + 1) * D] for k in ks]


def _mlp(x1, g, shift, scale, w1_ref, w2_ref):
    h = _rms_mod(x1, g, shift, scale).astype(BF16)
    acc = jnp.zeros(x1.shape, F32)
    for c in range(D_FF // FF_CHUNK):
        a = jnp.dot(h, w1_ref[:, c * FF_CHUNK:(c + 1) * FF_CHUNK], preferred_element_type=F32)
        a = jnp.maximum(a, 0.0)
        a = (a * a).astype(BF16)
        acc = acc + jnp.dot(a, w2_ref[c * FF_CHUNK:(c + 1) * FF_CHUNK, :], preferred_element_type=F32)
    return acc


def _mod_body(cond_ref, w_ref, b_ref, o_ref):
    c = cond_ref[...]
    s = c * jax.nn.sigmoid(c)
    o_ref[...] = jnp.dot(s.astype(BF16), w_ref[...].astype(BF16), preferred_element_type=F32) + b_ref[...]


MOD_TILE = 3072


def _modulation(cond8, ada_w, ada_b, casts=()):
    nt = N_MOD * D // MOD_TILE
    c_in, c_out, c_shape = _cast_jobs(casts, DEPTH * nt, lambda l, n: l * nt + n)
    return pl.pallas_call(
        _with_casts(_mod_body, 3, 1, len(casts)),
        grid=(DEPTH, nt),
        in_specs=[_const_spec((8, D)),
                  pl.BlockSpec((None, D, MOD_TILE), lambda l, n: (l, 0, n)),
                  pl.BlockSpec((None, 1, MOD_TILE), lambda l, n: (l, 0, n))] + c_in,
        out_specs=[pl.BlockSpec((None, 8, MOD_TILE), lambda l, n: (l, 0, n))] + c_out,
        out_shape=[jax.ShapeDtypeStruct((DEPTH, 8, N_MOD * D), F32)] + c_shape,
        compiler_params=_cparams(2),
        name="adaln_mod",
    )(cond8, ada_w, ada_b.reshape(DEPTH, 1, N_MOD * D), *[a for a, _ in casts])


def _inproj_body(mod_row, seq_len, tm, xp_ref, x_ref, xn_ref, mod_ref, g_ref, w_ref, cw_ref,
                 ay_ref, f_ref):
    b = pl.program_id(0)
    t = pl.program_id(1)
    row = b if mod_row is None else mod_row
    xx = jnp.concatenate([xp_ref[...], x_ref[...], xn_ref[...]], axis=0)
    shift, scale = _mod_rows(mod_ref, row, (0, 1))
    h = _rms_mod(xx, g_ref[...], shift, scale).astype(BF16)
    u = jnp.dot(h, w_ref[...], preferred_element_type=F32)
    z = u[:, CONV_CH:2 * CONV_CH] * u[:, 0:CONV_CH]
    n = t * tm - HALO + lax.broadcasted_iota(jnp.int32, (tm + 2 * HALO, 1), 0)
    z = jnp.where((n >= 0) & (n < seq_len), z, 0.0)
    cw = cw_ref[...]
    zc = (z[HALO - 1:HALO - 1 + tm] * cw[0:1] + z[HALO:HALO + tm] * cw[1:2]
          + z[HALO + 1:HALO + 1 + tm] * cw[2:3])
    ay_ref[...] = (u[HALO:HALO + tm, 2 * CONV_CH:3 * CONV_CH] * zc).astype(BF16)
    f_ref[...] = u[HALO:HALO + tm, 3 * CONV_CH:]


def _inproj(x, mod, g, w_in_bf, conv_w8, mod_row, tm, casts=()):
    bsz, seq_len, _ = x.shape
    nt = seq_len // tm
    nb8 = seq_len // HALO
    r8 = tm // HALO
    c_in, c_out, c_shape = _cast_jobs(casts, bsz * nt, lambda b, t: b * nt + t)
    body = _with_casts(functools.partial(_inproj_body, mod_row, seq_len, tm), 7, 2, len(casts))
    return pl.pallas_call(
        body,
        grid=(bsz, nt),
        in_specs=[pl.BlockSpec((None, HALO, D), lambda b, t: (b, jnp.maximum(t * r8 - 1, 0), 0)),
                  pl.BlockSpec((None, tm, D), lambda b, t: (b, t, 0)),
                  pl.BlockSpec((None, HALO, D), lambda b, t: (b, jnp.minimum((t + 1) * r8, nb8 - 1), 0)),
                  _const_spec((8, N_MOD * D)),
                  _const_spec((1, D)),
                  _const_spec((D, 3 * CONV_CH + FOUR_CH)),
                  _const_spec((8, CONV_CH))] + c_in,
        out_specs=[pl.BlockSpec((None, tm, CONV_CH), lambda b, t: (b, t, 0)),
                   pl.BlockSpec((None, tm, FOUR_CH), lambda b, t: (b, t, 0))] + c_out,
        out_shape=[jax.ShapeDtypeStruct((bsz, seq_len, CONV_CH), BF16),
                   jax.ShapeDtypeStruct((bsz, seq_len, FOUR_CH), F32)] + c_shape,
        compiler_params=_cparams(2),
        name="mixer_in",
    )(x, x, x, mod, g, w_in_bf, conv_w8, *[a for a, _ in casts])


def _dft_consts():
    a = np.arange(DFT_A)
    s = np.arange(SUB)
    ang = 2.0 * np.pi * np.outer(a, a) / DFT_A
    eye = np.eye(SUB)
    re = np.einsum("va,ts->vtas", np.cos(ang), eye).reshape(DFT_A * SUB, DFT_A * SUB)
    im = np.einsum("va,ts->vtas", -np.sin(ang), eye).reshape(DFT_A * SUB, DFT_A * SUB)
    l1 = np.concatenate([re, im], axis=0)
    m = np.arange(DFT_B // SUB)
    bb = (SUB * m[:, None, None] + s[None, None, :])
    tang = 2.0 * np.pi * a[None, :, None] * bb / SEQ
    twr = np.cos(tang).reshape(DFT_B // SUB, DFT_A * SUB, 1)
    twi = (-np.sin(tang)).reshape(DFT_B // SUB, DFT_A * SUB, 1)
    u = np.arange(DFT_B)
    ang2 = 2.0 * np.pi * np.outer(u, u) / DFT_B
    c2, s2 = np.cos(ang2), np.sin(ang2)
    l2 = np.block([[c2, s2], [s2, -c2]])
    return l1, twr, twi, l2


DFT_MPAIR = 16
DFT_VB = 16
DFT_NA = DFT_B // SUB // DFT_MPAIR
DFT_NB = DFT_A // DFT_VB


def _channel_dft(pq, n, mix_ref):
    w = MXU_TILE
    outs = []
    for cb in range(FOUR_CH // w):
        cols = slice(cb * w, (cb + 1) * w)
        lhs = jnp.concatenate([pq[:n, cols], pq[n:, cols]], axis=1).astype(BF16)
        rhs = jnp.concatenate([mix_ref[cols, cols],
                               mix_ref[FOUR_CH + cb * w:FOUR_CH + (cb + 1) * w, cols]], axis=0)
        outs.append(jnp.dot(lhs, rhs, preferred_element_type=F32))
    return jnp.concatenate(outs, axis=1).astype(BF16)


def _dft_body(x_ref, l1_ref, twr_ref, twi_ref, l2_ref, mix_ref, fy_ref, br_scr, bi_scr):
    t = pl.program_id(1)
    rows = DFT_A * SUB
    reps = FOUR_CH // LANES

    @pl.when(t < DFT_NA)
    def _stage_a():
        x = x_ref[...]
        brs, bis = [], []
        for j in range(DFT_MPAIR):
            xj = x[:, j * SUB:(j + 1) * SUB, :].reshape(rows, FOUR_CH).astype(BF16)
            a = jnp.dot(l1_ref[...], xj, preferred_element_type=F32)
            ar, ai = a[:rows], a[rows:]
            twr = jnp.concatenate([twr_ref[j]] * reps, axis=1)
            twi = jnp.concatenate([twi_ref[j]] * reps, axis=1)
            brs.append((ar * twr - ai * twi).reshape(DFT_A, SUB, FOUR_CH))
            bis.append((ar * twi + ai * twr).reshape(DFT_A, SUB, FOUR_CH))
        shape = (DFT_A, 1, DFT_MPAIR * SUB, FOUR_CH)
        br_scr[:, pl.ds(t, 1)] = jnp.concatenate(brs, axis=1).astype(BF16).reshape(shape)
        bi_scr[:, pl.ds(t, 1)] = jnp.concatenate(bis, axis=1).astype(BF16).reshape(shape)

    @pl.when(t >= DFT_NA)
    def _stage_b():
        v0 = (t - DFT_NA) * DFT_VB
        for j in range(DFT_VB):
            br = br_scr[v0 + j].reshape(DFT_B, FOUR_CH)
            bi = bi_scr[v0 + j].reshape(DFT_B, FOUR_CH)
            pq = jnp.dot(l2_ref[...], jnp.concatenate([br, bi], axis=0),
                         preferred_element_type=F32)
            fy_ref[j] = _channel_dft(pq, DFT_B, mix_ref)


def _dft_2d(f, mix_bf, casts=()):
    bsz = f.shape[0]
    n_steps = DFT_NA + DFT_NB
    c_in, c_out, c_shape = _cast_jobs(casts, bsz * n_steps, lambda b, t: b * n_steps + t)
    l1, twr, twi, l2 = _dft_consts()
    nm = DFT_B // SUB
    rows = DFT_A * SUB
    twr_b = jnp.asarray(np.broadcast_to(twr, (nm, rows, LANES)), F32)
    twi_b = jnp.asarray(np.broadcast_to(twi, (nm, rows, LANES)), F32)
    f5 = f.reshape(bsz, DFT_A, DFT_NA, DFT_MPAIR * SUB, FOUR_CH)
    a_step = lambda t: jnp.minimum(t, DFT_NA - 1)
    b_step = lambda t: jnp.maximum(t - DFT_NA, 0)
    tw_spec = pl.BlockSpec((DFT_MPAIR, rows, LANES), lambda b, t: (a_step(t), 0, 0))
    scr = pltpu.VMEM((DFT_A, DFT_NA, DFT_MPAIR * SUB, FOUR_CH), BF16)
    fy, *cast_out = pl.pallas_call(
        _with_casts(_dft_body, 6, 1, len(casts)),
        grid=(bsz, n_steps),
        in_specs=[pl.BlockSpec((None, DFT_A, None, DFT_MPAIR * SUB, FOUR_CH),
                               lambda b, t: (b, 0, a_step(t), 0, 0)),
                  _const_spec((2 * rows, rows)), tw_spec, tw_spec,
                  _const_spec((2 * DFT_B, 2 * DFT_B)), _const_spec((2 * FOUR_CH, FOUR_CH))] + c_in,
        out_specs=[pl.BlockSpec((None, DFT_VB, DFT_B, FOUR_CH), lambda b, t: (b, b_step(t), 0, 0))] + c_out,
        out_shape=[jax.ShapeDtypeStruct((bsz, DFT_A, DFT_B, FOUR_CH), BF16)] + c_shape,
        scratch_shapes=[scr, scr],
        compiler_params=_cparams(2),
        name="dft_2d",
    )(f5, _bf16_const(l1), twr_b, twi_b, _bf16_const(l2), mix_bf, *[a for a, _ in casts])
    return [jnp.transpose(fy, (0, 2, 1, 3)).reshape(bsz, SEQ, FOUR_CH)] + cast_out


def _dft_small_body(f_ref, lc_ref, mix_ref, fy_ref):
    n = f_ref.shape[0]
    pq = jnp.dot(lc_ref[...], f_ref[...].astype(BF16), preferred_element_type=F32)
    fy_ref[...] = _channel_dft(pq, n, mix_ref)


def _dft_2d_small(f, mix_bf):
    bsz, n, _ = f.shape
    k = np.arange(n)
    ang = 2.0 * np.pi * np.outer(k, k) / n
    lc = np.concatenate([np.cos(ang), np.sin(ang)], axis=0)
    spec = pl.BlockSpec((None, n, FOUR_CH), lambda b: (b, 0, 0))
    return pl.pallas_call(
        _dft_small_body,
        grid=(bsz,),
        in_specs=[spec, _const_spec((2 * n, n)), _const_spec((2 * FOUR_CH, FOUR_CH))],
        out_specs=spec,
        out_shape=jax.ShapeDtypeStruct((bsz, n, FOUR_CH), BF16),
        compiler_params=_cparams(1),
        name="dft_small",
    )(f, _bf16_const(lc), mix_bf)


def _channel_mix_const(seq_len):
    k = np.arange(FOUR_GROUP)
    ang = 2.0 * np.pi * np.outer(k, k) / FOUR_GROUP
    ng = FOUR_CH // FOUR_GROUP
    bdc = np.kron(np.eye(ng), np.cos(ang))
    bds = np.kron(np.eye(ng), np.sin(ang))
    return np.concatenate([bdc, -bds], axis=0) / np.sqrt(seq_len * FOUR_GROUP)


def _mixout_mlp_body(mod_row, x_ref, ay_ref, fy_ref, mod_ref, g2_ref, wout_ref, w1_ref, w2_ref, o_ref):
    b = pl.program_id(0)
    row = b if mod_row is None else mod_row
    gate1, shift2, scale2, gate2 = _mod_rows(mod_ref, row, (2, 3, 4, 5))
    cat = jnp.concatenate([ay_ref[...], fy_ref[...]], axis=1)
    mo = jnp.dot(cat, wout_ref[...], preferred_element_type=F32)
    x1 = x_ref[...] + gate1 * mo
    o_ref[...] = x1 + gate2 * _mlp(x1, g2_ref[...], shift2, scale2, w1_ref, w2_ref)


def _mixout_mlp(x, ay, fy, mod, g2, wout_bf, w1_bf, w2_bf, mod_row, tm, casts=()):
    bsz, seq_len, _ = x.shape
    nt = seq_len // tm
    row_spec = lambda w: pl.BlockSpec((None, tm, w), lambda b, t: (b, t, 0))
    c_in, c_out, c_shape = _cast_jobs(casts, bsz * nt, lambda b, t: b * nt + t)
    body = _with_casts(functools.partial(_mixout_mlp_body, mod_row), 8, 1, len(casts))
    return pl.pallas_call(
        body,
        grid=(bsz, nt),
        in_specs=[row_spec(D), row_spec(CONV_CH), row_spec(FOUR_CH),
                  _const_spec((8, N_MOD * D)), _const_spec((1, D)),
                  _resident_spec((D, D)), _resident_spec((D, D_FF)), _resident_spec((D_FF, D))] + c_in,
        out_specs=[row_spec(D)] + c_out,
        out_shape=[jax.ShapeDtypeStruct((bsz, seq_len, D), F32)] + c_shape,
        compiler_params=_cparams(2),
        name="mixer_out_mlp",
    )(x, ay, fy, mod, g2, wout_bf, w1_bf, w2_bf, *[a for a, _ in casts])


def _head_rms(t, bd_ref, gt):
    tt = (t * t).astype(BF16)
    w = bd_ref.shape[0]
    ms = jnp.concatenate(
        [jnp.dot(tt[:, j * w:(j + 1) * w], bd_ref[...], preferred_element_type=F32)
         for j in range(D // w)], axis=1)
    return t * lax.rsqrt(ms + EPS) * gt


def _qkv_common(mod_row, x_ref, mod_ref, g_ref, w_ref, bd_ref, qg_ref, kg_ref):
    b = pl.program_id(0)
    row = b if mod_row is None else mod_row
    shift, scale = _mod_rows(mod_ref, row, (0, 1))
    h = _rms_mod(x_ref[...], g_ref[...], shift, scale).astype(BF16)
    qkv = jnp.dot(h, w_ref[...], preferred_element_type=F32)
    q = _head_rms(qkv[:, 0:D], bd_ref, qg_ref[...]) * (HEAD_DIM ** -0.5 * LOG2E)
    k = _head_rms(qkv[:, D:2 * D], bd_ref, kg_ref[...])
    v = qkv[:, 2 * D:]
    return q, k, v


def _qkv_grid_body(x_ref, mod_ref, g_ref, w_ref, bd_ref, qg_ref, kg_ref, q_ref, k_ref, v_ref):
    q, k, v = _qkv_common(None, x_ref, mod_ref, g_ref, w_ref, bd_ref, qg_ref, kg_ref)
    for rho in range(TM // GRID_W):
        for g in range(NGRP):
            q0 = rho * GRID_W + QCOLS * g
            q_ref[g, rho * QCOLS:(rho + 1) * QCOLS, :] = q[q0:q0 + QCOLS].astype(BF16)
            k0 = rho * GRID_W + KC0[g]
            k_ref[g, rho * KCOLS:(rho + 1) * KCOLS, :] = k[k0:k0 + KCOLS].astype(BF16)
            v_ref[g, rho * KCOLS:(rho + 1) * KCOLS, :] = v[k0:k0 + KCOLS].astype(BF16)


def _qkv_ctx_body(mod_row, x_ref, mod_ref, g_ref, w_ref, bd_ref, qg_ref, kg_ref, k_ref, v_ref):
    _, k, v = _qkv_common(mod_row, x_ref, mod_ref, g_ref, w_ref, bd_ref, qg_ref, kg_ref)
    k_ref[...] = k.astype(BF16)
    v_ref[...] = v.astype(BF16)


def _qkv_consts(q_g, k_g):
    bd = np.kron(np.eye(MXU_TILE // HEAD_DIM), np.ones((HEAD_DIM, HEAD_DIM))) / HEAD_DIM
    qg = jnp.tile(q_g, N_HEADS).reshape(1, D)
    kg = jnp.tile(k_g, N_HEADS).reshape(1, D)
    return _bf16_const(bd), qg, kg


def _qkv_in_specs(tm):
    return [pl.BlockSpec((None, tm, D), lambda b, t: (b, t, 0)),
            _const_spec((8, N_MOD * D)), _const_spec((1, D)), _resident_spec((D, 3 * D)),
            _const_spec((MXU_TILE, MXU_TILE)), _const_spec((1, D)), _const_spec((1, D))]


def _qkv_grid(x, mod, g, wqkv_bf, q_g, k_g):
    bsz = x.shape[0]
    bd, qg, kg = _qkv_consts(q_g, k_g)
    nq = TM // GRID_W * QCOLS
    nk = TM // GRID_W * KCOLS
    return pl.pallas_call(
        _qkv_grid_body,
        grid=(bsz, SEQ // TM),
        in_specs=_qkv_in_specs(TM),
        out_specs=[pl.BlockSpec((None, NGRP, nq, D), lambda b, t: (b, 0, t, 0)),
                   pl.BlockSpec((None, NGRP, nk, D), lambda b, t: (b, 0, t, 0)),
                   pl.BlockSpec((None, NGRP, nk, D), lambda b, t: (b, 0, t, 0))],
        out_shape=[jax.ShapeDtypeStruct((bsz, NGRP, GRID_ROWS * QCOLS, D), BF16),
                   jax.ShapeDtypeStruct((bsz, NGRP, GRID_ROWS * KCOLS, D), BF16),
                   jax.ShapeDtypeStruct((bsz, NGRP, GRID_ROWS * KCOLS, D), BF16)],
        compiler_params=_cparams(2),
        name="qkv_grid",
    )(x, mod, g, wqkv_bf, bd, qg, kg)


def _qkv_ctx(ctx, mod, g, wqkv_bf, q_g, k_g, mod_row):
    bsz, n, _ = ctx.shape
    bd, qg, kg = _qkv_consts(q_g, k_g)
    spec = pl.BlockSpec((None, n, D), lambda b, t: (b, t, 0))
    return pl.pallas_call(
        functools.partial(_qkv_ctx_body, mod_row),
        grid=(bsz, 1),
        in_specs=_qkv_in_specs(n),
        out_specs=[spec, spec],
        out_shape=[jax.ShapeDtypeStruct((bsz, n, D), BF16)] * 2,
        compiler_params=_cparams(2),
        name="qkv_ctx",
    )(ctx, mod, g, wqkv_bf, bd, qg, kg)


def _bias_tables(rpb):
    n_ro = 2 * WIN_ROWS - 1
    n_co = 2 * WIN_COLS - 1
    g = np.arange(NGRP)[:, None, None]
    cq = np.arange(QCOLS)[None, :, None]
    kcw = (np.arange(LANES) % KCOLS)[None, None, :]
    c = QCOLS * g + cq
    kc = np.asarray(KC0)[:, None, None] + kcw
    cs = np.clip(c - WIN_COLS // 2, 0, GRID_W - WIN_COLS)
    col_ok = (kc >= cs) & (kc < cs + WIN_COLS)
    co = np.where(col_ok, kc - c + (WIN_COLS - 1), -1)
    onehot = (np.arange(n_co)[:, None, None, None] == co[None]).astype(np.float32)
    n_pad = n_ro + 2 + KROWS_PER_TILE
    rpb_p = jnp.pad(rpb * LOG2E, ((0, 0), (3, KROWS_PER_TILE - 1), (0, 0)))
    sel = jnp.einsum("hrc,cn->hrn", rpb_p, jnp.asarray(onehot.reshape(n_co, -1)),
                     precision=lax.Precision.HIGHEST)
    sel = sel.reshape(N_HEADS, n_pad, NGRP, QCOLS, LANES).transpose(0, 2, 1, 3, 4)
    rt = jnp.where(col_ok[None, :, None], sel, NEG)
    rt = rt.reshape(N_HEADS // 2, 2, NGRP, n_pad, QCOLS, LANES)
    delta = np.arange(N_RM)[:, None, None] - WIN_ROWS
    jj2 = (np.arange(LANES) // KCOLS)[None, None, :]
    rm = np.where((jj2 >= delta) & (jj2 < delta + WIN_ROWS), 0.0, NEG)
    rm = np.broadcast_to(rm, (N_RM, QCOLS, LANES)).astype(np.float32)
    return rt.astype(F32), jnp.asarray(rm)


def _tile_indices(rb, i, jq):
    wr = int(np.clip(RB * rb - WIN_ROWS // 2, 0, GRID_ROWS - WROWS))
    r = RB * rb + i
    delta = int(np.clip(r - WIN_ROWS // 2, 0, GRID_ROWS - WIN_ROWS)) - wr - KROWS_PER_TILE * jq
    if not -WIN_ROWS < delta < KROWS_PER_TILE:
        return None
    bti = wr + KROWS_PER_TILE * jq - r + (WIN_ROWS - 1) + 3
    assert 0 <= bti < N_BT and 0 <= delta + WIN_ROWS < N_RM
    return bti, delta + WIN_ROWS


def _attn_body(q_ref, k_ref, v_ref, kc_ref, vc_ref, rt_ref, rm_ref, o_ref, s_scr, bt_scr, bm_scr):
    kc = kc_ref[...]
    vc = vc_ref[...]
    nq = RB * QCOLS
    nk = WROWS * KCOLS
    lane = lax.broadcasted_iota(jnp.int32, (nq, LANES), 1)
    nt = nk // LANES
    nc = CTX // LANES
    dn = (((1,), (1,)), ((), ()))

    @pl.when(pl.program_id(2) == 0)
    def _build_tables():
        lgrp = lax.broadcasted_iota(jnp.int32, (QCOLS, LANES), 1) // KCOLS
        for e in range(2):
            for t in range(N_BT):
                blk = rt_ref[e, t + KROWS_PER_TILE - 1]
                for jj in range(KROWS_PER_TILE - 2, -1, -1):
                    blk = jnp.where(lgrp == jj, rt_ref[e, t + jj], blk)
                bt_scr[e, t] = blk
        for e in range(2):
            for i in range(RB):
                for jq in range(nt):
                    idx = _tile_indices(1, i, jq)
                    if idx is not None:
                        bm_scr[e, i * nt + jq] = bt_scr[e, idx[0]] + rm_ref[idx[1]]

    def window_row(rb):
        return jnp.clip(RB * rb - WIN_ROWS // 2, 0, GRID_ROWS - WROWS)

    def scores(rb, slot):
        rb = jnp.asarray(rb, jnp.int32)
        koff = pl.multiple_of(window_row(rb) * KCOLS, LANES)
        kw = k_ref[pl.ds(koff, nk), :]
        q = q_ref[pl.ds(pl.multiple_of(rb * nq, nq), nq), :]
        for e in range(2):
            in_head = (lane >= HEAD_DIM * e) & (lane < HEAD_DIM * (e + 1))
            qm = jnp.where(in_head, q, jnp.zeros_like(q))
            s_scr[slot, e, :, 0:nk] = lax.dot_general(qm, kw, dn, preferred_element_type=F32)
            s_scr[slot, e, :, nk:] = lax.dot_general(qm, kc, dn, preferred_element_type=F32)

    def finish(rb, slot, edge_rb):
        rb = jnp.asarray(rb, jnp.int32)
        wr = window_row(rb)
        vw = v_ref[pl.ds(pl.multiple_of(wr * KCOLS, LANES), nk), :]
        zero_tile = jnp.zeros((QCOLS, LANES), F32)
        outs = []
        for e in range(2):
            p_rows = []
            l_rows = []
            for i in range(RB):
                rows = slice(i * QCOLS, (i + 1) * QCOLS)
                blks = {}
                for jq in range(nt):
                    idx = _tile_indices(1 if edge_rb is None else edge_rb, i, jq)
                    if idx is None:
                        continue
                    add = bm_scr[e, i * nt + jq] if edge_rb is None else bt_scr[e, idx[0]] + rm_ref[idx[1]]
                    blks[jq] = s_scr[slot, e, rows, jq * LANES:(jq + 1) * LANES] + add
                for jc in range(nc):
                    blks[nt + jc] = s_scr[slot, e, rows, nk + jc * LANES:nk + (jc + 1) * LANES]
                vals = list(blks.values())
                m = jnp.max(functools.reduce(jnp.maximum, vals), axis=-1, keepdims=True)
                ps = {j: jnp.exp2(sb - m) for j, sb in blks.items()}
                l_rows.append(jnp.sum(functools.reduce(jnp.add, list(ps.values())), axis=-1, keepdims=True))
                p_rows.append(jnp.concatenate([ps.get(j, zero_tile) for j in range(nt + nc)], axis=1))
            p = jnp.concatenate(p_rows, axis=0).astype(BF16)
            lsum = jnp.concatenate(l_rows, axis=0)
            o = (jnp.dot(p[:, :nk], vw, preferred_element_type=F32)
                 + jnp.dot(p[:, nk:], vc, preferred_element_type=F32))
            outs.append(o / lsum)
        o = jnp.where(lane < HEAD_DIM, outs[0], outs[1])
        o_ref[pl.ds(pl.multiple_of(rb * nq, nq), nq), :] = o.astype(BF16)

    scores(0, 0)
    scores(1, 1)
    finish(0, 0, 0)
    scores(2, 0)
    finish(1, 1, None)

    def pair(tt, carry):
        t0 = 2 * tt
        scores(t0 + 1, 1)
        finish(t0, 0, None)
        scores(t0 + 2, 0)
        finish(t0 + 1, 1, None)
        return carry

    lax.fori_loop(1, NRB // 2 - 1, pair, 0, unroll=True)
    scores(NRB - 1, 1)
    finish(NRB - 2, 0, None)
    finish(NRB - 1, 1, NRB - 1)


def _attention(qcb, kcb, vcb, kc, vc, bt, rm, casts=()):
    bsz = qcb.shape[0]
    nhp = N_HEADS // 2
    n_t = bt.shape[3]
    c_in, c_out, c_shape = _cast_jobs(casts, nhp * NGRP * bsz, lambda h, g, b: (h * NGRP + g) * bsz + b)
    return pl.pallas_call(
        _with_casts(_attn_body, 7, 1, len(casts)),
        grid=(nhp, NGRP, bsz),
        in_specs=[pl.BlockSpec((None, None, GRID_ROWS * QCOLS, LANES), lambda h, g, b: (b, g, 0, h)),
                  pl.BlockSpec((None, None, GRID_ROWS * KCOLS, LANES), lambda h, g, b: (b, g, 0, h)),
                  pl.BlockSpec((None, None, GRID_ROWS * KCOLS, LANES), lambda h, g, b: (b, g, 0, h)),
                  pl.BlockSpec((None, CTX, LANES), lambda h, g, b: (b, 0, h)),
                  pl.BlockSpec((None, CTX, LANES), lambda h, g, b: (b, 0, h)),
                  pl.BlockSpec((None, 2, None, n_t, QCOLS, LANES), lambda h, g, b: (h, 0, g, 0, 0, 0)),
                  _const_spec((N_RM, QCOLS, LANES))] + c_in,
        out_specs=[pl.BlockSpec((None, None, GRID_ROWS * QCOLS, LANES), lambda h, g, b: (b, g, 0, h))] + c_out,
        out_shape=[jax.ShapeDtypeStruct((bsz, NGRP, GRID_ROWS * QCOLS, D), BF16)] + c_shape,
        scratch_shapes=[pltpu.VMEM((2, 2, RB * QCOLS, WROWS * KCOLS + CTX), F32),
                        pltpu.VMEM((2, N_BT, QCOLS, LANES), F32),
                        pltpu.VMEM((2, RB * WROWS * KCOLS // LANES, QCOLS, LANES), F32)],
        compiler_params=_cparams(3),
        name="nbr_attention",
    )(qcb, kcb, vcb, kc, vc, bt, rm, *[a for a, _ in casts])


def _attnout_mlp_body(x_ref, o_ref_in, mod_ref, g2_ref, wo_ref, w1_ref, w2_ref, out_ref):
    b = pl.program_id(0)
    gate1, shift2, scale2, gate2 = _mod_rows(mod_ref, b, (2, 3, 4, 5))
    chunks = [o_ref_in[g, rho * QCOLS:(rho + 1) * QCOLS, :]
              for rho in range(TM_MLP // GRID_W) for g in range(NGRP)]
    o_nat = jnp.concatenate(chunks, axis=0)
    mo = jnp.dot(o_nat, wo_ref[...], preferred_element_type=F32)
    x1 = x_ref[...] + gate1 * mo
    out_ref[...] = x1 + gate2 * _mlp(x1, g2_ref[...], shift2, scale2, w1_ref, w2_ref)


def _attnout_mlp(x, ocb, mod, g2, wo_bf, w1_bf, w2_bf):
    bsz = x.shape[0]
    nq = TM_MLP // GRID_W * QCOLS
    return pl.pallas_call(
        _attnout_mlp_body,
        grid=(bsz, SEQ // TM_MLP),
        in_specs=[pl.BlockSpec((None, TM_MLP, D), lambda b, t: (b, t, 0)),
                  pl.BlockSpec((None, NGRP, nq, D), lambda b, t: (b, 0, t, 0)),
                  _const_spec((8, N_MOD * D)), _const_spec((1, D)),
                  _resident_spec((D, D)), _resident_spec((D, D_FF)), _resident_spec((D_FF, D))],
        out_specs=pl.BlockSpec((None, TM_MLP, D), lambda b, t: (b, t, 0)),
        out_shape=jax.ShapeDtypeStruct((bsz, SEQ, D), F32),
        compiler_params=_cparams(2),
        name="attn_out_mlp",
    )(x, ocb, mod, g2, wo_bf, w1_bf, w2_bf)


def kernel(x, c, ctx, c_ctx, ada_w, ada_b, norm_mix_g, norm_mlp_g, mlp_w1, mlp_w2, ab_w_in, ab_conv_w,
           ab_w_out, na_w_qkv, na_q_g, na_k_g, na_rpb, na_w_out):
    bsz = x.shape[0]
    ctx_row = bsz
    cond8 = jnp.zeros((8, D), F32).at[:bsz].set(c).at[ctx_row].set(c_ctx)
    mods, w_in = _modulation(cond8, ada_w, ada_b, casts=((ab_w_in, 0),))

    g_mix = norm_mix_g.reshape(DEPTH, 1, D)
    g_mlp = norm_mlp_g.reshape(DEPTH, 1, D)
    conv_w8 = jnp.zeros((8, CONV_CH), F32).at[:3].set(ab_conv_w[0])

    ay, f, w1_0, w2_0, w_out0, wqkv, w1_1, w2_1, wo = _inproj(
        x, mods[0], g_mix[0], w_in, conv_w8, None, TM_IN,
        casts=((mlp_w1, 0), (mlp_w2, 0), (ab_w_out, 0), (na_w_qkv, 0), (mlp_w1, 1), (mlp_w2, 1), (na_w_out, 0)))
    fy, = _dft_2d(f, _bf16_const(_channel_mix_const(SEQ)))
    x, = _mixout_mlp(x, ay, fy, mods[0], g_mlp[0], w_out0, w1_0, w2_0, None, TM_MLP)

    ay_c, f_c = _inproj(ctx, mods[0], g_mix[0], w_in, conv_w8, ctx_row, CTX)
    fy_c = _dft_2d_small(f_c, _bf16_const(_channel_mix_const(CTX)))
    flat = lambda a: a.reshape(1, bsz * CTX, a.shape[-1])
    ctx, = _mixout_mlp(flat(ctx), flat(ay_c), flat(fy_c), mods[0], g_mlp[0], w_out0, w1_0, w2_0,
                       ctx_row, bsz * CTX)

    qcb, kcb, vcb = _qkv_grid(x, mods[1], g_mix[1], wqkv, na_q_g[0], na_k_g[0])
    kc, vc = _qkv_ctx(ctx, mods[1], g_mix[1], wqkv, na_q_g[0], na_k_g[0], ctx_row)
    kc, vc = kc.reshape(bsz, CTX, D), vc.reshape(bsz, CTX, D)
    bt, rm = _bias_tables(na_rpb[0])
    ocb, = _attention(qcb, kcb, vcb, kc, vc, bt, rm)
    x = _attnout_mlp(x, ocb, mods[1], g_mlp[1], wo, w1_1, w2_1)
    return x
```

```python
import functools

import numpy as np
import jax
import jax.numpy as jnp
from jax import lax
from jax.experimental import pallas as pl
from jax.experimental.pallas import tpu as pltpu

F32 = jnp.float32
BF16 = jnp.bfloat16

D = 1024
DEPTH = 2
SEQ = 8192
CTX = 256
GRID_W = 64
GRID_ROWS = SEQ // GRID_W
HEAD_DIM = 64
N_HEADS = 16
CONV_CH = 512
FOUR_CH = 512
FOUR_GROUP = 64
WIN_ROWS = 8
WIN_COLS = 16
D_FF = 4 * D
N_MOD = 6
EPS = 1e-6

LANES = 128
MXU_TILE = 256

TM = 1024
TM_IN = 1024
TM_MLP = 1024
HALO = 8
FF_CHUNK = 1024
VMEM_LIMIT = 56 * 1024 * 1024

DFT_A = 32
DFT_B = SEQ // DFT_A
SUB = 8

QCOLS = 16
NGRP = GRID_W // QCOLS
KCOLS = 32
KC0 = tuple(int(np.clip(QCOLS * g - 8, 0, GRID_W - KCOLS)) for g in range(NGRP))
RB = 8
NRB = GRID_ROWS // RB
WROWS = 16
KROWS_PER_TILE = LANES // KCOLS
N_RM = WIN_ROWS + KROWS_PER_TILE + 1
NEG = -1e30
LOG2E = float(np.log2(np.e))
N_BT = 2 * WIN_ROWS - 1 + KROWS_PER_TILE - 1


def _cparams(n_axes):
    return pltpu.CompilerParams(dimension_semantics=("arbitrary",) * n_axes,
                                vmem_limit_bytes=VMEM_LIMIT)


def _const_spec(shape):
    nd = len(shape)
    return pl.BlockSpec(shape, lambda *_: (0,) * nd)


def _resident_spec(shape):
    nd = len(shape)
    return pl.BlockSpec(shape, lambda *_: (0,) * nd, pipeline_mode=pl.Buffered(1))


def _cast_jobs(jobs, n_steps, step_of):
    in_specs, out_specs, out_shapes = [], [], []
    for arr, layer in jobs:
        _, rows, cols = arr.shape
        chunk = rows // n_steps
        assert chunk * n_steps == rows and chunk % 16 == 0
        in_specs.append(pl.BlockSpec((None, chunk, cols), lambda *ids, layer=layer: (layer, step_of(*ids), 0)))
        out_specs.append(pl.BlockSpec((chunk, cols), lambda *ids: (step_of(*ids), 0)))
        out_shapes.append(jax.ShapeDtypeStruct((rows, cols), BF16))
    return in_specs, out_specs, out_shapes


def _with_casts(body, n_in, n_out, n_jobs):
    def wrapped(*refs):
        ins, refs = refs[:n_in], refs[n_in:]
        cast_ins, refs = refs[:n_jobs], refs[n_jobs:]
        outs, refs = refs[:n_out], refs[n_out:]
        cast_outs, scratch = refs[:n_jobs], refs[n_jobs:]
        for src, dst in zip(cast_ins, cast_outs):
            dst[...] = src[...].astype(BF16)
        body(*ins, *outs, *scratch)
    return wrapped


def _bf16_const(a):
    return jnp.asarray(a, F32).astype(BF16)


def _rms_mod(x, g, shift, scale):
    ms = jnp.mean(x * x, axis=-1, keepdims=True)
    y = x * lax.rsqrt(ms + EPS)
    return (y * g) * (1.0 + scale) + shift


def _mod_rows(mod_ref, row, ks):
    return [mod_ref[pl.ds(row, 1), k * D:(k + 1) * D] for k in ks]


def _mlp(x1, g, shift, scale, w1_ref, w2_ref):
    h = _rms_mod(x1, g, shift, scale).astype(BF16)
    acc = jnp.zeros(x1.shape, F32)
    for c in range(D_FF // FF_CHUNK):
        a = jnp.dot(h, w1_ref[:, c * FF_CHUNK:(c + 1) * FF_CHUNK], preferred_element_type=F32)
        a = jnp.maximum(a, 0.0)
        a = (a * a).astype(BF16)
        acc = acc + jnp.dot(a, w2_ref[c * FF_CHUNK:(c + 1) * FF_CHUNK, :], preferred_element_type=F32)
    return acc


def _mod_body(cond_ref, w_ref, b_ref, o_ref):
    c = cond_ref[...]
    s = c * jax.nn.sigmoid(c)
    o_ref[...] = jnp.dot(s.astype(BF16), w_ref[...].astype(BF16), preferred_element_type=F32) + b_ref[...]


MOD_TILE = 1536


def _modulation(cond8, ada_w, ada_b, casts=()):
    nt = N_MOD * D // MOD_TILE
    c_in, c_out, c_shape = _cast_jobs(casts, DEPTH * nt, lambda l, n: l * nt + n)
    return pl.pallas_call(
        _with_casts(_mod_body, 3, 1, len(casts)),
        grid=(DEPTH, nt),
        in_specs=[_const_spec((8, D)),
                  pl.BlockSpec((None, D, MOD_TILE), lambda l, n: (l, 0, n)),
                  pl.BlockSpec((None, 1, MOD_TILE), lambda l, n: (l, 0, n))] + c_in,
        out_specs=[pl.BlockSpec((None, 8, MOD_TILE), lambda l, n: (l, 0, n))] + c_out,
        out_shape=[jax.ShapeDtypeStruct((DEPTH, 8, N_MOD * D), F32)] + c_shape,
        compiler_params=_cparams(2),
        name="adaln_mod",
    )(cond8, ada_w, ada_b.reshape(DEPTH, 1, N_MOD * D), *[a for a, _ in casts])


def _inproj_body(mod_row, seq_len, tm, xp_ref, x_ref, xn_ref, mod_ref, g_ref, w_ref, cw_ref,
                 ay_ref, f_ref):
    ay, f = _inproj_tile(mod_row, seq_len, tm, xp_ref, x_ref, xn_ref, mod_ref, g_ref, w_ref, cw_ref)
    ay_ref[...] = ay
    f_ref[...] = f


def _ctx_in_body(mod_row, seq_len, xp_ref, x_ref, xn_ref, mod_ref, g_ref, w_ref, cw_ref, lc_ref, mix_ref,
                 ay_ref, fy_ref):
    ay, f = _inproj_tile(mod_row, seq_len, seq_len, xp_ref, x_ref, xn_ref, mod_ref, g_ref, w_ref, cw_ref)
    ay_ref[...] = ay
    pq = jnp.dot(lc_ref[...], f.astype(BF16), preferred_element_type=F32)
    fy_ref[...] = _channel_dft(pq, seq_len, mix_ref)


def _inproj_tile(mod_row, seq_len, tm, xp_ref, x_ref, xn_ref, mod_ref, g_ref, w_ref, cw_ref):
    b = pl.program_id(0)
    t = pl.program_id(1)
    row = b if mod_row is None else mod_row
    xx = jnp.concatenate([xp_ref[...], x_ref[...], xn_ref[...]], axis=0)
    shift, scale = _mod_rows(mod_ref, row, (0, 1))
    h = _rms_mod(xx, g_ref[...], shift, scale).astype(BF16)
    u = jnp.dot(h, w_ref[...], preferred_element_type=F32)
    z = u[:, CONV_CH:2 * CONV_CH] * u[:, 0:CONV_CH]
    n = t * tm - HALO + lax.broadcasted_iota(jnp.int32, (tm + 2 * HALO, 1), 0)
    z = jnp.where((n >= 0) & (n < seq_len), z, 0.0)
    cw = cw_ref[...]
    zc = (z[HALO - 1:HALO - 1 + tm] * cw[0:1] + z[HALO:HALO + tm] * cw[1:2]
          + z[HALO + 1:HALO + 1 + tm] * cw[2:3])
    return (u[HALO:HALO + tm, 2 * CONV_CH:3 * CONV_CH] * zc).astype(BF16), u[HALO:HALO + tm, 3 * CONV_CH:]


def _inproj(x, mod, g, w_in_bf, conv_w8, mod_row, tm, casts=()):
    bsz, seq_len, _ = x.shape
    nt = seq_len // tm
    nb8 = seq_len // HALO
    r8 = tm // HALO
    c_in, c_out, c_shape = _cast_jobs(casts, bsz * nt, lambda b, t: b * nt + t)
    body = _with_casts(functools.partial(_inproj_body, mod_row, seq_len, tm), 7, 2, len(casts))
    return pl.pallas_call(
        body,
        grid=(bsz, nt),
        in_specs=[pl.BlockSpec((None, HALO, D), lambda b, t: (b, jnp.maximum(t * r8 - 1, 0), 0)),
                  pl.BlockSpec((None, tm, D), lambda b, t: (b, t, 0)),
                  pl.BlockSpec((None, HALO, D), lambda b, t: (b, jnp.minimum((t + 1) * r8, nb8 - 1), 0)),
                  _const_spec((8, N_MOD * D)),
                  _const_spec((1, D)),
                  _const_spec((D, 3 * CONV_CH + FOUR_CH)),
                  _const_spec((8, CONV_CH))] + c_in,
        out_specs=[pl.BlockSpec((None, tm, CONV_CH), lambda b, t: (b, t, 0)),
                   pl.BlockSpec((None, tm, FOUR_CH), lambda b, t: (b, t, 0))] + c_out,
        out_shape=[jax.ShapeDtypeStruct((bsz, seq_len, CONV_CH), BF16),
                   jax.ShapeDtypeStruct((bsz, seq_len, FOUR_CH), F32)] + c_shape,
        compiler_params=_cparams(2),
        name="mixer_in",
    )(x, x, x, mod, g, w_in_bf, conv_w8, *[a for a, _ in casts])


def _dft_consts():
    a = np.arange(DFT_A)
    s = np.arange(SUB)
    ang = 2.0 * np.pi * np.outer(a, a) / DFT_A
    eye = np.eye(SUB)
    re = np.einsum("va,ts->vtas", np.cos(ang), eye).reshape(DFT_A * SUB, DFT_A * SUB)
    im = np.einsum("va,ts->vtas", -np.sin(ang), eye).reshape(DFT_A * SUB, DFT_A * SUB)
    l1 = np.concatenate([re, im], axis=0)
    m = np.arange(DFT_B // SUB)
    bb = (SUB * m[:, None, None] + s[None, None, :])
    tang = 2.0 * np.pi * a[None, :, None] * bb / SEQ
    twr = np.cos(tang).reshape(DFT_B // SUB, DFT_A * SUB, 1)
    twi = (-np.sin(tang)).reshape(DFT_B // SUB, DFT_A * SUB, 1)
    u = np.arange(DFT_B)
    ang2 = 2.0 * np.pi * np.outer(u, u) / DFT_B
    c2, s2 = np.cos(ang2), np.sin(ang2)
    l2 = np.block([[c2, s2], [s2, -c2]])
    return l1, twr, twi, l2


DFT_MPAIR = 8
DFT_VB = 8
DFT_NA = DFT_B // SUB // DFT_MPAIR
DFT_NB = DFT_A // DFT_VB


def _channel_dft(pq, n, mix_ref):
    w = MXU_TILE
    outs = []
    for cb in range(FOUR_CH // w):
        cols = slice(cb * w, (cb + 1) * w)
        lhs = jnp.concatenate([pq[:n, cols], pq[n:, cols]], axis=1).astype(BF16)
        rhs = jnp.concatenate([mix_ref[cols, cols],
                               mix_ref[FOUR_CH + cb * w:FOUR_CH + (cb + 1) * w, cols]], axis=0)
        outs.append(jnp.dot(lhs, rhs, preferred_element_type=F32))
    return jnp.concatenate(outs, axis=1).astype(BF16)


def _dft_body(x_ref, l1_ref, twr_ref, twi_ref, l2_ref, mix_ref, fy_ref, br_scr, bi_scr):
    t = pl.program_id(1)
    rows = DFT_A * SUB
    reps = FOUR_CH // LANES

    @pl.when(t < DFT_NA)
    def _stage_a():
        x = x_ref[...]
        brs, bis = [], []
        for j in range(DFT_MPAIR):
            xj = x[:, j * SUB:(j + 1) * SUB, :].reshape(rows, FOUR_CH).astype(BF16)
            a = jnp.dot(l1_ref[...], xj, preferred_element_type=F32)
            ar, ai = a[:rows], a[rows:]
            twr = jnp.concatenate([twr_ref[j]] * reps, axis=1)
            twi = jnp.concatenate([twi_ref[j]] * reps, axis=1)
            brs.append((ar * twr - ai * twi).reshape(DFT_A, SUB, FOUR_CH))
            bis.append((ar * twi + ai * twr).reshape(DFT_A, SUB, FOUR_CH))
        shape = (DFT_A, 1, DFT_MPAIR * SUB, FOUR_CH)
        br_scr[:, pl.ds(t, 1)] = jnp.concatenate(brs, axis=1).astype(BF16).reshape(shape)
        bi_scr[:, pl.ds(t, 1)] = jnp.concatenate(bis, axis=1).astype(BF16).reshape(shape)

    @pl.when(t >= DFT_NA)
    def _stage_b():
        v0 = (t - DFT_NA) * DFT_VB
        for j in range(DFT_VB):
            br = br_scr[v0 + j].reshape(DFT_B, FOUR_CH)
            bi = bi_scr[v0 + j].reshape(DFT_B, FOUR_CH)
            pq = jnp.dot(l2_ref[...], jnp.concatenate([br, bi], axis=0),
                         preferred_element_type=F32)
            fy_ref[j] = _channel_dft(pq, DFT_B, mix_ref)


def _dft_2d(f, mix_bf, casts=()):
    bsz = f.shape[0]
    n_steps = DFT_NA + DFT_NB
    c_in, c_out, c_shape = _cast_jobs(casts, bsz * n_steps, lambda b, t: b * n_steps + t)
    l1, twr, twi, l2 = _dft_consts()
    nm = DFT_B // SUB
    rows = DFT_A * SUB
    twr_b = jnp.asarray(np.broadcast_to(twr, (nm, rows, LANES)), F32)
    twi_b = jnp.asarray(np.broadcast_to(twi, (nm, rows, LANES)), F32)
    f5 = f.reshape(bsz, DFT_A, DFT_NA, DFT_MPAIR * SUB, FOUR_CH)
    a_step = lambda t: jnp.minimum(t, DFT_NA - 1)
    b_step = lambda t: jnp.maximum(t - DFT_NA, 0)
    tw_spec = pl.BlockSpec((DFT_MPAIR, rows, LANES), lambda b, t: (a_step(t), 0, 0))
    scr = pltpu.VMEM((DFT_A, DFT_NA, DFT_MPAIR * SUB, FOUR_CH), BF16)
    fy, *cast_out = pl.pallas_call(
        _with_casts(_dft_body, 6, 1, len(casts)),
        grid=(bsz, n_steps),
        in_specs=[pl.BlockSpec((None, DFT_A, None, DFT_MPAIR * SUB, FOUR_CH),
                               lambda b, t: (b, 0, a_step(t), 0, 0)),
                  _const_spec((2 * rows, rows)), tw_spec, tw_spec,
                  _const_spec((2 * DFT_B, 2 * DFT_B)), _const_spec((2 * FOUR_CH, FOUR_CH))] + c_in,
        out_specs=[pl.BlockSpec((None, DFT_VB, DFT_B, FOUR_CH), lambda b, t: (b, b_step(t), 0, 0))] + c_out,
        out_shape=[jax.ShapeDtypeStruct((bsz, DFT_A, DFT_B, FOUR_CH), BF16)] + c_shape,
        scratch_shapes=[scr, scr],
        compiler_params=_cparams(2),
        name="dft_2d",
    )(f5, _bf16_const(l1), twr_b, twi_b, _bf16_const(l2), mix_bf, *[a for a, _ in casts])
    return [jnp.transpose(fy, (0, 2, 1, 3)).reshape(bsz, SEQ, FOUR_CH)] + cast_out


def _ctx_in(ctx, mod, g, w_in_bf, conv_w8, mix_bf, mod_row):
    bsz, n, _ = ctx.shape
    k = np.arange(n)
    ang = 2.0 * np.pi * np.outer(k, k) / n
    lc = np.concatenate([np.cos(ang), np.sin(ang)], axis=0)
    last8 = n // HALO - 1
    out_spec = pl.BlockSpec((None, n, CONV_CH), lambda b, t: (b, 0, 0))
    return pl.pallas_call(
        functools.partial(_ctx_in_body, mod_row, n),
        grid=(bsz, 1),
        in_specs=[pl.BlockSpec((None, HALO, D), lambda b, t: (b, 0, 0)),
                  pl.BlockSpec((None, n, D), lambda b, t: (b, 0, 0)),
                  pl.BlockSpec((None, HALO, D), lambda b, t: (b, last8, 0)),
                  _const_spec((8, N_MOD * D)), _const_spec((1, D)),
                  _const_spec((D, 3 * CONV_CH + FOUR_CH)), _const_spec((8, CONV_CH)),
                  _const_spec((2 * n, n)), _const_spec((2 * FOUR_CH, FOUR_CH))],
        out_specs=[out_spec, out_spec],
        out_shape=[jax.ShapeDtypeStruct((bsz, n, CONV_CH), BF16)] * 2,
        compiler_params=_cparams(2),
        name="ctx_in_dft",
    )(ctx, ctx, ctx, mod, g, w_in_bf, conv_w8, _bf16_const(lc), mix_bf)


def _channel_mix_const(seq_len):
    k = np.arange(FOUR_GROUP)
    ang = 2.0 * np.pi * np.outer(k, k) / FOUR_GROUP
    ng = FOUR_CH // FOUR_GROUP
    bdc = np.kron(np.eye(ng), np.cos(ang))
    bds = np.kron(np.eye(ng), np.sin(ang))
    return np.concatenate([bdc, -bds], axis=0) / np.sqrt(seq_len * FOUR_GROUP)


def _mixout_mlp_body(mod_row, x_ref, ay_ref, fy_ref, mod_ref, g2_ref, wout_ref, w1_ref, w2_ref, o_ref):
    b = pl.program_id(0)
    row = b if mod_row is None else mod_row
    gate1, shift2, scale2, gate2 = _mod_rows(mod_ref, row, (2, 3, 4, 5))
    cat = jnp.concatenate([ay_ref[...], fy_ref[...]], axis=1)
    mo = jnp.dot(cat, wout_ref[...], preferred_element_type=F32)
    x1 = x_ref[...] + gate1 * mo
    o_ref[...] = x1 + gate2 * _mlp(x1, g2_ref[...], shift2, scale2, w1_ref, w2_ref)


def _mixout_mlp(x, ay, fy, mod, g2, wout_bf, w1_bf, w2_bf, mod_row, tm, casts=()):
    bsz, seq_len, _ = x.shape
    nt = seq_len // tm
    row_spec = lambda w: pl.BlockSpec((None, tm, w), lambda b, t: (b, t, 0))
    c_in, c_out, c_shape = _cast_jobs(casts, bsz * nt, lambda b, t: b * nt + t)
    body = _with_casts(functools.partial(_mixout_mlp_body, mod_row), 8, 1, len(casts))
    return pl.pallas_call(
        body,
        grid=(bsz, nt),
        in_specs=[row_spec(D), row_spec(CONV_CH), row_spec(FOUR_CH),
                  _const_spec((8, N_MOD * D)), _const_spec((1, D)),
                  _resident_spec((D, D)), _resident_spec((D, D_FF)), _resident_spec((D_FF, D))] + c_in,
        out_specs=[row_spec(D)] + c_out,
        out_shape=[jax.ShapeDtypeStruct((bsz, seq_len, D), F32)] + c_shape,
        compiler_params=_cparams(2),
        name="mixer_out_mlp",
    )(x, ay, fy, mod, g2, wout_bf, w1_bf, w2_bf, *[a for a, _ in casts])


def _head_rms(t, bd_ref, gt):
    tt = (t * t).astype(BF16)
    w = bd_ref.shape[0]
    ms = jnp.concatenate(
        [jnp.dot(tt[:, j * w:(j + 1) * w], bd_ref[...], preferred_element_type=F32)
         for j in range(D // w)], axis=1)
    return t * lax.rsqrt(ms + EPS) * gt


def _qkv_common(mod_row, x_ref, mod_ref, g_ref, w_ref, bd_ref, qg_ref, kg_ref):
    b = pl.program_id(0)
    row = b if mod_row is None else mod_row
    shift, scale = _mod_rows(mod_ref, row, (0, 1))
    h = _rms_mod(x_ref[...], g_ref[...], shift, scale).astype(BF16)
    qkv = jnp.dot(h, w_ref[...], preferred_element_type=F32)
    q = _head_rms(qkv[:, 0:D], bd_ref, qg_ref[...]) * (HEAD_DIM ** -0.5 * LOG2E)
    k = _head_rms(qkv[:, D:2 * D], bd_ref, kg_ref[...])
    v = qkv[:, 2 * D:]
    return q, k, v


def _qkv_grid_body(x_ref, mod_ref, g_ref, w_ref, bd_ref, qg_ref, kg_ref, q_ref, k_ref, v_ref):
    q, k, v = _qkv_common(None, x_ref, mod_ref, g_ref, w_ref, bd_ref, qg_ref, kg_ref)
    for rho in range(TM // GRID_W):
        for g in range(NGRP):
            q0 = rho * GRID_W + QCOLS * g
            q_ref[g, rho * QCOLS:(rho + 1) * QCOLS, :] = q[q0:q0 + QCOLS].astype(BF16)
            k0 = rho * GRID_W + KC0[g]
            k_ref[g, rho * KCOLS:(rho + 1) * KCOLS, :] = k[k0:k0 + KCOLS].astype(BF16)
            v_ref[g, rho * KCOLS:(rho + 1) * KCOLS, :] = v[k0:k0 + KCOLS].astype(BF16)


def _ctx_out_kv_body(mod_row, x_ref, ay_ref, fy_ref, mod0_ref, g2_ref, wout_ref, w1_ref, w2_ref,
                     mod1_ref, g1_ref, wqkv_ref, bd_ref, kg_ref, k_ref, v_ref):
    gate1, shift2, scale2, gate2 = _mod_rows(mod0_ref, mod_row, (2, 3, 4, 5))
    cat = jnp.concatenate([ay_ref[...], fy_ref[...]], axis=1)
    mo = jnp.dot(cat, wout_ref[...], preferred_element_type=F32)
    x1 = x_ref[...] + gate1 * mo
    x2 = x1 + gate2 * _mlp(x1, g2_ref[...], shift2, scale2, w1_ref, w2_ref)
    shift, scale = _mod_rows(mod1_ref, mod_row, (0, 1))
    h = _rms_mod(x2, g1_ref[...], shift, scale).astype(BF16)
    kv = jnp.dot(h, wqkv_ref[:, D:], preferred_element_type=F32)
    k_ref[...] = _head_rms(kv[:, :D], bd_ref, kg_ref[...]).astype(BF16)
    v_ref[...] = kv[:, D:].astype(BF16)


def _qkv_consts(q_g, k_g):
    bd = np.kron(np.eye(MXU_TILE // HEAD_DIM), np.ones((HEAD_DIM, HEAD_DIM))) / HEAD_DIM
    qg = jnp.tile(q_g, N_HEADS).reshape(1, D)
    kg = jnp.tile(k_g, N_HEADS).reshape(1, D)
    return _bf16_const(bd), qg, kg


def _qkv_in_specs(tm):
    return [pl.BlockSpec((None, tm, D), lambda b, t: (b, t, 0)),
            _const_spec((8, N_MOD * D)), _const_spec((1, D)), _resident_spec((D, 3 * D)),
            _const_spec((MXU_TILE, MXU_TILE)), _const_spec((1, D)), _const_spec((1, D))]


def _qkv_grid(x, mod, g, wqkv_bf, q_g, k_g):
    bsz = x.shape[0]
    bd, qg, kg = _qkv_consts(q_g, k_g)
    nq = TM // GRID_W * QCOLS
    nk = TM // GRID_W * KCOLS
    return pl.pallas_call(
        _qkv_grid_body,
        grid=(bsz, SEQ // TM),
        in_specs=_qkv_in_specs(TM),
        out_specs=[pl.BlockSpec((None, NGRP, nq, D), lambda b, t: (b, 0, t, 0)),
                   pl.BlockSpec((None, NGRP, nk, D), lambda b, t: (b, 0, t, 0)),
                   pl.BlockSpec((None, NGRP, nk, D), lambda b, t: (b, 0, t, 0))],
        out_shape=[jax.ShapeDtypeStruct((bsz, NGRP, GRID_ROWS * QCOLS, D), BF16),
                   jax.ShapeDtypeStruct((bsz, NGRP, GRID_ROWS * KCOLS, D), BF16),
                   jax.ShapeDtypeStruct((bsz, NGRP, GRID_ROWS * KCOLS, D), BF16)],
        compiler_params=_cparams(2),
        name="qkv_grid",
    )(x, mod, g, wqkv_bf, bd, qg, kg)


def _ctx_out_kv(ctx, ay, fy, mod0, g2, wout_bf, w1_bf, w2_bf, mod1, g1, wqkv_bf, q_g, k_g, mod_row):
    _, n, _ = ctx.shape
    bd, _, kg = _qkv_consts(q_g, k_g)
    row_spec = lambda w: pl.BlockSpec((None, n, w), lambda s: (0, 0, 0))
    return pl.pallas_call(
        functools.partial(_ctx_out_kv_body, mod_row),
        grid=(1,),
        in_specs=[row_spec(D), row_spec(CONV_CH), row_spec(FOUR_CH),
                  _const_spec((8, N_MOD * D)), _const_spec((1, D)),
                  _resident_spec((D, D)), _resident_spec((D, D_FF)), _resident_spec((D_FF, D)),
                  _const_spec((8, N_MOD * D)), _const_spec((1, D)), _resident_spec((D, 3 * D)),
                  _const_spec((MXU_TILE, MXU_TILE)), _const_spec((1, D))],
        out_specs=[row_spec(D), row_spec(D)],
        out_shape=[jax.ShapeDtypeStruct((1, n, D), BF16)] * 2,
        compiler_params=_cparams(1),
        name="ctx_out_kv",
    )(ctx, ay, fy, mod0, g2, wout_bf, w1_bf, w2_bf, mod1, g1, wqkv_bf, bd, kg)


def _bias_tables(rpb):
    n_ro = 2 * WIN_ROWS - 1
    n_co = 2 * WIN_COLS - 1
    g = np.arange(NGRP)[:, None, None]
    cq = np.arange(QCOLS)[None, :, None]
    kcw = (np.arange(LANES) % KCOLS)[None, None, :]
    c = QCOLS * g + cq
    kc = np.asarray(KC0)[:, None, None] + kcw
    cs = np.clip(c - WIN_COLS // 2, 0, GRID_W - WIN_COLS)
    col_ok = (kc >= cs) & (kc < cs + WIN_COLS)
    co = np.where(col_ok, kc - c + (WIN_COLS - 1), -1)
    onehot = (np.arange(n_co)[:, None, None, None] == co[None]).astype(np.float32)
    n_pad = n_ro + 2 + KROWS_PER_TILE
    rpb_p = jnp.pad(rpb * LOG2E, ((0, 0), (3, KROWS_PER_TILE - 1), (0, 0)))
    sel = jnp.einsum("hrc,cn->hrn", rpb_p, jnp.asarray(onehot.reshape(n_co, -1)),
                     precision=lax.Precision.HIGHEST)
    sel = sel.reshape(N_HEADS, n_pad, NGRP, QCOLS, LANES).transpose(0, 2, 1, 3, 4)
    rt = jnp.where(col_ok[None, :, None], sel, NEG)
    rt = rt.reshape(N_HEADS // 2, 2, NGRP, n_pad, QCOLS, LANES)
    delta = np.arange(N_RM)[:, None, None] - WIN_ROWS
    jj2 = (np.arange(LANES) // KCOLS)[None, None, :]
    rm = np.where((jj2 >= delta) & (jj2 < delta + WIN_ROWS), 0.0, NEG)
    rm = np.broadcast_to(rm, (N_RM, QCOLS, LANES)).astype(np.float32)
    return rt.astype(F32), jnp.asarray(rm)


def _tile_indices(rb, i, jq):
    wr = int(np.clip(RB * rb - WIN_ROWS // 2, 0, GRID_ROWS - WROWS))
    r = RB * rb + i
    delta = int(np.clip(r - WIN_ROWS // 2, 0, GRID_ROWS - WIN_ROWS)) - wr - KROWS_PER_TILE * jq
    if not -WIN_ROWS < delta < KROWS_PER_TILE:
        return None
    bti = wr + KROWS_PER_TILE * jq - r + (WIN_ROWS - 1) + 3
    assert 0 <= bti < N_BT and 0 <= delta + WIN_ROWS < N_RM
    return bti, delta + WIN_ROWS


def _attn_body(q_ref, k_ref, v_ref, kc_ref, vc_ref, rt_ref, rm_ref, o_ref, s_scr, bt_scr, bm_scr):
    kc = kc_ref[...]
    vc = vc_ref[...]
    nq = RB * QCOLS
    nk = WROWS * KCOLS
    lane = lax.broadcasted_iota(jnp.int32, (nq, LANES), 1)
    nt = nk // LANES
    nc = CTX // LANES
    dn = (((1,), (1,)), ((), ()))

    @pl.when(pl.program_id(2) == 0)
    def _build_tables():
        lgrp = lax.broadcasted_iota(jnp.int32, (QCOLS, LANES), 1) // KCOLS
        for e in range(2):
            for t in range(N_BT):
                blk = rt_ref[e, t + KROWS_PER_TILE - 1]
                for jj in range(KROWS_PER_TILE - 2, -1, -1):
                    blk = jnp.where(lgrp == jj, rt_ref[e, t + jj], blk)
                bt_scr[e, t] = blk
        for e in range(2):
            for i in range(RB):
                for jq in range(nt):
                    idx = _tile_indices(1, i, jq)
                    if idx is not None:
                        bm_scr[e, i * nt + jq] = bt_scr[e, idx[0]] + rm_ref[idx[1]]

    def window_row(rb):
        return jnp.clip(RB * rb - WIN_ROWS // 2, 0, GRID_ROWS - WROWS)

    def scores(rb, slot):
        rb = jnp.asarray(rb, jnp.int32)
        koff = pl.multiple_of(window_row(rb) * KCOLS, LANES)
        kw = k_ref[pl.ds(koff, nk), :]
        q = q_ref[pl.ds(pl.multiple_of(rb * nq, nq), nq), :]
        for e in range(2):
            in_head = (lane >= HEAD_DIM * e) & (lane < HEAD_DIM * (e + 1))
            qm = jnp.where(in_head, q, jnp.zeros_like(q))
            s_scr[slot, e, :, 0:nk] = lax.dot_general(qm, kw, dn, preferred_element_type=F32)
            s_scr[slot, e, :, nk:] = lax.dot_general(qm, kc, dn, preferred_element_type=F32)

    def finish(rb, slot, edge_rb):
        rb = jnp.asarray(rb, jnp.int32)
        wr = window_row(rb)
        vw = v_ref[pl.ds(pl.multiple_of(wr * KCOLS, LANES), nk), :]
        zero_tile = jnp.zeros((QCOLS, LANES), F32)
        outs = []
        for e in range(2):
            p_rows = []
            l_rows = []
            for i in range(RB):
                rows = slice(i * QCOLS, (i + 1) * QCOLS)
                blks = {}
                for jq in range(nt):
                    idx = _tile_indices(1 if edge_rb is None else edge_rb, i, jq)
                    if idx is None:
                        continue
                    add = bm_scr[e, i * nt + jq] if edge_rb is None else bt_scr[e, idx[0]] + rm_ref[idx[1]]
                    blks[jq] = s_scr[slot, e, rows, jq * LANES:(jq + 1) * LANES] + add
                for jc in range(nc):
                    blks[nt + jc] = s_scr[slot, e, rows, nk + jc * LANES:nk + (jc + 1) * LANES]
                vals = list(blks.values())
                m = jnp.max(functools.reduce(jnp.maximum, vals), axis=-1, keepdims=True)
                ps = {j: jnp.exp2(sb - m) for j, sb in blks.items()}
                l_rows.append(jnp.sum(functools.reduce(jnp.add, list(ps.values())), axis=-1, keepdims=True))
                p_rows.append(jnp.concatenate([ps.get(j, zero_tile) for j in range(nt + nc)], axis=1))
            p = jnp.concatenate(p_rows, axis=0).astype(BF16)
            lsum = jnp.concatenate(l_rows, axis=0)
            o = (jnp.dot(p[:, :nk], vw, preferred_element_type=F32)
                 + jnp.dot(p[:, nk:], vc, preferred_element_type=F32))
            outs.append(o / lsum)
        o = jnp.where(lane < HEAD_DIM, outs[0], outs[1])
        o_ref[pl.ds(pl.multiple_of(rb * nq, nq), nq), :] = o.astype(BF16)

    scores(0, 0)
    scores(1, 1)
    finish(0, 0, 0)
    scores(2, 0)
    finish(1, 1, None)

    def pair(tt, carry):
        t0 = 2 * tt
        scores(t0 + 1, 1)
        finish(t0, 0, None)
        scores(t0 + 2, 0)
        finish(t0 + 1, 1, None)
        return carry

    lax.fori_loop(1, NRB // 2 - 1, pair, 0, unroll=True)
    scores(NRB - 1, 1)
    finish(NRB - 2, 0, None)
    finish(NRB - 1, 1, NRB - 1)


def _attention(qcb, kcb, vcb, kc, vc, bt, rm, casts=()):
    bsz = qcb.shape[0]
    nhp = N_HEADS // 2
    n_t = bt.shape[3]
    c_in, c_out, c_shape = _cast_jobs(casts, nhp * NGRP * bsz, lambda h, g, b: (h * NGRP + g) * bsz + b)
    return pl.pallas_call(
        _with_casts(_attn_body, 7, 1, len(casts)),
        grid=(nhp, NGRP, bsz),
        in_specs=[pl.BlockSpec((None, None, GRID_ROWS * QCOLS, LANES), lambda h, g, b: (b, g, 0, h)),
                  pl.BlockSpec((None, None, GRID_ROWS * KCOLS, LANES), lambda h, g, b: (b, g, 0, h)),
                  pl.BlockSpec((None, None, GRID_ROWS * KCOLS, LANES), lambda h, g, b: (b, g, 0, h)),
                  pl.BlockSpec((None, CTX, LANES), lambda h, g, b: (b, 0, h)),
                  pl.BlockSpec((None, CTX, LANES), lambda h, g, b: (b, 0, h)),
                  pl.BlockSpec((None, 2, None, n_t, QCOLS, LANES), lambda h, g, b: (h, 0, g, 0, 0, 0)),
                  _const_spec((N_RM, QCOLS, LANES))] + c_in,
        out_specs=[pl.BlockSpec((None, None, GRID_ROWS * QCOLS, LANES), lambda h, g, b: (b, g, 0, h))] + c_out,
        out_shape=[jax.ShapeDtypeStruct((bsz, NGRP, GRID_ROWS * QCOLS, D), BF16)] + c_shape,
        scratch_shapes=[pltpu.VMEM((2, 2, RB * QCOLS, WROWS * KCOLS + CTX), F32),
                        pltpu.VMEM((2, N_BT, QCOLS, LANES), F32),
                        pltpu.VMEM((2, RB * WROWS * KCOLS // LANES, QCOLS, LANES), F32)],
        compiler_params=_cparams(3),
        name="nbr_attention",
    )(qcb, kcb, vcb, kc, vc, bt, rm, *[a for a, _ in casts])


def _attnout_mlp_body(x_ref, o_ref_in, mod_ref, g2_ref, wo_ref, w1_ref, w2_ref, out_ref):
    b = pl.program_id(0)
    gate1, shift2, scale2, gate2 = _mod_rows(mod_ref, b, (2, 3, 4, 5))
    chunks = [o_ref_in[g, rho * QCOLS:(rho + 1) * QCOLS, :]
              for rho in range(TM_MLP // GRID_W) for g in range(NGRP)]
    o_nat = jnp.concatenate(chunks, axis=0)
    mo = jnp.dot(o_nat, wo_ref[...], preferred_element_type=F32)
    x1 = x_ref[...] + gate1 * mo
    out_ref[...] = x1 + gate2 * _mlp(x1, g2_ref[...], shift2, scale2, w1_ref, w2_ref)


def _attnout_mlp(x, ocb, mod, g2, wo_bf, w1_bf, w2_bf):
    bsz = x.shape[0]
    nq = TM_MLP // GRID_W * QCOLS
    return pl.pallas_call(
        _attnout_mlp_body,
        grid=(bsz, SEQ // TM_MLP),
        in_specs=[pl.BlockSpec((None, TM_MLP, D), lambda b, t: (b, t, 0)),
                  pl.BlockSpec((None, NGRP, nq, D), lambda b, t: (b, 0, t, 0)),
                  _const_spec((8, N_MOD * D)), _const_spec((1, D)),
                  _resident_spec((D, D)), _resident_spec((D, D_FF)), _resident_spec((D_FF, D))],
        out_specs=pl.BlockSpec((None, TM_MLP, D), lambda b, t: (b, t, 0)),
        out_shape=jax.ShapeDtypeStruct((bsz, SEQ, D), F32),
        compiler_params=_cparams(2),
        name="attn_out_mlp",
    )(x, ocb, mod, g2, wo_bf, w1_bf, w2_bf)


def kernel(x, c, ctx, c_ctx, ada_w, ada_b, norm_mix_g, norm_mlp_g, mlp_w1, mlp_w2, ab_w_in, ab_conv_w,
           ab_w_out, na_w_qkv, na_q_g, na_k_g, na_rpb, na_w_out):
    bsz = x.shape[0]
    ctx_row = bsz
    cond8 = jnp.zeros((8, D), F32).at[:bsz].set(c).at[ctx_row].set(c_ctx)
    mods, w_in = _modulation(cond8, ada_w, ada_b, casts=((ab_w_in, 0),))

    g_mix = norm_mix_g.reshape(DEPTH, 1, D)
    g_mlp = norm_mlp_g.reshape(DEPTH, 1, D)
    conv_w8 = jnp.zeros((8, CONV_CH), F32).at[:3].set(ab_conv_w[0])

    ay, f, w1_0, w2_0, w_out0, wqkv, w1_1, w2_1, wo = _inproj(
        x, mods[0], g_mix[0], w_in, conv_w8, None, TM_IN,
        casts=((mlp_w1, 0), (mlp_w2, 0), (ab_w_out, 0), (na_w_qkv, 0), (mlp_w1, 1), (mlp_w2, 1), (na_w_out, 0)))
    fy, = _dft_2d(f, _bf16_const(_channel_mix_const(SEQ)))
    x, = _mixout_mlp(x, ay, fy, mods[0], g_mlp[0], w_out0, w1_0, w2_0, None, TM_MLP)

    ay_c, fy_c = _ctx_in(ctx, mods[0], g_mix[0], w_in, conv_w8, _bf16_const(_channel_mix_const(CTX)), ctx_row)
    flat = lambda a: a.reshape(1, bsz * CTX, a.shape[-1])
    kc, vc = _ctx_out_kv(flat(ctx), flat(ay_c), flat(fy_c), mods[0], g_mlp[0], w_out0, w1_0, w2_0,
                         mods[1], g_mix[1], wqkv, na_q_g[0], na_k_g[0], ctx_row)
    kc, vc = kc.reshape(bsz, CTX, D), vc.reshape(bsz, CTX, D)

    qcb, kcb, vcb = _qkv_grid(x, mods[1], g_mix[1], wqkv, na_q_g[0], na_k_g[0])
    bt, rm = _bias_tables(na_rpb[0])
    ocb, = _attention(qcb, kcb, vcb, kc, vc, bt, rm)
    x = _attnout_mlp(x, ocb, mods[1], g_mlp[1], wo, w1_1, w2_1)
    return x
```
